```python
import math
import jax, jax.numpy as jnp
from jax import lax
import numpy as np

D_MODEL = 4096
BATCH = 2
SEQ = 8192
DEPTH = 2

HEAD_DIM = 128
Q_BLOCK = 128
ROPE_THETA = 10000.0
RMS_EPS = 1e-6
NEG_BIG = -1e30

SB_HEADS = 8
SB_W = SB_HEADS * HEAD_DIM
DIL_GROUPS = ((128, 1), (512, 4), (2048, 16))
DIL_HEADS_PER_GROUP = 4
DIL_HEADS = DIL_HEADS_PER_GROUP * len(DIL_GROUPS)
DIL_W = DIL_HEADS * HEAD_DIM
DIL_OUT_W = DIL_HEADS_PER_GROUP * HEAD_DIM
DIFF_HEADS = 8
DIFF_DIM = 64
DIFF_QK_W = DIFF_HEADS * 2 * DIFF_DIM
DIFF_V_W = DIFF_HEADS * 2 * DIFF_DIM

QKV_SPLITS = (SB_W, SB_W, SB_W, DIL_W, DIL_W, DIL_W, DIFF_QK_W, DIFF_QK_W, DIFF_V_W)
QKV_COLS = 3 * SB_W + 3 * DIL_W + 2 * DIFF_QK_W + DIFF_V_W
N_BRANCH = 3

PEER_HEADS = 8
PEER_NKEYS = 128
PEER_EXPERTS = PEER_NKEYS * PEER_NKEYS
PEER_QDIM = 256
PEER_TOPK = 16
PEER_TOKEN_BLOCK = 128

kernel_name = "hybrid_sb_dilated_diff_peer"

F32 = jnp.float32


def rms_norm(x, g):
    xf = x.astype(F32)
    y = xf * lax.rsqrt(jnp.mean(xf * xf, axis=-1, keepdims=True) + RMS_EPS)
    return (y * g.astype(F32)).astype(x.dtype)


def rope_tables(seq_len, dim):
    inv_freq = 1.0 / (ROPE_THETA ** (jnp.arange(0, dim, 2, dtype=F32) / dim))
    ang = jnp.arange(seq_len, dtype=F32)[:, None] * inv_freq[None, :]
    ang = jnp.concatenate([ang, ang], axis=-1)
    return jnp.cos(ang), jnp.sin(ang)


def apply_rope(x, cos, sin):
    xf = x.astype(F32)
    x1, x2 = jnp.split(xf, 2, axis=-1)
    rot = jnp.concatenate([-x2, x1], axis=-1)
    return xf * cos[None, :, None, :] + rot * sin[None, :, None, :]


def split_columns(proj, sizes):
    out = []
    off = 0
    for n in sizes:
        out.append(proj[..., off:off + n])
        off += n
    return out


def stick_breaking_attention(q, k, v):
    b, s, h, dh = q.shape
    nb = s // Q_BLOCK
    scale = dh ** -0.5
    kf = k.astype(F32)
    vf = v.astype(F32)
    key_pos = jnp.arange(s)

    def block(i):
        t0 = i * Q_BLOCK
        qb = lax.dynamic_slice_in_dim(q, t0, Q_BLOCK, axis=1).astype(F32)
        z = jnp.einsum('bqhd,bkhd->bhqk', qb, kf) * scale
        qpos = t0 + jnp.arange(Q_BLOCK)
        before = key_pos[None, :] < qpos[:, None]
        neg_log_keep = jnp.where(before, jax.nn.softplus(z), 0.0)
        later = lax.cumsum(neg_log_keep, axis=3, reverse=True) - neg_log_keep
        w = jnp.where(before, jnp.exp(jax.nn.log_sigmoid(z) - later), 0.0)
        return jnp.einsum('bhqk,bkhd->bqhd', w, vf)

    out = lax.map(block, jnp.arange(nb))
    return out.transpose(1, 0, 2, 3, 4).reshape(b, s, h * dh)


def dilated_window_attention(q, k, v):
    b, s, _, dh = q.shape
    nb = s // Q_BLOCK
    scale = dh ** -0.5
    hg = DIL_HEADS_PER_GROUP
    q_groups = [q[:, :, g * hg:(g + 1) * hg] for g in range(len(DIL_GROUPS))]
    k_groups = [k[:, :, g * hg:(g + 1) * hg].astype(F32) for g in range(len(DIL_GROUPS))]
    v_groups = [v[:, :, g * hg:(g + 1) * hg].astype(F32) for g in range(len(DIL_GROUPS))]

    def block(i):
        t0 = i * Q_BLOCK
        qpos = t0 + jnp.arange(Q_BLOCK)
        outs = []
        lses = []
        for g, (window, dil) in enumerate(DIL_GROUPS):
            n_keys = window // dil + 1
            kpos = qpos[:, None] - dil * jnp.arange(n_keys)[None, :]
            valid = kpos >= 0
            kidx = jnp.maximum(kpos, 0)
            qb = lax.dynamic_slice_in_dim(q_groups[g], t0, Q_BLOCK, axis=1).astype(F32)
            kg = jnp.take(k_groups[g], kidx, axis=1)
            vg = jnp.take(v_groups[g], kidx, axis=1)
            sc = jnp.einsum('bqhd,bqmhd->bhqm', qb, kg) * scale
            sc = jnp.where(valid[None, None], sc, NEG_BIG)
            m = jnp.max(sc, axis=-1, keepdims=True)
            p = jnp.exp(sc - m)
            l = jnp.sum(p, axis=-1, keepdims=True)
            o = jnp.einsum('bhqm,bqmhd->bqhd', p / l, vg)
            lse = (m + jnp.log(l))[..., 0].transpose(0, 2, 1)
            outs.append(o)
            lses.append(lse)
        o_all = jnp.stack(outs, axis=0)
        alpha = jax.nn.softmax(jnp.stack(lses, axis=0), axis=0)
        return jnp.sum(alpha[..., None] * o_all, axis=0)

    out = lax.map(block, jnp.arange(nb))
    return out.transpose(1, 0, 2, 3, 4).reshape(b, s, hg * dh)


def differential_attention(q, k, v, lam):
    b, s, h2, d = q.shape
    hd = h2 // 2
    nb = s // Q_BLOCK
    scale = d ** -0.5
    kf = k.astype(F32)
    vf = v.astype(F32)
    key_pos = jnp.arange(s)

    def block(i):
        t0 = i * Q_BLOCK
        qb = lax.dynamic_slice_in_dim(q, t0, Q_BLOCK, axis=1).astype(F32)
        sc = jnp.einsum('bqhd,bkhd->bhqk', qb, kf) * scale
        qpos = t0 + jnp.arange(Q_BLOCK)
        causal = key_pos[None, :] <= qpos[:, None]
        a = jax.nn.softmax(jnp.where(causal, sc, NEG_BIG), axis=-1)
        a = a.reshape(b, hd, 2, Q_BLOCK, s)
        w = a[:, :, 0] - lam * a[:, :, 1]
        return jnp.einsum('bhqk,bkhd->bqhd', w, vf)

    out = lax.map(block, jnp.arange(nb))
    return out.transpose(1, 0, 2, 3, 4).reshape(b, s, hd, 2 * d)


def peer_ffn(h, w_q, sub_keys, u_tab, v_tab):
    b, s, d = h.shape
    n_tok = b * s
    t = h.reshape(n_tok, d)
    q = (t @ w_q).astype(F32).reshape(n_tok, PEER_HEADS, 2, PEER_QDIM // 2)
    sc = jnp.einsum('thcd,hcnd->thcn', q, sub_keys.astype(F32))
    s_top, i_top = lax.top_k(sc, PEER_TOPK)
    cand = s_top[:, :, 0, :, None] + s_top[:, :, 1, None, :]
    cand_idx = i_top[:, :, 0, :, None] * PEER_NKEYS + i_top[:, :, 1, None, :]
    best, pos = lax.top_k(cand.reshape(n_tok, PEER_HEADS, PEER_TOPK * PEER_TOPK), PEER_TOPK)
    experts = jnp.take_along_axis(
        cand_idx.reshape(n_tok, PEER_HEADS, PEER_TOPK * PEER_TOPK), pos, axis=-1)
    gates = jax.nn.softmax(best, axis=-1)
    nblk = n_tok // PEER_TOKEN_BLOCK
    n_sel = PEER_HEADS * PEER_TOPK
    experts = experts.reshape(nblk, PEER_TOKEN_BLOCK, n_sel)
    gates = gates.reshape(nblk, PEER_TOKEN_BLOCK, n_sel)
    tb = t.reshape(nblk, PEER_TOKEN_BLOCK, d)

    def block(args):
        xb, eb, gb = args
        u = jnp.take(u_tab, eb, axis=0).astype(F32)
        a = jnp.einsum('td,ted->te', xb.astype(F32), u)
        coef = gb * jax.nn.gelu(a, approximate=False)
        vv = jnp.take(v_tab, eb, axis=0).astype(F32)
        return jnp.einsum('te,ted->td', coef, vv)

    out = lax.map(block, (tb, experts, gates))
    return out.reshape(b, s, d).astype(h.dtype)


def setup_inputs(seed: int = 0) -> dict:
    key = jax.random.key(seed)
    ks = jax.random.split(key, 17)
    L, D = DEPTH, D_MODEL

    def nrm(k, shape, scale):
        return jax.random.normal(k, shape, F32) * scale

    return {
        "x": nrm(ks[0], (BATCH, SEQ, D), 1.0),
        "attn_norm_g": 1.0 + nrm(ks[1], (L, D), 0.05),
        "ffn_norm_g": 1.0 + nrm(ks[2], (L, D), 0.05),
        "final_norm_g": 1.0 + nrm(ks[3], (D,), 0.05),
        "w_qkv": nrm(ks[4], (L, D, QKV_COLS), D ** -0.5),
        "w_gate": nrm(ks[5], (L, D, N_BRANCH * D), D ** -0.5),
        "w_branch_sb": nrm(ks[6], (L, SB_W, D), SB_W ** -0.5),
        "w_branch_dil": nrm(ks[7], (L, DIL_OUT_W, D), DIL_OUT_W ** -0.5),
        "w_branch_diff": nrm(ks[8], (L, DIFF_V_W, D), DIFF_V_W ** -0.5),
        "w_out": nrm(ks[9], (L, D, D), D ** -0.5),
        "diff_lambda": nrm(ks[10], (L, 4, DIFF_DIM), 0.1),
        "diff_subln_g": 1.0 + nrm(ks[11], (L, 2 * DIFF_DIM), 0.05),
        "peer_w_q": nrm(ks[12], (L, D, PEER_HEADS * PEER_QDIM), D ** -0.5),
        "peer_sub_keys": nrm(ks[13], (L, PEER_HEADS, 2, PEER_NKEYS, PEER_QDIM // 2), (PEER_QDIM // 2) ** -0.5),
        "peer_u": nrm(ks[14], (L, PEER_EXPERTS, D), D ** -0.5),
        "peer_v": nrm(ks[15], (L, PEER_EXPERTS, D), PEER_HEADS ** -0.5),
    }


def reference(x, attn_norm_g, ffn_norm_g, final_norm_g, w_qkv, w_gate, w_branch_sb, w_branch_dil,
              w_branch_diff, w_out, diff_lambda, diff_subln_g, peer_w_q, peer_sub_keys, peer_u, peer_v):
    b, s, _ = x.shape
    cos_h, sin_h = rope_tables(s, HEAD_DIM)
    cos_d, sin_d = rope_tables(s, DIFF_DIM)
    for layer in range(DEPTH):
        h = rms_norm(x, attn_norm_g[layer])
        proj = h @ w_qkv[layer]
        q_sb, k_sb, v_sb, q_dl, k_dl, v_dl, q_df, k_df, v_df = split_columns(proj, QKV_SPLITS)
        gates = jax.nn.sigmoid(h @ w_gate[layer]).reshape(b, s, N_BRANCH, D_MODEL)

        o_sb = stick_breaking_attention(
            q_sb.reshape(b, s, SB_HEADS, HEAD_DIM),
            k_sb.reshape(b, s, SB_HEADS, HEAD_DIM),
            v_sb.reshape(b, s, SB_HEADS, HEAD_DIM)).astype(x.dtype)

        o_dl = dilated_window_attention(
            apply_rope(q_dl.reshape(b, s, DIL_HEADS, HEAD_DIM), cos_h, sin_h),
            apply_rope(k_dl.reshape(b, s, DIL_HEADS, HEAD_DIM), cos_h, sin_h),
            v_dl.reshape(b, s, DIL_HEADS, HEAD_DIM)).astype(x.dtype)

        lam_init = 0.8 - 0.6 * math.exp(-0.3 * layer)
        lp = diff_lambda[layer].astype(F32)
        lam = jnp.exp(jnp.sum(lp[0] * lp[1])) - jnp.exp(jnp.sum(lp[2] * lp[3])) + lam_init
        o_df = differential_attention(
            apply_rope(q_df.reshape(b, s, 2 * DIFF_HEADS, DIFF_DIM), cos_d, sin_d),
            apply_rope(k_df.reshape(b, s, 2 * DIFF_HEADS, DIFF_DIM), cos_d, sin_d),
            v_df.reshape(b, s, DIFF_HEADS, 2 * DIFF_DIM), lam)
        o_df = (rms_norm(o_df, diff_subln_g[layer]) * (1.0 - lam_init)).reshape(b, s, DIFF_V_W).astype(x.dtype)

        merged = (gates[:, :, 0] * (o_sb @ w_branch_sb[layer])
                  + gates[:, :, 1] * (o_dl @ w_branch_dil[layer])
                  + gates[:, :, 2] * (o_df @ w_branch_diff[layer]))
        x = x + (merged @ w_out[layer]).astype(x.dtype)

        h2 = rms_norm(x, ffn_norm_g[layer])
        x = x + peer_ffn(h2, peer_w_q[layer], peer_sub_keys[layer], peer_u[layer], peer_v[layer])
    return rms_norm(x, final_norm_g)
```

```python
import functools
import math

import jax
import jax.numpy as jnp
from jax import lax
from jax.experimental import pallas as pl
from jax.experimental.pallas import tpu as pltpu

F32 = jnp.float32
BF16 = jnp.bfloat16

HEAD_DIM = 128
ROPE_THETA = 10000.0
RMS_EPS = 1e-6
NEG_BIG = -1e30

SB_HEADS = 8
DIL_GROUPS = ((128, 1), (512, 4), (2048, 16))
DIL_HEADS_PER_GROUP = 4
DIFF_HEADS = 8
DIFF_DIM = 64
N_BRANCH = 3

PEER_HEADS = 8
PEER_NKEYS = 128
PEER_HALF_QDIM = 128
PEER_TOPK = 16

LANES = 128
VMEM_LIMIT = 56 * 1024 * 1024


def _params(sem, vmem=VMEM_LIMIT):
    return pltpu.CompilerParams(dimension_semantics=sem, vmem_limit_bytes=vmem)


def _rmsnorm_kernel(x_ref, g_ref, o_ref):
    x = x_ref[...]
    ms = jnp.mean(x * x, axis=-1, keepdims=True)
    o_ref[...] = (x * lax.rsqrt(ms + RMS_EPS) * g_ref[...]).astype(o_ref.dtype)


def rmsnorm(x, g, out_dtype, rows=256):
    t, d = x.shape
    rows = min(rows, t)
    return pl.pallas_call(
        _rmsnorm_kernel,
        grid=(t // rows,),
        in_specs=[pl.BlockSpec((rows, d), lambda i: (i, 0)), pl.BlockSpec((1, d), lambda i: (0, 0))],
        out_specs=pl.BlockSpec((rows, d), lambda i: (i, 0)),
        out_shape=jax.ShapeDtypeStruct((t, d), out_dtype),
        compiler_params=_params(("parallel",)),
        name="rmsnorm",
    )(x, g.reshape(1, d))


def _rope_tables(seq_len, dim):
    inv_freq = 1.0 / (ROPE_THETA ** (jnp.arange(0, dim, 2, dtype=F32) / dim))
    ang = jnp.arange(seq_len, dtype=F32)[:, None] * inv_freq[None, :]
    ang = jnp.concatenate([ang, ang], axis=-1)
    return jnp.cos(ang), jnp.sin(ang)


def _rope_operands(seq_len):
    cos_h, sin_h = _rope_tables(seq_len, HEAD_DIM)
    lane = jnp.arange(LANES)
    sin_h_signed = jnp.where(lane < HEAD_DIM // 2, -sin_h, sin_h)
    cos_d, sin_d = _rope_tables(seq_len, DIFF_DIM)
    cos_d2 = jnp.concatenate([cos_d, cos_d], axis=-1)
    sin_d2 = jnp.concatenate([sin_d, sin_d], axis=-1)
    low = (lane % DIFF_DIM) < DIFF_DIM // 2
    sin_d_low = jnp.where(low, -sin_d2, 0.0)
    sin_d_high = jnp.where(low, 0.0, sin_d2)
    return cos_h, sin_h_signed, cos_d2, sin_d_low, sin_d_high


def _qkv_kernel(a_ref, w_ref, cos_h, sin_h, cos_d, sin_dl, sin_dh, o_ref, *, rope128_blocks, rope64_blocks):
    j = pl.program_id(1)
    acc = jnp.dot(a_ref[...], w_ref[...], preferred_element_type=F32)
    n_chunks = acc.shape[1] // LANES
    in128 = (j >= rope128_blocks[0]) & (j < rope128_blocks[1])
    in64 = (j >= rope64_blocks[0]) & (j < rope64_blocks[1])

    @pl.when(in128)
    def _():
        for c in range(n_chunks):
            x = acc[:, c * LANES:(c + 1) * LANES]
            y = x * cos_h[...] + pltpu.roll(x, HEAD_DIM // 2, 1) * sin_h[...]
            o_ref[:, c * LANES:(c + 1) * LANES] = y.astype(o_ref.dtype)

    @pl.when(in64)
    def _():
        for c in range(n_chunks):
            x = acc[:, c * LANES:(c + 1) * LANES]
            y = (x * cos_d[...] + pltpu.roll(x, LANES - DIFF_DIM // 2, 1) * sin_dl[...]
                 + pltpu.roll(x, DIFF_DIM // 2, 1) * sin_dh[...])
            o_ref[:, c * LANES:(c + 1) * LANES] = y.astype(o_ref.dtype)

    @pl.when(jnp.logical_not(in128 | in64))
    def _():
        o_ref[...] = acc.astype(o_ref.dtype)


def qkv_projection(h, w, rope_ops, seq_len, rope128_cols, rope64_cols, tm=1024, tn=512):
    m, k = h.shape
    n = w.shape[1]
    tm = min(tm, seq_len)
    assert seq_len % tm == 0 and m % tm == 0 and n % tn == 0
    for lo, hi in (rope128_cols, rope64_cols):
        assert lo % tn == 0 and hi % tn == 0
    seq_blocks = seq_len // tm
    tab_spec = pl.BlockSpec((tm, LANES), lambda i, j: (i % seq_blocks, 0))
    kern = functools.partial(
        _qkv_kernel,
        rope128_blocks=(rope128_cols[0] // tn, rope128_cols[1] // tn),
        rope64_blocks=(rope64_cols[0] // tn, rope64_cols[1] // tn))
    return pl.pallas_call(
        kern,
        grid=(m // tm, n // tn),
        in_specs=[pl.BlockSpec((tm, k), lambda i, j: (i, 0)),
                  pl.BlockSpec((k, tn), lambda i, j: (0, j)),
                  tab_spec, tab_spec, tab_spec, tab_spec, tab_spec],
        out_specs=pl.BlockSpec((tm, tn), lambda i, j: (i, j)),
        out_shape=jax.ShapeDtypeStruct((m, n), BF16),
        compiler_params=_params(("parallel", "arbitrary")),
        name="qkv_projection",
    )(h, w, *rope_ops)


def _matmul_kernel(a_ref, w_ref, o_ref):
    o_ref[...] = jnp.dot(a_ref[...], w_ref[...], preferred_element_type=F32).astype(o_ref.dtype)


def _matmul_residual_kernel(a_ref, w_ref, r_ref, o_ref):
    o_ref[...] = r_ref[...] + jnp.dot(a_ref[...], w_ref[...], preferred_element_type=F32)


def matmul(a, w, out_dtype, residual=None, tm=1024, tn=512):
    m, k = a.shape
    n = w.shape[1]
    tm, tn = min(tm, m), min(tn, n)
    assert m % tm == 0 and n % tn == 0
    in_specs = [pl.BlockSpec((tm, k), lambda i, j: (i, 0)), pl.BlockSpec((k, tn), lambda i, j: (0, j))]
    args = [a, w]
    kern = _matmul_kernel
    if residual is not None:
        in_specs.append(pl.BlockSpec((tm, tn), lambda i, j: (i, j)))
        args.append(residual)
        kern = _matmul_residual_kernel
    return pl.pallas_call(
        kern,
        grid=(m // tm, n // tn),
        in_specs=in_specs,
        out_specs=pl.BlockSpec((tm, tn), lambda i, j: (i, j)),
        out_shape=jax.ShapeDtypeStruct((m, n), out_dtype),
        compiler_params=_params(("parallel", "arbitrary")),
        name="matmul_residual" if residual is not None else "matmul",
    )(*args)


def _softplus(z):
    return jnp.maximum(z, 0.0) + jnp.log(1.0 + jnp.exp(-jnp.abs(z)))


def _sb_kernel(q_ref, k_ref, v_ref, o_ref, *, tq, scale):
    i = pl.program_id(2)
    q = q_ref[...]
    row = lax.broadcasted_iota(jnp.int32, (tq, tq), 0)
    col = lax.broadcasted_iota(jnp.int32, (tq, tq), 1)
    suffix = (row > col).astype(BF16)
    before = col < row

    def block(kb, carry, acc, diagonal):
        start = pl.multiple_of(kb * tq, tq)
        k = k_ref[pl.ds(start, tq), :]
        v = v_ref[pl.ds(start, tq), :]
        z = lax.dot_general(q, k, (((1,), (1,)), ((), ())), preferred_element_type=F32) * scale
        sp = _softplus(z)
        spm = jnp.where(before, sp, 0.0) if diagonal else sp
        later = jnp.dot(spm.astype(BF16), suffix, preferred_element_type=F32)
        w = jnp.exp(z - sp - later - carry)
        if diagonal:
            w = jnp.where(before, w, 0.0)
        acc = acc + jnp.dot(w.astype(BF16), v, preferred_element_type=F32)
        carry = carry + jnp.sum(spm, axis=1, keepdims=True)
        return carry, acc

    carry0 = jnp.zeros((tq, 1), F32)
    acc0 = jnp.zeros((tq, HEAD_DIM), F32)
    carry, acc = block(i, carry0, acc0, True)

    def body(n, state):
        return block(i - 1 - n, state[0], state[1], False)

    carry, acc = lax.fori_loop(0, i, body, (carry, acc))
    o_ref[...] = acc.astype(o_ref.dtype)


def stick_breaking_attention(proj, seq_len, col0, tq=256):
    b = proj.shape[0]
    tq = min(tq, seq_len)
    c0 = col0 // HEAD_DIM
    kern = functools.partial(_sb_kernel, tq=tq, scale=HEAD_DIM ** -0.5)
    return pl.pallas_call(
        kern,
        grid=(b, SB_HEADS, seq_len // tq),
        in_specs=[pl.BlockSpec((None, tq, HEAD_DIM), lambda bi, h, i: (bi, i, c0 + h)),
                  pl.BlockSpec((None, seq_len, HEAD_DIM), lambda bi, h, i: (bi, 0, c0 + SB_HEADS + h)),
                  pl.BlockSpec((None, seq_len, HEAD_DIM), lambda bi, h, i: (bi, 0, c0 + 2 * SB_HEADS + h))],
        out_specs=pl.BlockSpec((None, tq, HEAD_DIM), lambda bi, h, i: (bi, i, h)),
        out_shape=jax.ShapeDtypeStruct((b, seq_len, SB_HEADS * HEAD_DIM), BF16),
        compiler_params=_params(("parallel", "parallel", "arbitrary")),
        name="stick_breaking_attention",
    )(proj, proj, proj)


def _dil_kernel(q_ref, kp_ref, kc_ref, vp_ref, vc_ref, o_ref, lse_ref, *, tq, scale):
    i = pl.program_id(2)
    row = lax.broadcasted_iota(jnp.int32, (tq, tq), 0)
    col = lax.broadcasted_iota(jnp.int32, (tq, tq), 1)
    cur_ok = col <= row
    prev_ok = col >= row + jnp.where(i > 0, 0, tq)
    dims = (((1,), (1,)), ((), ()))
    for h in range(DIL_HEADS_PER_GROUP):
        sl = slice(h * HEAD_DIM, (h + 1) * HEAD_DIM)
        q = q_ref[:, sl]
        s_cur = lax.dot_general(q, kc_ref[:, sl], dims, preferred_element_type=F32) * scale
        s_prev = lax.dot_general(q, kp_ref[:, sl], dims, preferred_element_type=F32) * scale
        s_cur = jnp.where(cur_ok, s_cur, NEG_BIG)
        s_prev = jnp.where(prev_ok, s_prev, NEG_BIG)
        m = jnp.maximum(jnp.max(s_cur, axis=1, keepdims=True), jnp.max(s_prev, axis=1, keepdims=True))
        p_cur = jnp.exp(s_cur - m)
        p_prev = jnp.exp(s_prev - m)
        l = jnp.sum(p_cur, axis=1, keepdims=True) + jnp.sum(p_prev, axis=1, keepdims=True)
        o = (jnp.dot(p_cur.astype(BF16), vc_ref[:, sl], preferred_element_type=F32)
             + jnp.dot(p_prev.astype(BF16), vp_ref[:, sl], preferred_element_type=F32))
        o_ref[:, sl] = o / l
        lse_ref[:, sl] = jnp.broadcast_to(m + jnp.log(l), (tq, HEAD_DIM))


def dilated_group_attention(proj, seq_len, group, dilation, q_col, k_col, v_col):
    b, _, cols = proj.shape
    gw = DIL_HEADS_PER_GROUP * HEAD_DIM
    tq = DIL_GROUPS[group][0] // dilation
    sub_len = seq_len // dilation
    assert sub_len % tq == 0 and cols % gw == 0
    view = proj.reshape(b, sub_len, dilation * cols)
    per_row = cols // gw
    qb, kb, vb = (q_col // gw + group, k_col // gw + group, v_col // gw + group)
    prev = lambda i: jnp.maximum(i - 1, 0)
    out_sds = jax.ShapeDtypeStruct((b, sub_len, dilation * gw), F32)
    kern = functools.partial(_dil_kernel, tq=tq, scale=HEAD_DIM ** -0.5)
    o, lse = pl.pallas_call(
        kern,
        grid=(b, dilation, sub_len // tq),
        in_specs=[pl.BlockSpec((None, tq, gw), lambda bi, c, i: (bi, i, c * per_row + qb)),
                  pl.BlockSpec((None, tq, gw), lambda bi, c, i: (bi, prev(i), c * per_row + kb)),
                  pl.BlockSpec((None, tq, gw), lambda bi, c, i: (bi, i, c * per_row + kb)),
                  pl.BlockSpec((None, tq, gw), lambda bi, c, i: (bi, prev(i), c * per_row + vb)),
                  pl.BlockSpec((None, tq, gw), lambda bi, c, i: (bi, i, c * per_row + vb))],
        out_specs=[pl.BlockSpec((None, tq, gw), lambda bi, c, i: (bi, i, c)),
                   pl.BlockSpec((None, tq, gw), lambda bi, c, i: (bi, i, c))],
        out_shape=[out_sds, out_sds],
        compiler_params=_params(("parallel", "parallel", "arbitrary")),
        name=f"dilated_attention_g{group}",
    )(view, view, view, view, view)
    return o.reshape(b * seq_len, gw), lse.reshape(b * seq_len, gw)


def _dil_merge_kernel(o0, o1, o2, l0, l1, l2, out_ref):
    m = jnp.maximum(jnp.maximum(l0[...], l1[...]), l2[...])
    e0, e1, e2 = jnp.exp(l0[...] - m), jnp.exp(l1[...] - m), jnp.exp(l2[...] - m)
    out_ref[...] = ((e0 * o0[...] + e1 * o1[...] + e2 * o2[...]) / (e0 + e1 + e2)).astype(out_ref.dtype)


def dilated_merge(outs, lses, rows=512):
    t, w = outs[0].shape
    rows = min(rows, t)
    spec = pl.BlockSpec((rows, w), lambda i: (i, 0))
    return pl.pallas_call(
        _dil_merge_kernel,
        grid=(t // rows,),
        in_specs=[spec] * 6,
        out_specs=spec,
        out_shape=jax.ShapeDtypeStruct((t, w), BF16),
        compiler_params=_params(("parallel",)),
        name="dilated_merge",
    )(*outs, *lses)


def _diff_kernel(lam_ref, g_ref, q_ref, k_ref, v_ref, o_ref, *, tq, scale, lam_init):
    i = pl.program_id(2)
    lp = lam_ref[...]
    lam = (jnp.exp(jnp.sum(lp[0:1] * lp[1:2], axis=1, keepdims=True))
           - jnp.exp(jnp.sum(lp[2:3] * lp[3:4], axis=1, keepdims=True)) + lam_init)
    q = q_ref[...].astype(F32)
    lane = lax.broadcasted_iota(jnp.int32, (tq, LANES), 1)
    qq = jnp.concatenate([jnp.where(lane < DIFF_DIM, q, 0.0), jnp.where(lane >= DIFF_DIM, q, 0.0)],
                         axis=0).astype(BF16)
    row = lax.broadcasted_iota(jnp.int32, (2 * tq, tq), 0)
    row = jnp.where(row >= tq, row - tq, row)
    col = lax.broadcasted_iota(jnp.int32, (2 * tq, tq), 1)
    causal = col <= row

    def block(kb, m, l, acc, diagonal):
        start = pl.multiple_of(kb * tq, tq)
        k = k_ref[pl.ds(start, tq), :]
        v = v_ref[pl.ds(start, tq), :]
        s = lax.dot_general(qq, k, (((1,), (1,)), ((), ())), preferred_element_type=F32) * scale
        if diagonal:
            s = jnp.where(causal, s, NEG_BIG)
        m_new = jnp.maximum(m, jnp.max(s, axis=1, keepdims=True))
        alpha = jnp.exp(m - m_new)
        p = jnp.exp(s - m_new)
        l = alpha * l + jnp.sum(p, axis=1, keepdims=True)
        acc = alpha * acc + jnp.dot(p.astype(BF16), v, preferred_element_type=F32)
        return m_new, l, acc

    m0 = jnp.full((2 * tq, 1), NEG_BIG, F32)
    l0 = jnp.zeros((2 * tq, 1), F32)
    acc0 = jnp.zeros((2 * tq, LANES), F32)
    m, l, acc = block(i, m0, l0, acc0, True)

    def body(n, state):
        return block(n, state[0], state[1], state[2], False)

    m, l, acc = lax.fori_loop(0, i, body, (m, l, acc))
    o_all = acc / l
    o = o_all[:tq] - lam * o_all[tq:]
    ms = jnp.mean(o * o, axis=-1, keepdims=True)
    y = o * lax.rsqrt(ms + RMS_EPS) * g_ref[...]
    o_ref[...] = (y * (1.0 - lam_init)).astype(o_ref.dtype)


def differential_attention(proj, lam_params, subln_g, seq_len, q_col, k_col, v_col, lam_init, tq=256):
    b = proj.shape[0]
    tq = min(tq, seq_len)
    qb, kb, vb = q_col // LANES, k_col // LANES, v_col // LANES
    kern = functools.partial(_diff_kernel, tq=tq, scale=DIFF_DIM ** -0.5, lam_init=lam_init)
    return pl.pallas_call(
        kern,
        grid=(b, DIFF_HEADS, seq_len // tq),
        in_specs=[pl.BlockSpec((4, DIFF_DIM), lambda bi, h, i: (0, 0)),
                  pl.BlockSpec((1, 2 * DIFF_DIM), lambda bi, h, i: (0, 0)),
                  pl.BlockSpec((None, tq, LANES), lambda bi, h, i: (bi, i, qb + h)),
                  pl.BlockSpec((None, seq_len, LANES), lambda bi, h, i: (bi, 0, kb + h)),
                  pl.BlockSpec((None, seq_len, LANES), lambda bi, h, i: (bi, 0, vb + h))],
        out_specs=pl.BlockSpec((None, tq, LANES), lambda bi, h, i: (bi, i, h)),
        out_shape=jax.ShapeDtypeStruct((b, seq_len, DIFF_HEADS * 2 * DIFF_DIM), BF16),
        compiler_params=_params(("parallel", "parallel", "arbitrary")),
        name="differential_attention",
    )(lam_params, subln_g.reshape(1, -1), proj, proj, proj)


def _gate_merge_kernel(h_ref, wg0, wg1, wg2, o0, o1, o2, wb0, wb1, wb2, out_ref):
    h = h_ref[...]
    acc = None
    for wg, o, wb in ((wg0, o0, wb0), (wg1, o1, wb1), (wg2, o2, wb2)):
        gate = 1.0 / (1.0 + jnp.exp(-jnp.dot(h, wg[...], preferred_element_type=F32)))
        term = gate * jnp.dot(o[...], wb[...], preferred_element_type=F32)
        acc = term if acc is None else acc + term
    out_ref[...] = acc.astype(out_ref.dtype)


def gate_merge(h, w_gate, branch_outs, branch_ws, tm=1024, tn=256):
    t, d = h.shape
    tm, tn = min(tm, t), min(tn, d)
    nj = d // tn
    in_specs = [pl.BlockSpec((tm, d), lambda i, j: (i, 0))]
    in_specs += [pl.BlockSpec((d, tn), functools.partial(lambda i, j, b: (0, b * nj + j), b=b))
                 for b in range(N_BRANCH)]
    in_specs += [pl.BlockSpec((tm, o.shape[1]), lambda i, j: (i, 0)) for o in branch_outs]
    in_specs += [pl.BlockSpec((w.shape[0], tn), lambda i, j: (0, j)) for w in branch_ws]
    return pl.pallas_call(
        _gate_merge_kernel,
        grid=(t // tm, nj),
        in_specs=in_specs,
        out_specs=pl.BlockSpec((tm, tn), lambda i, j: (i, j)),
        out_shape=jax.ShapeDtypeStruct((t, d), BF16),
        compiler_params=_params(("parallel", "arbitrary")),
        name="gate_merge",
    )(h, w_gate, w_gate, w_gate, *branch_outs, *branch_ws)


STAT_TAU, STAT_MAX1, STAT_MAX2, STAT_INVZ = 0, 1, 2, 3
STAT_ROWS = 8


def _top_values(x, scr, count):
    for kk in range(count):
        m = jnp.max(x, axis=0, keepdims=True)
        scr[kk:kk + 1, :] = m
        x = jnp.where(x == m, -jnp.inf, x)


def _peer_route_kernel(q_ref, keys_ref, s1_ref, s2_ref, stat_ref, a_scr, b_scr, c_scr, t_scr):
    dims = (((1,), (1,)), ((), ()))
    s1 = lax.dot_general(keys_ref[0], q_ref[:, :PEER_HALF_QDIM], dims, preferred_element_type=F32)
    s2 = lax.dot_general(keys_ref[1], q_ref[:, PEER_HALF_QDIM:], dims, preferred_element_type=F32)
    s1_ref[...] = s1
    s2_ref[...] = s2
    _top_values(s1, a_scr, PEER_TOPK)
    _top_values(s2, b_scr, PEER_TOPK)
    b_top = b_scr[...]
    for ii in range(PEER_TOPK):
        c_scr[ii * PEER_TOPK:(ii + 1) * PEER_TOPK, :] = a_scr[ii:ii + 1, :] + b_top
    cand = c_scr[...]
    _top_values(cand, t_scr, PEER_TOPK)
    tau = t_scr[PEER_TOPK - 1:PEER_TOPK, :]
    best = t_scr[0:1, :]
    z = jnp.sum(jnp.where(cand >= tau, jnp.exp(cand - best), 0.0), axis=0, keepdims=True)
    stat_ref[...] = jnp.zeros_like(stat_ref)
    stat_ref[STAT_TAU:STAT_TAU + 1, :] = tau
    stat_ref[STAT_MAX1:STAT_MAX1 + 1, :] = a_scr[0:1, :]
    stat_ref[STAT_MAX2:STAT_MAX2 + 1, :] = b_scr[0:1, :]
    stat_ref[STAT_INVZ:STAT_INVZ + 1, :] = 1.0 / z


def peer_route(q, sub_keys, tb=256):
    t = q.shape[0]
    tb = min(tb, t)
    score_sds = jax.ShapeDtypeStruct((PEER_HEADS, PEER_NKEYS, t), F32)
    score_spec = pl.BlockSpec((None, PEER_NKEYS, tb), lambda i, h: (h, 0, i))
    return pl.pallas_call(
        _peer_route_kernel,
        grid=(t // tb, PEER_HEADS),
        in_specs=[pl.BlockSpec((tb, 2 * PEER_HALF_QDIM), lambda i, h: (i, h)),
                  pl.BlockSpec((None, 2, PEER_NKEYS, PEER_HALF_QDIM), lambda i, h: (h, 0, 0, 0))],
        out_specs=[score_spec, score_spec, pl.BlockSpec((None, STAT_ROWS, tb), lambda i, h: (h, 0, i))],
        out_shape=[score_sds, score_sds, jax.ShapeDtypeStruct((PEER_HEADS, STAT_ROWS, t), F32)],
        scratch_shapes=[pltpu.VMEM((PEER_TOPK, tb), F32), pltpu.VMEM((PEER_TOPK, tb), F32),
                        pltpu.VMEM((PEER_TOPK * PEER_TOPK, tb), F32), pltpu.VMEM((PEER_TOPK, tb), F32)],
        compiler_params=_params(("parallel", "arbitrary")),
        name="peer_route",
    )(q, sub_keys)


def _gelu(a):
    return 0.5 * a * (1.0 + lax.erf(a * (2.0 ** -0.5)))


def _peer_dense_kernel(ht_ref, u_ref, vt_ref, s1_ref, s2_ref, stat_ref, out_ref, e2_scr, coef_scr, *, te):
    e = pl.program_id(1)

    @pl.when(e == 0)
    def _():
        out_ref[...] = jnp.zeros_like(out_ref)
        for h in range(PEER_HEADS):
            e2_scr[h] = jnp.exp(s2_ref[h] - stat_ref[h, STAT_MAX2:STAT_MAX2 + 1, :])

    act = _gelu(jnp.dot(u_ref[...], ht_ref[...], preferred_element_type=F32))
    n_sub = te // PEER_NKEYS
    for sub in range(n_sub):
        i_idx = e * n_sub + sub
        gate = None
        for h in range(PEER_HEADS):
            s1_row = s1_ref[h, pl.ds(i_idx, 1), :]
            tau = stat_ref[h, STAT_TAU:STAT_TAU + 1, :]
            e1_row = jnp.exp(s1_row - stat_ref[h, STAT_MAX1:STAT_MAX1 + 1, :]) * stat_ref[h, STAT_INVZ:STAT_INVZ + 1, :]
            term = jnp.where(s2_ref[h] + s1_row >= tau, e2_scr[h] * e1_row, 0.0)
            gate = term if gate is None else gate + term
        rows = slice(sub * PEER_NKEYS, (sub + 1) * PEER_NKEYS)
        coef_scr[rows, :] = (gate * act[rows, :]).astype(BF16)
    out_ref[...] += jnp.dot(vt_ref[...], coef_scr[...], preferred_element_type=F32)


def peer_dense(ht, u_tab, vt_tab, s1, s2, stats, tb=512, te=256):
    d, t = ht.shape
    n_exp = u_tab.shape[0]
    tb = min(tb, t)
    assert t % tb == 0 and n_exp % te == 0 and te % PEER_NKEYS == 0
    tok_spec = pl.BlockSpec((PEER_HEADS, PEER_NKEYS, tb), lambda i, e: (0, 0, i))
    kern = functools.partial(_peer_dense_kernel, te=te)
    return pl.pallas_call(
        kern,
        grid=(t // tb, n_exp // te),
        in_specs=[pl.BlockSpec((d, tb), lambda i, e: (0, i)),
                  pl.BlockSpec((te, d), lambda i, e: (e, 0)),
                  pl.BlockSpec((d, te), lambda i, e: (0, e)),
                  tok_spec, tok_spec,
                  pl.BlockSpec((PEER_HEADS, STAT_ROWS, tb), lambda i, e: (0, 0, i))],
        out_specs=pl.BlockSpec((d, tb), lambda i, e: (0, i)),
        out_shape=jax.ShapeDtypeStruct((d, t), F32),
        scratch_shapes=[pltpu.VMEM((PEER_HEADS, PEER_NKEYS, tb), F32), pltpu.VMEM((te, tb), BF16)],
        compiler_params=_params(("parallel", "arbitrary")),
        name="peer_dense",
    )(ht, u_tab, vt_tab, s1, s2, stats)


def kernel(x, attn_norm_g, ffn_norm_g, final_norm_g, w_qkv, w_gate, w_branch_sb, w_branch_dil,
           w_branch_diff, w_out, diff_lambda, diff_subln_g, peer_w_q, peer_sub_keys, peer_u, peer_v):
    b, s, d = x.shape
    t = b * s
    depth = w_qkv.shape[0]
    sb_w = SB_HEADS * HEAD_DIM
    dil_w = DIL_HEADS_PER_GROUP * len(DIL_GROUPS) * HEAD_DIM
    diff_w = DIFF_HEADS * 2 * DIFF_DIM
    dl_q, dl_k, dl_v = 3 * sb_w, 3 * sb_w + dil_w, 3 * sb_w + 2 * dil_w
    df_q = 3 * sb_w + 3 * dil_w
    df_k, df_v = df_q + diff_w, df_q + 2 * diff_w
    qkv_cols = df_v + diff_w
    rope_ops = _rope_operands(s)

    xt = x.reshape(t, d)
    for layer in range(depth):
        h = rmsnorm(xt, attn_norm_g[layer], BF16)
        proj = qkv_projection(h, w_qkv[layer].astype(BF16), rope_ops, s,
                              rope128_cols=(dl_q, dl_v), rope64_cols=(df_q, df_v))
        proj3 = proj.reshape(b, s, qkv_cols)
        o_sb = stick_breaking_attention(proj3, s, 0).reshape(t, sb_w)
        dil = [dilated_group_attention(proj3, s, g, dilation, dl_q, dl_k, dl_v)
               for g, (_, dilation) in enumerate(DIL_GROUPS)]
        o_dl = dilated_merge([o for o, _ in dil], [l for _, l in dil])
        lam_init = 0.8 - 0.6 * math.exp(-0.3 * layer)
        o_df = differential_attention(proj3, diff_lambda[layer], diff_subln_g[layer], s,
                                      df_q, df_k, df_v, lam_init).reshape(t, diff_w)
        merged = gate_merge(h, w_gate[layer].astype(BF16), (o_sb, o_dl, o_df),
                            (w_branch_sb[layer].astype(BF16), w_branch_dil[layer].astype(BF16),
                             w_branch_diff[layer].astype(BF16)))
        xt = matmul(merged, w_out[layer].astype(BF16), F32, residual=xt)

        h2 = rmsnorm(xt, ffn_norm_g[layer], BF16)
        q = matmul(h2, peer_w_q[layer].astype(BF16), BF16)
        s1, s2, stats = peer_route(q, peer_sub_keys[layer].astype(BF16))
        out_t = peer_dense(h2.T, peer_u[layer].astype(BF16), peer_v[layer].T.astype(BF16), s1, s2, stats)
        xt = xt + out_t.T
    return rmsnorm(xt, final_norm_g, F32).reshape(b, s, d)
```

```python
import functools
import math

import jax
import jax.numpy as jnp
from jax import lax
from jax.experimental import pallas as pl
from jax.experimental.pallas import tpu as pltpu

F32 = jnp.float32
BF16 = jnp.bfloat16

HEAD_DIM = 128
ROPE_THETA = 10000.0
RMS_EPS = 1e-6
NEG_BIG = -1e30

SB_HEADS = 8
DIL_GROUPS = ((128, 1), (512, 4), (2048, 16))
DIL_HEADS_PER_GROUP = 4
DIFF_HEADS = 8
DIFF_DIM = 64
N_BRANCH = 3

PEER_HEADS = 8
PEER_NKEYS = 128
PEER_HALF_QDIM = 128
PEER_TOPK = 16

LANES = 128
VMEM_LIMIT = 56 * 1024 * 1024


def _params(sem, vmem=VMEM_LIMIT):
    return pltpu.CompilerParams(dimension_semantics=sem, vmem_limit_bytes=vmem)


def _rmsnorm_kernel(x_ref, g_ref, o_ref):
    x = x_ref[...]
    ms = jnp.mean(x * x, axis=-1, keepdims=True)
    o_ref[...] = (x * lax.rsqrt(ms + RMS_EPS) * g_ref[...]).astype(o_ref.dtype)


def rmsnorm(x, g, out_dtype, rows=256):
    t, d = x.shape
    rows = min(rows, t)
    return pl.pallas_call(
        _rmsnorm_kernel,
        grid=(t // rows,),
        in_specs=[pl.BlockSpec((rows, d), lambda i: (i, 0)), pl.BlockSpec((1, d), lambda i: (0, 0))],
        out_specs=pl.BlockSpec((rows, d), lambda i: (i, 0)),
        out_shape=jax.ShapeDtypeStruct((t, d), out_dtype),
        compiler_params=_params(("parallel",)),
        name="rmsnorm",
    )(x, g.reshape(1, d))


def _rope_tables(seq_len, dim):
    inv_freq = 1.0 / (ROPE_THETA ** (jnp.arange(0, dim, 2, dtype=F32) / dim))
    ang = jnp.arange(seq_len, dtype=F32)[:, None] * inv_freq[None, :]
    ang = jnp.concatenate([ang, ang], axis=-1)
    return jnp.cos(ang), jnp.sin(ang)


def _rope_operands(seq_len):
    cos_h, sin_h = _rope_tables(seq_len, HEAD_DIM)
    lane = jnp.arange(LANES)
    sin_h_signed = jnp.where(lane < HEAD_DIM // 2, -sin_h, sin_h)
    cos_d, sin_d = _rope_tables(seq_len, DIFF_DIM)
    cos_d2 = jnp.concatenate([cos_d, cos_d], axis=-1)
    sin_d2 = jnp.concatenate([sin_d, sin_d], axis=-1)
    low = (lane % DIFF_DIM) < DIFF_DIM // 2
    sin_d_low = jnp.where(low, -sin_d2, 0.0)
    sin_d_high = jnp.where(low, 0.0, sin_d2)
    return cos_h, sin_h_signed, cos_d2, sin_d_low, sin_d_high


def _qkv_kernel(a_ref, w_ref, cos_h, sin_h, cos_d, sin_dl, sin_dh, o_ref, *, rope128_blocks, rope64_blocks):
    j = pl.program_id(1)
    acc = jnp.dot(a_ref[...], w_ref[...], preferred_element_type=F32)
    n_chunks = acc.shape[1] // LANES
    in128 = (j >= rope128_blocks[0]) & (j < rope128_blocks[1])
    in64 = (j >= rope64_blocks[0]) & (j < rope64_blocks[1])

    @pl.when(in128)
    def _():
        for c in range(n_chunks):
            x = acc[:, c * LANES:(c + 1) * LANES]
            y = x * cos_h[...] + pltpu.roll(x, HEAD_DIM // 2, 1) * sin_h[...]
            o_ref[:, c * LANES:(c + 1) * LANES] = y.astype(o_ref.dtype)

    @pl.when(in64)
    def _():
        for c in range(n_chunks):
            x = acc[:, c * LANES:(c + 1) * LANES]
            y = (x * cos_d[...] + pltpu.roll(x, LANES - DIFF_DIM // 2, 1) * sin_dl[...]
                 + pltpu.roll(x, DIFF_DIM // 2, 1) * sin_dh[...])
            o_ref[:, c * LANES:(c + 1) * LANES] = y.astype(o_ref.dtype)

    @pl.when(jnp.logical_not(in128 | in64))
    def _():
        o_ref[...] = acc.astype(o_ref.dtype)


def qkv_projection(h, w, rope_ops, seq_len, rope128_cols, rope64_cols, tm=1024, tn=512):
    m, k = h.shape
    n = w.shape[1]
    tm = min(tm, seq_len)
    assert seq_len % tm == 0 and m % tm == 0 and n % tn == 0
    for lo, hi in (rope128_cols, rope64_cols):
        assert lo % tn == 0 and hi % tn == 0
    seq_blocks = seq_len // tm
    tab_spec = pl.BlockSpec((tm, LANES), lambda i, j: (i % seq_blocks, 0))
    kern = functools.partial(
        _qkv_kernel,
        rope128_blocks=(rope128_cols[0] // tn, rope128_cols[1] // tn),
        rope64_blocks=(rope64_cols[0] // tn, rope64_cols[1] // tn))
    return pl.pallas_call(
        kern,
        grid=(m // tm, n // tn),
        in_specs=[pl.BlockSpec((tm, k), lambda i, j: (i, 0)),
                  pl.BlockSpec((k, tn), lambda i, j: (0, j)),
                  tab_spec, tab_spec, tab_spec, tab_spec, tab_spec],
        out_specs=pl.BlockSpec((tm, tn), lambda i, j: (i, j)),
        out_shape=jax.ShapeDtypeStruct((m, n), BF16),
        compiler_params=_params(("parallel", "arbitrary")),
        name="qkv_projection",
    )(h, w, *rope_ops)


def _matmul_kernel(a_ref, w_ref, o_ref):
    o_ref[...] = jnp.dot(a_ref[...], w_ref[...], preferred_element_type=F32).astype(o_ref.dtype)


def _matmul_residual_kernel(a_ref, w_ref, r_ref, o_ref):
    o_ref[...] = r_ref[...] + jnp.dot(a_ref[...], w_ref[...], preferred_element_type=F32)


def matmul(a, w, out_dtype, residual=None, tm=1024, tn=512):
    m, k = a.shape
    n = w.shape[1]
    tm, tn = min(tm, m), min(tn, n)
    assert m % tm == 0 and n % tn == 0
    in_specs = [pl.BlockSpec((tm, k), lambda i, j: (i, 0)), pl.BlockSpec((k, tn), lambda i, j: (0, j))]
    args = [a, w]
    kern = _matmul_kernel
    if residual is not None:
        in_specs.append(pl.BlockSpec((tm, tn), lambda i, j: (i, j)))
        args.append(residual)
        kern = _matmul_residual_kernel
    return pl.pallas_call(
        kern,
        grid=(m // tm, n // tn),
        in_specs=in_specs,
        out_specs=pl.BlockSpec((tm, tn), lambda i, j: (i, j)),
        out_shape=jax.ShapeDtypeStruct((m, n), out_dtype),
        compiler_params=_params(("parallel", "arbitrary")),
        name="matmul_residual" if residual is not None else "matmul",
    )(*args)


def _softplus(z):
    return jnp.maximum(z, 0.0) + jnp.log(1.0 + jnp.exp(-jnp.abs(z)))


def _sb_kernel(suffix_ref, q_ref, k_ref, v_ref, o_ref, *, tq, heads, scale):
    i = pl.program_id(2)
    tk = 2 * tq
    suffix = suffix_ref[...]
    row = lax.broadcasted_iota(jnp.int32, (tq, tk), 0)
    col = lax.broadcasted_iota(jnp.int32, (tq, tk), 1)
    kd = lax.shift_right_logical(i, 1)
    before = col < row + (i - 2 * kd) * tq
    dims = (((1,), (1,)), ((), ()))
    qs = [q_ref[:, h * HEAD_DIM:(h + 1) * HEAD_DIM] for h in range(heads)]

    def block(kb, state, diagonal):
        start = pl.multiple_of(kb * tk, tk)
        out = []
        for h in range(heads):
            carry, acc = state[h]
            sl = slice(h * HEAD_DIM, (h + 1) * HEAD_DIM)
            k = k_ref[pl.ds(start, tk), sl]
            v = v_ref[pl.ds(start, tk), sl]
            z = lax.dot_general(qs[h], k, dims, preferred_element_type=F32) * scale
            sp = _softplus(z)
            spm = jnp.where(before, sp, 0.0) if diagonal else sp
            spb = spm.astype(BF16)
            sum_lo = jnp.sum(spm[:, :tq], axis=1, keepdims=True)
            sum_hi = jnp.sum(spm[:, tq:], axis=1, keepdims=True)
            later_lo = jnp.dot(spb[:, :tq], suffix, preferred_element_type=F32) + (carry + sum_hi)
            later_hi = jnp.dot(spb[:, tq:], suffix, preferred_element_type=F32) + carry
            w = jnp.exp(z - sp - jnp.concatenate([later_lo, later_hi], axis=1))
            if diagonal:
                w = jnp.where(before, w, 0.0)
            acc = acc + jnp.dot(w.astype(BF16), v, preferred_element_type=F32)
            out.append((carry + (sum_lo + sum_hi), acc))
        return tuple(out)

    state = tuple((jnp.zeros((tq, 1), F32), jnp.zeros((tq, HEAD_DIM), F32)) for _ in range(heads))
    state = block(kd, state, True)
    state = lax.fori_loop(0, kd, lambda n, st: block(kd - 1 - n, st, False), state)
    for h in range(heads):
        o_ref[:, h * HEAD_DIM:(h + 1) * HEAD_DIM] = state[h][1].astype(o_ref.dtype)


def stick_breaking_attention(proj, seq_len, col0, tq=256, heads=2):
    b = proj.shape[0]
    tq = min(tq, seq_len // 2)
    hw = heads * HEAD_DIM
    assert seq_len % (2 * tq) == 0 and SB_HEADS % heads == 0 and col0 % hw == 0
    c0 = col0 // hw
    per = SB_HEADS // heads
    idx = jnp.arange(tq)
    suffix = (idx[:, None] > idx[None, :]).astype(BF16)
    kern = functools.partial(_sb_kernel, tq=tq, heads=heads, scale=HEAD_DIM ** -0.5)
    return pl.pallas_call(
        kern,
        grid=(b, per, seq_len // tq),
        in_specs=[pl.BlockSpec((tq, tq), lambda bi, h, i: (0, 0)),
                  pl.BlockSpec((None, tq, hw), lambda bi, h, i: (bi, i, c0 + h)),
                  pl.BlockSpec((None, seq_len, hw), lambda bi, h, i: (bi, 0, c0 + per + h)),
                  pl.BlockSpec((None, seq_len, hw), lambda bi, h, i: (bi, 0, c0 + 2 * per + h))],
        out_specs=pl.BlockSpec((None, tq, hw), lambda bi, h, i: (bi, i, h)),
        out_shape=jax.ShapeDtypeStruct((b, seq_len, SB_HEADS * HEAD_DIM), BF16),
        compiler_params=_params(("parallel", "parallel", "arbitrary")),
        name="stick_breaking_attention",
    )(suffix, proj, proj, proj)


def _dil_kernel(q_ref, kp_ref, kc_ref, vp_ref, vc_ref, o_ref, lse_ref, *, tq, scale):
    i = pl.program_id(2)
    row = lax.broadcasted_iota(jnp.int32, (tq, tq), 0)
    col = lax.broadcasted_iota(jnp.int32, (tq, tq), 1)
    cur_ok = col <= row
    prev_ok = col >= row + jnp.where(i > 0, 0, tq)
    dims = (((1,), (1,)), ((), ()))
    for h in range(DIL_HEADS_PER_GROUP):
        sl = slice(h * HEAD_DIM, (h + 1) * HEAD_DIM)
        q = q_ref[:, sl]
        s_cur = lax.dot_general(q, kc_ref[:, sl], dims, preferred_element_type=F32) * scale
        s_prev = lax.dot_general(q, kp_ref[:, sl], dims, preferred_element_type=F32) * scale
        s_cur = jnp.where(cur_ok, s_cur, NEG_BIG)
        s_prev = jnp.where(prev_ok, s_prev, NEG_BIG)
        m = jnp.maximum(jnp.max(s_cur, axis=1, keepdims=True), jnp.max(s_prev, axis=1, keepdims=True))
        p_cur = jnp.exp(s_cur - m)
        p_prev = jnp.exp(s_prev - m)
        l = jnp.sum(p_cur, axis=1, keepdims=True) + jnp.sum(p_prev, axis=1, keepdims=True)
        o = (jnp.dot(p_cur.astype(BF16), vc_ref[:, sl], preferred_element_type=F32)
             + jnp.dot(p_prev.astype(BF16), vp_ref[:, sl], preferred_element_type=F32))
        o_ref[:, sl] = o / l
        lse_ref[:, sl] = jnp.broadcast_to(m + jnp.log(l), (tq, HEAD_DIM))


def dilated_group_attention(proj, seq_len, group, slot, dilation, q_col, k_col, v_col):
    b, _, cols = proj.shape
    gw = DIL_HEADS_PER_GROUP * HEAD_DIM
    tq = DIL_GROUPS[group][0] // dilation
    sub_len = seq_len // dilation
    assert sub_len % tq == 0 and cols % gw == 0
    view = proj.reshape(b, sub_len, dilation * cols)
    per_row = cols // gw
    qb, kb, vb = (q_col // gw + slot, k_col // gw + slot, v_col // gw + slot)
    prev = lambda i: jnp.maximum(i - 1, 0)
    out_sds = jax.ShapeDtypeStruct((b, sub_len, dilation * gw), F32)
    kern = functools.partial(_dil_kernel, tq=tq, scale=HEAD_DIM ** -0.5)
    o, lse = pl.pallas_call(
        kern,
        grid=(b, dilation, sub_len // tq),
        in_specs=[pl.BlockSpec((None, tq, gw), lambda bi, c, i: (bi, i, c * per_row + qb)),
                  pl.BlockSpec((None, tq, gw), lambda bi, c, i: (bi, prev(i), c * per_row + kb)),
                  pl.BlockSpec((None, tq, gw), lambda bi, c, i: (bi, i, c * per_row + kb)),
                  pl.BlockSpec((None, tq, gw), lambda bi, c, i: (bi, prev(i), c * per_row + vb)),
                  pl.BlockSpec((None, tq, gw), lambda bi, c, i: (bi, i, c * per_row + vb))],
        out_specs=[pl.BlockSpec((None, tq, gw), lambda bi, c, i: (bi, i, c)),
                   pl.BlockSpec((None, tq, gw), lambda bi, c, i: (bi, i, c))],
        out_shape=[out_sds, out_sds],
        compiler_params=_params(("parallel", "parallel", "arbitrary")),
        name=f"dilated_attention_g{group}",
    )(view, view, view, view, view)
    return o.reshape(b * seq_len, gw), lse.reshape(b * seq_len, gw)


def _dil_merge_kernel(o0, o1, o2, l0, l1, l2, out_ref):
    m = jnp.maximum(jnp.maximum(l0[...], l1[...]), l2[...])
    e0, e1, e2 = jnp.exp(l0[...] - m), jnp.exp(l1[...] - m), jnp.exp(l2[...] - m)
    out_ref[...] = ((e0 * o0[...] + e1 * o1[...] + e2 * o2[...]) / (e0 + e1 + e2)).astype(out_ref.dtype)


def dilated_merge(outs, lses, rows=512):
    t, w = outs[0].shape
    rows = min(rows, t)
    spec = pl.BlockSpec((rows, w), lambda i: (i, 0))
    return pl.pallas_call(
        _dil_merge_kernel,
        grid=(t // rows,),
        in_specs=[spec] * 6,
        out_specs=spec,
        out_shape=jax.ShapeDtypeStruct((t, w), BF16),
        compiler_params=_params(("parallel",)),
        name="dilated_merge",
    )(*outs, *lses)


def _diff_kernel(lam_ref, g_ref, q_ref, k_ref, v_ref, o_ref, *, tq, heads, scale, lam_init):
    i = pl.program_id(2)
    tk = 2 * tq
    lp = lam_ref[...]
    lam = (jnp.exp(jnp.sum(lp[0:1] * lp[1:2], axis=1, keepdims=True))
           - jnp.exp(jnp.sum(lp[2:3] * lp[3:4], axis=1, keepdims=True)) + lam_init)
    lane = lax.broadcasted_iota(jnp.int32, (tq, LANES), 1)
    qqs = []
    for h in range(heads):
        q = q_ref[:, h * LANES:(h + 1) * LANES].astype(F32)
        qqs.append(jnp.concatenate([jnp.where(lane < DIFF_DIM, q, 0.0), jnp.where(lane >= DIFF_DIM, q, 0.0)],
                                   axis=0).astype(BF16))
    row = lax.broadcasted_iota(jnp.int32, (2 * tq, tk), 0)
    row = jnp.where(row >= tq, row - tq, row)
    col = lax.broadcasted_iota(jnp.int32, (2 * tq, tk), 1)
    kd = lax.shift_right_logical(i, 1)
    causal = col <= row + (i - 2 * kd) * tq
    dims = (((1,), (1,)), ((), ()))

    def block(kb, state, diagonal):
        start = pl.multiple_of(kb * tk, tk)
        out = []
        for h in range(heads):
            m, l, acc = state[h]
            sl = slice(h * LANES, (h + 1) * LANES)
            k = k_ref[pl.ds(start, tk), sl]
            v = v_ref[pl.ds(start, tk), sl]
            s = lax.dot_general(qqs[h], k, dims, preferred_element_type=F32) * scale
            if diagonal:
                s = jnp.where(causal, s, NEG_BIG)
            m_new = jnp.maximum(m, jnp.max(s, axis=1, keepdims=True))
            alpha = jnp.exp(m - m_new)
            p = jnp.exp(s - m_new)
            l = alpha * l + jnp.sum(p, axis=1, keepdims=True)
            acc = alpha * acc + jnp.dot(p.astype(BF16), v, preferred_element_type=F32)
            out.append((m_new, l, acc))
        return tuple(out)

    state = tuple((jnp.full((2 * tq, 1), NEG_BIG, F32), jnp.zeros((2 * tq, 1), F32),
                   jnp.zeros((2 * tq, LANES), F32)) for _ in range(heads))
    state = block(kd, state, True)
    state = lax.fori_loop(0, kd, lambda n, st: block(n, st, False), state)
    for h in range(heads):
        _, l, acc = state[h]
        o_all = acc / l
        o = o_all[:tq] - lam * o_all[tq:]
        ms = jnp.mean(o * o, axis=-1, keepdims=True)
        y = o * lax.rsqrt(ms + RMS_EPS) * g_ref[...]
        o_ref[:, h * LANES:(h + 1) * LANES] = (y * (1.0 - lam_init)).astype(o_ref.dtype)


def differential_attention(proj, lam_params, subln_g, seq_len, q_col, k_col, v_col, lam_init, tq=256, heads=2):
    b = proj.shape[0]
    tq = min(tq, seq_len // 2)
    hw = heads * LANES
    assert seq_len % (2 * tq) == 0 and DIFF_HEADS % heads == 0
    assert q_col % hw == 0 and k_col % hw == 0 and v_col % hw == 0
    qb, kb, vb = q_col // hw, k_col // hw, v_col // hw
    kern = functools.partial(_diff_kernel, tq=tq, heads=heads, scale=DIFF_DIM ** -0.5, lam_init=lam_init)
    return pl.pallas_call(
        kern,
        grid=(b, DIFF_HEADS // heads, seq_len // tq),
        in_specs=[pl.BlockSpec((4, DIFF_DIM), lambda bi, h, i: (0, 0)),
                  pl.BlockSpec((1, 2 * DIFF_DIM), lambda bi, h, i: (0, 0)),
                  pl.BlockSpec((None, tq, hw), lambda bi, h, i: (bi, i, qb + h)),
                  pl.BlockSpec((None, seq_len, hw), lambda bi, h, i: (bi, 0, kb + h)),
                  pl.BlockSpec((None, seq_len, hw), lambda bi, h, i: (bi, 0, vb + h))],
        out_specs=pl.BlockSpec((None, tq, hw), lambda bi, h, i: (bi, i, h)),
        out_shape=jax.ShapeDtypeStruct((b, seq_len, DIFF_HEADS * 2 * DIFF_DIM), BF16),
        compiler_params=_params(("parallel", "parallel", "arbitrary")),
        name="differential_attention",
    )(lam_params, subln_g.reshape(1, -1), proj, proj, proj)


def _gate_merge_kernel(h_ref, wg0, wg1, wg2, o0, o1, o2, wb0, wb1, wb2, out_ref):
    h = h_ref[...]
    acc = None
    for wg, o, wb in ((wg0, o0, wb0), (wg1, o1, wb1), (wg2, o2, wb2)):
        gate = 1.0 / (1.0 + jnp.exp(-jnp.dot(h, wg[...], preferred_element_type=F32)))
        term = gate * jnp.dot(o[...], wb[...], preferred_element_type=F32)
        acc = term if acc is None else acc + term
    out_ref[...] = acc.astype(out_ref.dtype)


def gate_merge(h, w_gate, branch_outs, branch_ws, tm=1024, tn=256):
    t, d = h.shape
    tm, tn = min(tm, t), min(tn, d)
    nj = d // tn
    in_specs = [pl.BlockSpec((tm, d), lambda i, j: (i, 0))]
    in_specs += [pl.BlockSpec((d, tn), functools.partial(lambda i, j, b: (0, b * nj + j), b=b))
                 for b in range(N_BRANCH)]
    in_specs += [pl.BlockSpec((tm, o.shape[1]), lambda i, j: (i, 0)) for o in branch_outs]
    in_specs += [pl.BlockSpec((w.shape[0], tn), lambda i, j: (0, j)) for w in branch_ws]
    return pl.pallas_call(
        _gate_merge_kernel,
        grid=(t // tm, nj),
        in_specs=in_specs,
        out_specs=pl.BlockSpec((tm, tn), lambda i, j: (i, j)),
        out_shape=jax.ShapeDtypeStruct((t, d), BF16),
        compiler_params=_params(("parallel", "arbitrary")),
        name="gate_merge",
    )(h, w_gate, w_gate, w_gate, *branch_outs, *branch_ws)


STAT_TAU, STAT_MAX1, STAT_MAX2, STAT_INVZ = 0, 1, 2, 3
STAT_ROWS = 8


def _top_values(x, scr, count):
    for kk in range(count):
        m = jnp.max(x, axis=0, keepdims=True)
        scr[kk:kk + 1, :] = m
        x = jnp.where(x == m, -jnp.inf, x)


def _peer_route_kernel(q_ref, keys_ref, s1_ref, s2_ref, stat_ref, a_scr, b_scr, c_scr, t_scr):
    dims = (((1,), (1,)), ((), ()))
    s1 = lax.dot_general(keys_ref[0], q_ref[:, :PEER_HALF_QDIM], dims, preferred_element_type=F32)
    s2 = lax.dot_general(keys_ref[1], q_ref[:, PEER_HALF_QDIM:], dims, preferred_element_type=F32)
    s1_ref[...] = s1
    s2_ref[...] = s2
    _top_values(s1, a_scr, PEER_TOPK)
    _top_values(s2, b_scr, PEER_TOPK)
    b_top = b_scr[...]
    for ii in range(PEER_TOPK):
        c_scr[ii * PEER_TOPK:(ii + 1) * PEER_TOPK, :] = a_scr[ii:ii + 1, :] + b_top
    cand = c_scr[...]
    _top_values(cand, t_scr, PEER_TOPK)
    tau = t_scr[PEER_TOPK - 1:PEER_TOPK, :]
    best = t_scr[0:1, :]
    z = jnp.sum(jnp.where(cand >= tau, jnp.exp(cand - best), 0.0), axis=0, keepdims=True)
    stat_ref[...] = jnp.zeros_like(stat_ref)
    stat_ref[STAT_TAU:STAT_TAU + 1, :] = tau
    stat_ref[STAT_MAX1:STAT_MAX1 + 1, :] = a_scr[0:1, :]
    stat_ref[STAT_MAX2:STAT_MAX2 + 1, :] = b_scr[0:1, :]
    stat_ref[STAT_INVZ:STAT_INVZ + 1, :] = 1.0 / z


def peer_route(q, sub_keys, tb=256):
    t = q.shape[0]
    tb = min(tb, t)
    score_sds = jax.ShapeDtypeStruct((PEER_HEADS, PEER_NKEYS, t), F32)
    score_spec = pl.BlockSpec((None, PEER_NKEYS, tb), lambda i, h: (h, 0, i))
    return pl.pallas_call(
        _peer_route_kernel,
        grid=(t // tb, PEER_HEADS),
        in_specs=[pl.BlockSpec((tb, 2 * PEER_HALF_QDIM), lambda i, h: (i, h)),
                  pl.BlockSpec((None, 2, PEER_NKEYS, PEER_HALF_QDIM), lambda i, h: (h, 0, 0, 0))],
        out_specs=[score_spec, score_spec, pl.BlockSpec((None, STAT_ROWS, tb), lambda i, h: (h, 0, i))],
        out_shape=[score_sds, score_sds, jax.ShapeDtypeStruct((PEER_HEADS, STAT_ROWS, t), F32)],
        scratch_shapes=[pltpu.VMEM((PEER_TOPK, tb), F32), pltpu.VMEM((PEER_TOPK, tb), F32),
                        pltpu.VMEM((PEER_TOPK * PEER_TOPK, tb), F32), pltpu.VMEM((PEER_TOPK, tb), F32)],
        compiler_params=_params(("parallel", "arbitrary")),
        name="peer_route",
    )(q, sub_keys)


def _gelu(a):
    return 0.5 * a * (1.0 + lax.erf(a * (2.0 ** -0.5)))


PEER_STAGES = 3
PEER_EXPERT_BLOCK = 256


def _peer_dense_kernel(ht_ref, u_ref, vt_ref, s1_ref, s2_ref, stat_ref, out_ref, e2_scr,
                       act_a, act_b, coef_a, coef_b, *, te, n_blocks):
    e = pl.program_id(1)

    @pl.when(e == 0)
    def _():
        out_ref[...] = jnp.zeros_like(out_ref)
        for scr in (act_a, act_b, coef_a, coef_b):
            scr[...] = jnp.zeros_like(scr)
        for h in range(PEER_HEADS):
            e2_scr[h] = jnp.exp(s2_ref[h] - stat_ref[h, STAT_MAX2:STAT_MAX2 + 1, :])

    n_sub = te // PEER_NKEYS
    d_model, tb = ht_ref.shape
    gate_block = jnp.clip(e - 1, 0, n_blocks - 1)

    def stages(act_cur, act_prev, coef_cur, coef_prev):
        n_lane = tb // LANES
        n_slices = n_sub * n_lane
        kc = d_model // n_slices
        act_cur[...] = jnp.dot(u_ref[...], ht_ref[...], preferred_element_type=F32)
        for sub in range(n_sub):
            i_idx = gate_block * n_sub + sub
            rows = slice(sub * PEER_NKEYS, (sub + 1) * PEER_NKEYS)
            s1_rows = [s1_ref[h, pl.ds(i_idx, 1), :] for h in range(PEER_HEADS)]
            e1_rows = [jnp.exp(s1_rows[h] - stat_ref[h, STAT_MAX1:STAT_MAX1 + 1, :])
                       * stat_ref[h, STAT_INVZ:STAT_INVZ + 1, :] for h in range(PEER_HEADS)]
            for c in range(n_lane):
                r = sub * n_lane + c
                chunk = slice(r * kc, (r + 1) * kc)
                lanes = slice(c * LANES, (c + 1) * LANES)
                gate = None
                for h in range(PEER_HEADS):
                    total = s2_ref[h, :, lanes] + s1_rows[h][:, lanes]
                    term = jnp.where(total >= stat_ref[h, STAT_TAU:STAT_TAU + 1, lanes],
                                     e2_scr[h, :, lanes] * e1_rows[h][:, lanes], 0.0)
                    gate = term if gate is None else gate + term
                coef_prev[rows, lanes] = (gate * _gelu(act_prev[rows, lanes])).astype(BF16)
                out_ref[chunk, :] += jnp.dot(vt_ref[chunk, :], coef_cur[...], preferred_element_type=F32)

    parity = lax.rem(e, 2)

    @pl.when(parity == 0)
    def _():
        stages(act_a, act_b, coef_a, coef_b)

    @pl.when(parity == 1)
    def _():
        stages(act_b, act_a, coef_b, coef_a)


def peer_dense(ht, u_tab, vt_blocks, s1, s2, stats, tb=512):
    d, t = ht.shape
    n_blocks, _, te = vt_blocks.shape
    tb = min(tb, t)
    assert t % tb == 0 and u_tab.shape[0] == n_blocks * te and te % PEER_NKEYS == 0
    tok_spec = pl.BlockSpec((PEER_HEADS, PEER_NKEYS, tb), lambda i, e: (0, 0, i))
    last = n_blocks - 1
    kern = functools.partial(_peer_dense_kernel, te=te, n_blocks=n_blocks)
    return pl.pallas_call(
        kern,
        grid=(t // tb, n_blocks + PEER_STAGES - 1),
        in_specs=[pl.BlockSpec((d, tb), lambda i, e: (0, i)),
                  pl.BlockSpec((te, d), lambda i, e: (jnp.minimum(e, last), 0)),
                  pl.BlockSpec((None, d, te), lambda i, e: (jnp.clip(e - 2, 0, last), 0, 0)),
                  tok_spec, tok_spec,
                  pl.BlockSpec((PEER_HEADS, STAT_ROWS, tb), lambda i, e: (0, 0, i))],
        out_specs=pl.BlockSpec((d, tb), lambda i, e: (0, i)),
        out_shape=jax.ShapeDtypeStruct((d, t), F32),
        scratch_shapes=[pltpu.VMEM((PEER_HEADS, PEER_NKEYS, tb), F32),
                        pltpu.VMEM((te, tb), F32), pltpu.VMEM((te, tb), F32),
                        pltpu.VMEM((te, tb), BF16), pltpu.VMEM((te, tb), BF16)],
        compiler_params=_params(("parallel", "arbitrary")),
        name="peer_dense",
    )(ht, u_tab, vt_blocks, s1, s2, stats)


def kernel(x, attn_norm_g, ffn_norm_g, final_norm_g, w_qkv, w_gate, w_branch_sb, w_branch_dil,
           w_branch_diff, w_out, diff_lambda, diff_subln_g, peer_w_q, peer_sub_keys, peer_u, peer_v):
    b, s, d = x.shape
    t = b * s
    depth = w_qkv.shape[0]
    sb_w = SB_HEADS * HEAD_DIM
    dil_w = DIL_HEADS_PER_GROUP * len(DIL_GROUPS) * HEAD_DIM
    diff_w = DIFF_HEADS * 2 * DIFF_DIM
    dl_q, dl_k, dl_v = 3 * sb_w, 3 * sb_w + dil_w, 3 * sb_w + 2 * dil_w
    df_q = 3 * sb_w + 3 * dil_w
    df_k, df_v = df_q + diff_w, df_q + 2 * diff_w
    qkv_cols = df_v + diff_w
    rope_ops = _rope_operands(s)

    xt = x.reshape(t, d)
    for layer in range(depth):
        h = rmsnorm(xt, attn_norm_g[layer], BF16)
        proj = qkv_projection(h, w_qkv[layer].astype(BF16), rope_ops, s,
                              rope128_cols=(dl_q, dl_v), rope64_cols=(df_q, df_v))
        proj3 = proj.reshape(b, s, qkv_cols)
        o_sb = stick_breaking_attention(proj3, s, 0).reshape(t, sb_w)
        dil = []
        gw = DIL_HEADS_PER_GROUP * HEAD_DIM
        for g, (_, dilation) in enumerate(DIL_GROUPS):
            if dilation == 1:
                dil.append(dilated_group_attention(proj3, s, g, g, dilation, dl_q, dl_k, dl_v))
            else:
                cols = jnp.concatenate([proj3[:, :, c + g * gw:c + (g + 1) * gw] for c in (dl_q, dl_k, dl_v)], -1)
                dil.append(dilated_group_attention(cols, s, g, 0, dilation, 0, gw, 2 * gw))
        o_dl = dilated_merge([o for o, _ in dil], [l for _, l in dil])
        lam_init = 0.8 - 0.6 * math.exp(-0.3 * layer)
        o_df = differential_attention(proj3, diff_lambda[layer], diff_subln_g[layer], s,
                                      df_q, df_k, df_v, lam_init).reshape(t, diff_w)
        merged = gate_merge(h, w_gate[layer].astype(BF16), (o_sb, o_dl, o_df),
                            (w_branch_sb[layer].astype(BF16), w_branch_dil[layer].astype(BF16),
                             w_branch_diff[layer].astype(BF16)))
        xt = matmul(merged, w_out[layer].astype(BF16), F32, residual=xt)

        h2 = rmsnorm(xt, ffn_norm_g[layer], BF16)
        q = matmul(h2, peer_w_q[layer].astype(BF16), BF16)
        s1, s2, stats = peer_route(q, peer_sub_keys[layer].astype(BF16))
        vt_blocks = peer_v[layer].reshape(-1, PEER_EXPERT_BLOCK, d).transpose(0, 2, 1).astype(BF16)
        out_t = peer_dense(h2.T, peer_u[layer].astype(BF16), vt_blocks, s1, s2, stats)
        xt = xt + out_t.T
    return rmsnorm(xt, final_norm_g, F32).reshape(b, s, d)
```

```python
import functools
import math

import jax
import jax.numpy as jnp
from jax import lax
from jax.experimental import pallas as pl
from jax.experimental.pallas import tpu as pltpu

F32 = jnp.float32
BF16 = jnp.bfloat16

HEAD_DIM = 128
ROPE_THETA = 10000.0
RMS_EPS = 1e-6
NEG_BIG = -1e30

SB_HEADS = 8
DIL_GROUPS = ((128, 1), (512, 4), (2048, 16))
DIL_HEADS_PER_GROUP = 4
DIFF_HEADS = 8
DIFF_DIM = 64
N_BRANCH = 3

PEER_HEADS = 8
PEER_NKEYS = 128
PEER_HALF_QDIM = 128
PEER_TOPK = 16

LANES = 128
VMEM_LIMIT = 56 * 1024 * 1024


def _params(sem, vmem=VMEM_LIMIT):
    return pltpu.CompilerParams(dimension_semantics=sem, vmem_limit_bytes=vmem)


def _rmsnorm_kernel(x_ref, g_ref, o_ref):
    x = x_ref[...]
    ms = jnp.mean(x * x, axis=-1, keepdims=True)
    o_ref[...] = (x * lax.rsqrt(ms + RMS_EPS) * g_ref[...]).astype(o_ref.dtype)


def rmsnorm(x, g, out_dtype, rows=256):
    t, d = x.shape
    rows = min(rows, t)
    return pl.pallas_call(
        _rmsnorm_kernel,
        grid=(t // rows,),
        in_specs=[pl.BlockSpec((rows, d), lambda i: (i, 0)), pl.BlockSpec((1, d), lambda i: (0, 0))],
        out_specs=pl.BlockSpec((rows, d), lambda i: (i, 0)),
        out_shape=jax.ShapeDtypeStruct((t, d), out_dtype),
        compiler_params=_params(("parallel",)),
        name="rmsnorm",
    )(x, g.reshape(1, d))


def _rope_tables(seq_len, dim):
    inv_freq = 1.0 / (ROPE_THETA ** (jnp.arange(0, dim, 2, dtype=F32) / dim))
    ang = jnp.arange(seq_len, dtype=F32)[:, None] * inv_freq[None, :]
    ang = jnp.concatenate([ang, ang], axis=-1)
    return jnp.cos(ang), jnp.sin(ang)


def _rope_operands(seq_len):
    cos_h, sin_h = _rope_tables(seq_len, HEAD_DIM)
    lane = jnp.arange(LANES)
    sin_h_signed = jnp.where(lane < HEAD_DIM // 2, -sin_h, sin_h)
    cos_d, sin_d = _rope_tables(seq_len, DIFF_DIM)
    cos_d2 = jnp.concatenate([cos_d, cos_d], axis=-1)
    sin_d2 = jnp.concatenate([sin_d, sin_d], axis=-1)
    low = (lane % DIFF_DIM) < DIFF_DIM // 2
    sin_d_low = jnp.where(low, -sin_d2, 0.0)
    sin_d_high = jnp.where(low, 0.0, sin_d2)
    return cos_h, sin_h_signed, cos_d2, sin_d_low, sin_d_high


def _qkv_kernel(a_ref, w_ref, cos_h, sin_h, cos_d, sin_dl, sin_dh, o_ref, *, rope128_blocks, rope64_blocks):
    j = pl.program_id(1)
    acc = jnp.dot(a_ref[...], w_ref[...], preferred_element_type=F32)
    n_chunks = acc.shape[1] // LANES
    in128 = (j >= rope128_blocks[0]) & (j < rope128_blocks[1])
    in64 = (j >= rope64_blocks[0]) & (j < rope64_blocks[1])

    @pl.when(in128)
    def _():
        for c in range(n_chunks):
            x = acc[:, c * LANES:(c + 1) * LANES]
            y = x * cos_h[...] + pltpu.roll(x, HEAD_DIM // 2, 1) * sin_h[...]
            o_ref[:, c * LANES:(c + 1) * LANES] = y.astype(o_ref.dtype)

    @pl.when(in64)
    def _():
        for c in range(n_chunks):
            x = acc[:, c * LANES:(c + 1) * LANES]
            y = (x * cos_d[...] + pltpu.roll(x, LANES - DIFF_DIM // 2, 1) * sin_dl[...]
                 + pltpu.roll(x, DIFF_DIM // 2, 1) * sin_dh[...])
            o_ref[:, c * LANES:(c + 1) * LANES] = y.astype(o_ref.dtype)

    @pl.when(jnp.logical_not(in128 | in64))
    def _():
        o_ref[...] = acc.astype(o_ref.dtype)


def qkv_projection(h, w, rope_ops, seq_len, rope128_cols, rope64_cols, tm=1024, tn=512):
    m, k = h.shape
    n = w.shape[1]
    tm = min(tm, seq_len)
    assert seq_len % tm == 0 and m % tm == 0 and n % tn == 0
    for lo, hi in (rope128_cols, rope64_cols):
        assert lo % tn == 0 and hi % tn == 0
    seq_blocks = seq_len // tm
    tab_spec = pl.BlockSpec((tm, LANES), lambda i, j: (i % seq_blocks, 0))
    kern = functools.partial(
        _qkv_kernel,
        rope128_blocks=(rope128_cols[0] // tn, rope128_cols[1] // tn),
        rope64_blocks=(rope64_cols[0] // tn, rope64_cols[1] // tn))
    return pl.pallas_call(
        kern,
        grid=(m // tm, n // tn),
        in_specs=[pl.BlockSpec((tm, k), lambda i, j: (i, 0)),
                  pl.BlockSpec((k, tn), lambda i, j: (0, j)),
                  tab_spec, tab_spec, tab_spec, tab_spec, tab_spec],
        out_specs=pl.BlockSpec((tm, tn), lambda i, j: (i, j)),
        out_shape=jax.ShapeDtypeStruct((m, n), BF16),
        compiler_params=_params(("parallel", "arbitrary")),
        name="qkv_projection",
    )(h, w, *rope_ops)


def _matmul_kernel(a_ref, w_ref, o_ref):
    o_ref[...] = jnp.dot(a_ref[...], w_ref[...], preferred_element_type=F32).astype(o_ref.dtype)


def _matmul_residual_kernel(a_ref, w_ref, r_ref, o_ref):
    o_ref[...] = r_ref[...] + jnp.dot(a_ref[...], w_ref[...], preferred_element_type=F32)


def matmul(a, w, out_dtype, residual=None, tm=1024, tn=512):
    m, k = a.shape
    n = w.shape[1]
    tm, tn = min(tm, m), min(tn, n)
    assert m % tm == 0 and n % tn == 0
    in_specs = [pl.BlockSpec((tm, k), lambda i, j: (i, 0)), pl.BlockSpec((k, tn), lambda i, j: (0, j))]
    args = [a, w]
    kern = _matmul_kernel
    if residual is not None:
        in_specs.append(pl.BlockSpec((tm, tn), lambda i, j: (i, j)))
        args.append(residual)
        kern = _matmul_residual_kernel
    return pl.pallas_call(
        kern,
        grid=(m // tm, n // tn),
        in_specs=in_specs,
        out_specs=pl.BlockSpec((tm, tn), lambda i, j: (i, j)),
        out_shape=jax.ShapeDtypeStruct((m, n), out_dtype),
        compiler_params=_params(("parallel", "arbitrary")),
        name="matmul_residual" if residual is not None else "matmul",
    )(*args)


LOG2_E = math.log2(math.e)
SIGN_BIT = 0x80000000


def _softplus2(u):
    neg_abs = lax.bitcast_convert_type(lax.bitcast_convert_type(u, jnp.uint32) | jnp.uint32(SIGN_BIT), F32)
    return jnp.maximum(u, 0.0) + jnp.log2(1.0 + jnp.exp2(neg_abs))


def _sb_kernel(suffix_ref, q_ref, k_ref, v_ref, o_ref, *, tq, heads, scale2):
    i = pl.program_id(2)
    tk = 2 * tq
    suffix = suffix_ref[...]
    row = lax.broadcasted_iota(jnp.int32, (tq, tk), 0)
    col = lax.broadcasted_iota(jnp.int32, (tq, tk), 1)
    kd = lax.shift_right_logical(i, 1)
    before = col < row + (i - 2 * kd) * tq
    dims = (((1,), (1,)), ((), ()))
    qs = [q_ref[:, h * HEAD_DIM:(h + 1) * HEAD_DIM] for h in range(heads)]

    def block(kb, state, diagonal):
        start = pl.multiple_of(kb * tk, tk)
        out = []
        for h in range(heads):
            carry, acc = state[h]
            sl = slice(h * HEAD_DIM, (h + 1) * HEAD_DIM)
            k = k_ref[pl.ds(start, tk), sl]
            v = v_ref[pl.ds(start, tk), sl]
            z = lax.dot_general(qs[h], k, dims, preferred_element_type=F32) * scale2
            sp = _softplus2(z)
            spm = jnp.where(before, sp, 0.0) if diagonal else sp
            spb = spm.astype(BF16)
            sum_lo = jnp.sum(spm[:, :tq], axis=1, keepdims=True)
            sum_hi = jnp.sum(spm[:, tq:], axis=1, keepdims=True)
            later_lo = jnp.dot(spb[:, :tq], suffix, preferred_element_type=F32) + (carry + sum_hi)
            later_hi = jnp.dot(spb[:, tq:], suffix, preferred_element_type=F32) + carry
            w = jnp.exp2(z - sp - jnp.concatenate([later_lo, later_hi], axis=1))
            if diagonal:
                w = jnp.where(before, w, 0.0)
            acc = acc + jnp.dot(w.astype(BF16), v, preferred_element_type=F32)
            out.append((carry + (sum_lo + sum_hi), acc))
        return tuple(out)

    state = tuple((jnp.zeros((tq, 1), F32), jnp.zeros((tq, HEAD_DIM), F32)) for _ in range(heads))
    state = block(kd, state, True)
    state = lax.fori_loop(0, kd, lambda n, st: block(kd - 1 - n, st, False), state)
    for h in range(heads):
        o_ref[:, h * HEAD_DIM:(h + 1) * HEAD_DIM] = state[h][1].astype(o_ref.dtype)


def stick_breaking_attention(proj, seq_len, col0, tq=256, heads=4):
    b = proj.shape[0]
    tq = min(tq, seq_len // 2)
    hw = heads * HEAD_DIM
    assert seq_len % (2 * tq) == 0 and SB_HEADS % heads == 0 and col0 % hw == 0
    c0 = col0 // hw
    per = SB_HEADS // heads
    idx = jnp.arange(tq)
    suffix = (idx[:, None] > idx[None, :]).astype(BF16)
    kern = functools.partial(_sb_kernel, tq=tq, heads=heads, scale2=HEAD_DIM ** -0.5 * LOG2_E)
    return pl.pallas_call(
        kern,
        grid=(b, per, seq_len // tq),
        in_specs=[pl.BlockSpec((tq, tq), lambda bi, h, i: (0, 0)),
                  pl.BlockSpec((None, tq, hw), lambda bi, h, i: (bi, i, c0 + h)),
                  pl.BlockSpec((None, seq_len, hw), lambda bi, h, i: (bi, 0, c0 + per + h)),
                  pl.BlockSpec((None, seq_len, hw), lambda bi, h, i: (bi, 0, c0 + 2 * per + h))],
        out_specs=pl.BlockSpec((None, tq, hw), lambda bi, h, i: (bi, i, h)),
        out_shape=jax.ShapeDtypeStruct((b, seq_len, SB_HEADS * HEAD_DIM), BF16),
        compiler_params=_params(("parallel", "parallel", "arbitrary")),
        name="stick_breaking_attention",
    )(suffix, proj, proj, proj)


def _dil_kernel(q_ref, kp_ref, kc_ref, vp_ref, vc_ref, o_ref, lse_ref, *, tq, scale):
    i = pl.program_id(2)
    row = lax.broadcasted_iota(jnp.int32, (tq, tq), 0)
    col = lax.broadcasted_iota(jnp.int32, (tq, tq), 1)
    cur_ok = col <= row
    prev_ok = col >= row + jnp.where(i > 0, 0, tq)
    dims = (((1,), (1,)), ((), ()))
    for h in range(DIL_HEADS_PER_GROUP):
        sl = slice(h * HEAD_DIM, (h + 1) * HEAD_DIM)
        q = q_ref[:, sl]
        s_cur = lax.dot_general(q, kc_ref[:, sl], dims, preferred_element_type=F32) * scale
        s_prev = lax.dot_general(q, kp_ref[:, sl], dims, preferred_element_type=F32) * scale
        s_cur = jnp.where(cur_ok, s_cur, NEG_BIG)
        s_prev = jnp.where(prev_ok, s_prev, NEG_BIG)
        m = jnp.maximum(jnp.max(s_cur, axis=1, keepdims=True), jnp.max(s_prev, axis=1, keepdims=True))
        p_cur = jnp.exp(s_cur - m)
        p_prev = jnp.exp(s_prev - m)
        l = jnp.sum(p_cur, axis=1, keepdims=True) + jnp.sum(p_prev, axis=1, keepdims=True)
        o = (jnp.dot(p_cur.astype(BF16), vc_ref[:, sl], preferred_element_type=F32)
             + jnp.dot(p_prev.astype(BF16), vp_ref[:, sl], preferred_element_type=F32))
        o_ref[:, sl] = o / l
        lse_ref[:, sl] = jnp.broadcast_to(m + jnp.log(l), (tq, HEAD_DIM))


def dilated_group_attention(proj, seq_len, group, slot, dilation, q_col, k_col, v_col):
    b, _, cols = proj.shape
    gw = DIL_HEADS_PER_GROUP * HEAD_DIM
    tq = DIL_GROUPS[group][0] // dilation
    sub_len = seq_len // dilation
    assert sub_len % tq == 0 and cols % gw == 0
    view = proj.reshape(b, sub_len, dilation * cols)
    per_row = cols // gw
    qb, kb, vb = (q_col // gw + slot, k_col // gw + slot, v_col // gw + slot)
    prev = lambda i: jnp.maximum(i - 1, 0)
    out_sds = jax.ShapeDtypeStruct((b, sub_len, dilation * gw), F32)
    kern = functools.partial(_dil_kernel, tq=tq, scale=HEAD_DIM ** -0.5)
    o, lse = pl.pallas_call(
        kern,
        grid=(b, dilation, sub_len // tq),
        in_specs=[pl.BlockSpec((None, tq, gw), lambda bi, c, i: (bi, i, c * per_row + qb)),
                  pl.BlockSpec((None, tq, gw), lambda bi, c, i: (bi, prev(i), c * per_row + kb)),
                  pl.BlockSpec((None, tq, gw), lambda bi, c, i: (bi, i, c * per_row + kb)),
                  pl.BlockSpec((None, tq, gw), lambda bi, c, i: (bi, prev(i), c * per_row + vb)),
                  pl.BlockSpec((None, tq, gw), lambda bi, c, i: (bi, i, c * per_row + vb))],
        out_specs=[pl.BlockSpec((None, tq, gw), lambda bi, c, i: (bi, i, c)),
                   pl.BlockSpec((None, tq, gw), lambda bi, c, i: (bi, i, c))],
        out_shape=[out_sds, out_sds],
        compiler_params=_params(("parallel", "parallel", "arbitrary")),
        name=f"dilated_attention_g{group}",
    )(view, view, view, view, view)
    return o.reshape(b * seq_len, gw), lse.reshape(b * seq_len, gw)


def _dil_merge_kernel(o0, o1, o2, l0, l1, l2, out_ref):
    m = jnp.maximum(jnp.maximum(l0[...], l1[...]), l2[...])
    e0, e1, e2 = jnp.exp(l0[...] - m), jnp.exp(l1[...] - m), jnp.exp(l2[...] - m)
    out_ref[...] = ((e0 * o0[...] + e1 * o1[...] + e2 * o2[...]) / (e0 + e1 + e2)).astype(out_ref.dtype)


def dilated_merge(outs, lses, rows=512):
    t, w = outs[0].shape
    rows = min(rows, t)
    spec = pl.BlockSpec((rows, w), lambda i: (i, 0))
    return pl.pallas_call(
        _dil_merge_kernel,
        grid=(t // rows,),
        in_specs=[spec] * 6,
        out_specs=spec,
        out_shape=jax.ShapeDtypeStruct((t, w), BF16),
        compiler_params=_params(("parallel",)),
        name="dilated_merge",
    )(*outs, *lses)


def _diff_kernel(lam_ref, g_ref, q_ref, k_ref, v_ref, o_ref, *, tq, heads, scale2, lam_init):
    i = pl.program_id(2)
    tk = 2 * tq
    lp = lam_ref[...]
    lam = (jnp.exp(jnp.sum(lp[0:1] * lp[1:2], axis=1, keepdims=True))
           - jnp.exp(jnp.sum(lp[2:3] * lp[3:4], axis=1, keepdims=True)) + lam_init)
    lane = lax.broadcasted_iota(jnp.int32, (tq, LANES), 1)
    qqs = []
    for h in range(heads):
        q = q_ref[:, h * LANES:(h + 1) * LANES].astype(F32)
        qqs.append(jnp.concatenate([jnp.where(lane < DIFF_DIM, q, 0.0), jnp.where(lane >= DIFF_DIM, q, 0.0)],
                                   axis=0).astype(BF16))
    row = lax.broadcasted_iota(jnp.int32, (2 * tq, tk), 0)
    row = jnp.where(row >= tq, row - tq, row)
    col = lax.broadcasted_iota(jnp.int32, (2 * tq, tk), 1)
    kd = lax.shift_right_logical(i, 1)
    causal = col <= row + (i - 2 * kd) * tq
    dims = (((1,), (1,)), ((), ()))

    def block(kb, state, diagonal):
        start = pl.multiple_of(kb * tk, tk)
        out = []
        for h in range(heads):
            m, l, acc = state[h]
            sl = slice(h * LANES, (h + 1) * LANES)
            k = k_ref[pl.ds(start, tk), sl]
            v = v_ref[pl.ds(start, tk), sl]
            s = lax.dot_general(qqs[h], k, dims, preferred_element_type=F32) * scale2
            if diagonal:
                s = jnp.where(causal, s, NEG_BIG)
            m_new = jnp.maximum(m, jnp.max(s, axis=1, keepdims=True))
            alpha = jnp.exp2(m - m_new)
            p = jnp.exp2(s - m_new)
            l = alpha * l + jnp.sum(p, axis=1, keepdims=True)
            acc = alpha * acc + jnp.dot(p.astype(BF16), v, preferred_element_type=F32)
            out.append((m_new, l, acc))
        return tuple(out)

    state = tuple((jnp.full((2 * tq, 1), NEG_BIG, F32), jnp.zeros((2 * tq, 1), F32),
                   jnp.zeros((2 * tq, LANES), F32)) for _ in range(heads))
    state = block(kd, state, True)
    state = lax.fori_loop(0, kd, lambda n, st: block(n, st, False), state)
    for h in range(heads):
        _, l, acc = state[h]
        o_all = acc / l
        o = o_all[:tq] - lam * o_all[tq:]
        ms = jnp.mean(o * o, axis=-1, keepdims=True)
        y = o * lax.rsqrt(ms + RMS_EPS) * g_ref[...]
        o_ref[:, h * LANES:(h + 1) * LANES] = (y * (1.0 - lam_init)).astype(o_ref.dtype)


def differential_attention(proj, lam_params, subln_g, seq_len, q_col, k_col, v_col, lam_init, tq=256, heads=4):
    b = proj.shape[0]
    tq = min(tq, seq_len // 2)
    hw = heads * LANES
    assert seq_len % (2 * tq) == 0 and DIFF_HEADS % heads == 0
    assert q_col % hw == 0 and k_col % hw == 0 and v_col % hw == 0
    qb, kb, vb = q_col // hw, k_col // hw, v_col // hw
    kern = functools.partial(_diff_kernel, tq=tq, heads=heads, scale2=DIFF_DIM ** -0.5 * LOG2_E, lam_init=lam_init)
    return pl.pallas_call(
        kern,
        grid=(b, DIFF_HEADS // heads, seq_len // tq),
        in_specs=[pl.BlockSpec((4, DIFF_DIM), lambda bi, h, i: (0, 0)),
                  pl.BlockSpec((1, 2 * DIFF_DIM), lambda bi, h, i: (0, 0)),
                  pl.BlockSpec((None, tq, hw), lambda bi, h, i: (bi, i, qb + h)),
                  pl.BlockSpec((None, seq_len, hw), lambda bi, h, i: (bi, 0, kb + h)),
                  pl.BlockSpec((None, seq_len, hw), lambda bi, h, i: (bi, 0, vb + h))],
        out_specs=pl.BlockSpec((None, tq, hw), lambda bi, h, i: (bi, i, h)),
        out_shape=jax.ShapeDtypeStruct((b, seq_len, DIFF_HEADS * 2 * DIFF_DIM), BF16),
        compiler_params=_params(("parallel", "parallel", "arbitrary")),
        name="differential_attention",
    )(lam_params, subln_g.reshape(1, -1), proj, proj, proj)


def _gate_merge_kernel(h_ref, wg0, wg1, wg2, o0, o1, o2, wb0, wb1, wb2, out_ref):
    h = h_ref[...]
    acc = None
    for wg, o, wb in ((wg0, o0, wb0), (wg1, o1, wb1), (wg2, o2, wb2)):
        gate = 1.0 / (1.0 + jnp.exp(-jnp.dot(h, wg[...], preferred_element_type=F32)))
        term = gate * jnp.dot(o[...], wb[...], preferred_element_type=F32)
        acc = term if acc is None else acc + term
    out_ref[...] = acc.astype(out_ref.dtype)


def gate_merge(h, w_gate, branch_outs, branch_ws, tm=1024, tn=256):
    t, d = h.shape
    tm, tn = min(tm, t), min(tn, d)
    nj = d // tn
    in_specs = [pl.BlockSpec((tm, d), lambda i, j: (i, 0))]
    in_specs += [pl.BlockSpec((d, tn), functools.partial(lambda i, j, b: (0, b * nj + j), b=b))
                 for b in range(N_BRANCH)]
    in_specs += [pl.BlockSpec((tm, o.shape[1]), lambda i, j: (i, 0)) for o in branch_outs]
    in_specs += [pl.BlockSpec((w.shape[0], tn), lambda i, j: (0, j)) for w in branch_ws]
    return pl.pallas_call(
        _gate_merge_kernel,
        grid=(t // tm, nj),
        in_specs=in_specs,
        out_specs=pl.BlockSpec((tm, tn), lambda i, j: (i, j)),
        out_shape=jax.ShapeDtypeStruct((t, d), BF16),
        compiler_params=_params(("parallel", "arbitrary")),
        name="gate_merge",
    )(h, w_gate, w_gate, w_gate, *branch_outs, *branch_ws)


STAT_TAU, STAT_MAX1, STAT_MAX2, STAT_INVZ = 0, 1, 2, 3
STAT_ROWS = 8


def _top_values(x, scr, count):
    for kk in range(count):
        m = jnp.max(x, axis=0, keepdims=True)
        scr[kk:kk + 1, :] = m
        x = jnp.where(x == m, -jnp.inf, x)


def _peer_route_kernel(q_ref, keys_ref, s1_ref, s2_ref, stat_ref, a_scr, b_scr, c_scr, t_scr):
    dims = (((1,), (1,)), ((), ()))
    s1 = lax.dot_general(keys_ref[0], q_ref[:, :PEER_HALF_QDIM], dims, preferred_element_type=F32)
    s2 = lax.dot_general(keys_ref[1], q_ref[:, PEER_HALF_QDIM:], dims, preferred_element_type=F32)
    s1_ref[...] = s1
    s2_ref[...] = s2
    _top_values(s1, a_scr, PEER_TOPK)
    _top_values(s2, b_scr, PEER_TOPK)
    b_top = b_scr[...]
    for ii in range(PEER_TOPK):
        c_scr[ii * PEER_TOPK:(ii + 1) * PEER_TOPK, :] = a_scr[ii:ii + 1, :] + b_top
    cand = c_scr[...]
    _top_values(cand, t_scr, PEER_TOPK)
    tau = t_scr[PEER_TOPK - 1:PEER_TOPK, :]
    best = t_scr[0:1, :]
    z = jnp.sum(jnp.where(cand >= tau, jnp.exp(cand - best), 0.0), axis=0, keepdims=True)
    stat_ref[...] = jnp.zeros_like(stat_ref)
    stat_ref[STAT_TAU:STAT_TAU + 1, :] = tau
    stat_ref[STAT_MAX1:STAT_MAX1 + 1, :] = a_scr[0:1, :]
    stat_ref[STAT_MAX2:STAT_MAX2 + 1, :] = b_scr[0:1, :]
    stat_ref[STAT_INVZ:STAT_INVZ + 1, :] = 1.0 / z


def peer_route(q, sub_keys, tb=256):
    t = q.shape[0]
    tb = min(tb, t)
    score_sds = jax.ShapeDtypeStruct((PEER_HEADS, PEER_NKEYS, t), F32)
    score_spec = pl.BlockSpec((None, PEER_NKEYS, tb), lambda i, h: (h, 0, i))
    return pl.pallas_call(
        _peer_route_kernel,
        grid=(t // tb, PEER_HEADS),
        in_specs=[pl.BlockSpec((tb, 2 * PEER_HALF_QDIM), lambda i, h: (i, h)),
                  pl.BlockSpec((None, 2, PEER_NKEYS, PEER_HALF_QDIM), lambda i, h: (h, 0, 0, 0))],
        out_specs=[score_spec, score_spec, pl.BlockSpec((None, STAT_ROWS, tb), lambda i, h: (h, 0, i))],
        out_shape=[score_sds, score_sds, jax.ShapeDtypeStruct((PEER_HEADS, STAT_ROWS, t), F32)],
        scratch_shapes=[pltpu.VMEM((PEER_TOPK, tb), F32), pltpu.VMEM((PEER_TOPK, tb), F32),
                        pltpu.VMEM((PEER_TOPK * PEER_TOPK, tb), F32), pltpu.VMEM((PEER_TOPK, tb), F32)],
        compiler_params=_params(("parallel", "arbitrary")),
        name="peer_route",
    )(q, sub_keys)


def _gelu(a):
    return 0.5 * a * (1.0 + lax.erf(a * (2.0 ** -0.5)))


PEER_STAGES = 3
PEER_EXPERT_BLOCK = 256
SCORE_PIECES = 4


def _peer_dense_kernel(ht_ref, u_ref, vt_ref, s1_ref, s2_ref, stat_ref, out_ref, e2_scr,
                       act_a, act_b, coef_a, coef_b, *, te, n_blocks):
    e = pl.program_id(1)

    @pl.when(e == 0)
    def _():
        out_ref[...] = jnp.zeros_like(out_ref)
        for scr in (act_a, act_b, coef_a, coef_b):
            scr[...] = jnp.zeros_like(scr)
        for h in range(PEER_HEADS):
            e2_scr[h] = jnp.exp(s2_ref[h] - stat_ref[h, STAT_MAX2:STAT_MAX2 + 1, :])

    n_sub = te // PEER_NKEYS
    d_model, tb = ht_ref.shape
    gate_block = jnp.clip(e - 1, 0, n_blocks - 1)

    def stages(act_cur, act_prev, coef_cur, coef_prev):
        n_lane = tb // LANES
        n_slices = n_sub * n_lane
        kc = d_model // n_slices
        def score_piece(p):
            ks = slice(p * (d_model // SCORE_PIECES), (p + 1) * (d_model // SCORE_PIECES))
            part = jnp.dot(u_ref[:, ks], ht_ref[ks, :], preferred_element_type=F32)
            if p == 0:
                act_cur[...] = part
            else:
                act_cur[...] += part

        score_piece(0)
        assert n_slices >= SCORE_PIECES
        piece_after_slice = {p * n_slices // SCORE_PIECES - 1: p for p in range(1, SCORE_PIECES)}
        s1_rows, e1_rows = [], []
        for sub in range(n_sub):
            i_idx = gate_block * n_sub + sub
            s1_rows.append([s1_ref[h, pl.ds(i_idx, 1), :] for h in range(PEER_HEADS)])
            e1_rows.append([jnp.exp(s1_rows[sub][h] - stat_ref[h, STAT_MAX1:STAT_MAX1 + 1, :])
                            * stat_ref[h, STAT_INVZ:STAT_INVZ + 1, :] for h in range(PEER_HEADS)])
        half = PEER_NKEYS // 2
        for r in range(n_slices):
            c, jh = divmod(r, 2)
            lanes = slice(c * LANES, (c + 1) * LANES)
            keys = slice(jh * half, (jh + 1) * half)
            gates = [None] * n_sub
            for h in range(PEER_HEADS):
                s2_tile = s2_ref[h, keys, lanes]
                e2_tile = e2_scr[h, keys, lanes]
                tau = stat_ref[h, STAT_TAU:STAT_TAU + 1, lanes]
                for sub in range(n_sub):
                    term = jnp.where(s2_tile + s1_rows[sub][h][:, lanes] >= tau,
                                     e2_tile * e1_rows[sub][h][:, lanes], 0.0)
                    gates[sub] = term if gates[sub] is None else gates[sub] + term
            for sub in range(n_sub):
                rows = slice(sub * PEER_NKEYS + jh * half, sub * PEER_NKEYS + (jh + 1) * half)
                coef_prev[rows, lanes] = (gates[sub] * _gelu(act_prev[rows, lanes])).astype(BF16)
            chunk = slice(r * kc, (r + 1) * kc)
            out_ref[chunk, :] += jnp.dot(vt_ref[chunk, :], coef_cur[...], preferred_element_type=F32)
            if r in piece_after_slice:
                score_piece(piece_after_slice[r])

    parity = lax.rem(e, 2)

    @pl.when(parity == 0)
    def _():
        stages(act_a, act_b, coef_a, coef_b)

    @pl.when(parity == 1)
    def _():
        stages(act_b, act_a, coef_b, coef_a)


def peer_dense(ht, u_tab, vt_blocks, s1, s2, stats, tb=512):
    d, t = ht.shape
    n_blocks, _, te = vt_blocks.shape
    tb = min(tb, t)
    assert t % tb == 0 and u_tab.shape[0] == n_blocks * te and te % PEER_NKEYS == 0
    tok_spec = pl.BlockSpec((PEER_HEADS, PEER_NKEYS, tb), lambda i, e: (0, 0, i))
    last = n_blocks - 1
    kern = functools.partial(_peer_dense_kernel, te=te, n_blocks=n_blocks)
    return pl.pallas_call(
        kern,
        grid=(t // tb, n_blocks + PEER_STAGES - 1),
        in_specs=[pl.BlockSpec((d, tb), lambda i, e: (0, i)),
                  pl.BlockSpec((te, d), lambda i, e: (jnp.minimum(e, last), 0)),
                  pl.BlockSpec((None, d, te), lambda i, e: (jnp.clip(e - 2, 0, last), 0, 0)),
                  tok_spec, tok_spec,
                  pl.BlockSpec((PEER_HEADS, STAT_ROWS, tb), lambda i, e: (0, 0, i))],
        out_specs=pl.BlockSpec((d, tb), lambda i, e: (0, i)),
        out_shape=jax.ShapeDtypeStruct((d, t), F32),
        scratch_shapes=[pltpu.VMEM((PEER_HEADS, PEER_NKEYS, tb), F32),
                        pltpu.VMEM((te, tb), F32), pltpu.VMEM((te, tb), F32),
                        pltpu.VMEM((te, tb), BF16), pltpu.VMEM((te, tb), BF16)],
        compiler_params=_params(("parallel", "arbitrary")),
        name="peer_dense",
    )(ht, u_tab, vt_blocks, s1, s2, stats)


def kernel(x, attn_norm_g, ffn_norm_g, final_norm_g, w_qkv, w_gate, w_branch_sb, w_branch_dil,
           w_branch_diff, w_out, diff_lambda, diff_subln_g, peer_w_q, peer_sub_keys, peer_u, peer_v):
    b, s, d = x.shape
    t = b * s
    depth = w_qkv.shape[0]
    sb_w = SB_HEADS * HEAD_DIM
    dil_w = DIL_HEADS_PER_GROUP * len(DIL_GROUPS) * HEAD_DIM
    diff_w = DIFF_HEADS * 2 * DIFF_DIM
    dl_q, dl_k, dl_v = 3 * sb_w, 3 * sb_w + dil_w, 3 * sb_w + 2 * dil_w
    df_q = 3 * sb_w + 3 * dil_w
    df_k, df_v = df_q + diff_w, df_q + 2 * diff_w
    qkv_cols = df_v + diff_w
    rope_ops = _rope_operands(s)

    xt = x.reshape(t, d)
    for layer in range(depth):
        h = rmsnorm(xt, attn_norm_g[layer], BF16)
        proj = qkv_projection(h, w_qkv[layer].astype(BF16), rope_ops, s,
                              rope128_cols=(dl_q, dl_v), rope64_cols=(df_q, df_v))
        proj3 = proj.reshape(b, s, qkv_cols)
        o_sb = stick_breaking_attention(proj3, s, 0).reshape(t, sb_w)
        dil = []
        gw = DIL_HEADS_PER_GROUP * HEAD_DIM
        for g, (_, dilation) in enumerate(DIL_GROUPS):
            if dilation == 1:
                dil.append(dilated_group_attention(proj3, s, g, g, dilation, dl_q, dl_k, dl_v))
            else:
                cols = jnp.concatenate([proj3[:, :, c + g * gw:c + (g + 1) * gw] for c in (dl_q, dl_k, dl_v)], -1)
                dil.append(dilated_group_attention(cols, s, g, 0, dilation, 0, gw, 2 * gw))
        o_dl = dilated_merge([o for o, _ in dil], [l for _, l in dil])
        lam_init = 0.8 - 0.6 * math.exp(-0.3 * layer)
        o_df = differential_attention(proj3, diff_lambda[layer], diff_subln_g[layer], s,
                                      df_q, df_k, df_v, lam_init).reshape(t, diff_w)
        merged = gate_merge(h, w_gate[layer].astype(BF16), (o_sb, o_dl, o_df),
                            (w_branch_sb[layer].astype(BF16), w_branch_dil[layer].astype(BF16),
                             w_branch_diff[layer].astype(BF16)))
        xt = matmul(merged, w_out[layer].astype(BF16), F32, residual=xt)

        h2 = rmsnorm(xt, ffn_norm_g[layer], BF16)
        q = matmul(h2, peer_w_q[layer].astype(BF16), BF16)
        s1, s2, stats = peer_route(q, peer_sub_keys[layer].astype(BF16))
        vt_blocks = peer_v[layer].reshape(-1, PEER_EXPERT_BLOCK, d).transpose(0, 2, 1).astype(BF16)
        out_t = peer_dense(h2.T, peer_u[layer].astype(BF16), vt_blocks, s1, s2, stats)
        xt = xt + out_t.T
    return rmsnorm(xt, final_norm_g, F32).reshape(b, s, d)
```

```python
import functools
import math

import jax
import jax.numpy as jnp
from jax import lax
from jax.experimental import pallas as pl
from jax.experimental.pallas import tpu as pltpu

F32 = jnp.float32
BF16 = jnp.bfloat16

HEAD_DIM = 128
ROPE_THETA = 10000.0
RMS_EPS = 1e-6
NEG_BIG = -1e30

SB_HEADS = 8
DIL_GROUPS = ((128, 1), (512, 4), (2048, 16))
DIL_HEADS_PER_GROUP = 4
DIFF_HEADS = 8
DIFF_DIM = 64
N_BRANCH = 3

PEER_HEADS = 8
PEER_NKEYS = 128
PEER_HALF_QDIM = 128
PEER_TOPK = 16
PEER_CANDIDATES = 16 + 7 * 8 + 8

LANES = 128
VMEM_LIMIT = 56 * 1024 * 1024


def _params(sem, vmem=VMEM_LIMIT):
    return pltpu.CompilerParams(dimension_semantics=sem, vmem_limit_bytes=vmem)


def _rmsnorm_kernel(x_ref, g_ref, o_ref):
    x = x_ref[...]
    ms = jnp.mean(x * x, axis=-1, keepdims=True)
    o_ref[...] = (x * lax.rsqrt(ms + RMS_EPS) * g_ref[...]).astype(o_ref.dtype)


def _rmsnorm_both_kernel(x_ref, g_ref, o_ref, ot_ref):
    x = x_ref[...]
    ms = jnp.mean(x * x, axis=-1, keepdims=True)
    y = x * lax.rsqrt(ms + RMS_EPS) * g_ref[...]
    o_ref[...] = y.astype(o_ref.dtype)
    ot_ref[...] = y.T.astype(ot_ref.dtype)


def rmsnorm(x, g, out_dtype, rows=256, with_transpose=False):
    t, d = x.shape
    rows = min(rows, t)
    row_spec = pl.BlockSpec((rows, d), lambda i: (i, 0))
    out_specs, out_shape = row_spec, jax.ShapeDtypeStruct((t, d), out_dtype)
    if with_transpose:
        out_specs = [row_spec, pl.BlockSpec((d, rows), lambda i: (0, i))]
        out_shape = [out_shape, jax.ShapeDtypeStruct((d, t), out_dtype)]
    return pl.pallas_call(
        _rmsnorm_both_kernel if with_transpose else _rmsnorm_kernel,
        grid=(t // rows,),
        in_specs=[row_spec, pl.BlockSpec((1, d), lambda i: (0, 0))],
        out_specs=out_specs,
        out_shape=out_shape,
        compiler_params=_params(("parallel",)),
        name="rmsnorm_transposed" if with_transpose else "rmsnorm",
    )(x, g.reshape(1, d))


def _rope_tables(seq_len, dim):
    inv_freq = 1.0 / (ROPE_THETA ** (jnp.arange(0, dim, 2, dtype=F32) / dim))
    ang = jnp.arange(seq_len, dtype=F32)[:, None] * inv_freq[None, :]
    ang = jnp.concatenate([ang, ang], axis=-1)
    return jnp.cos(ang), jnp.sin(ang)


def _rope_operands(seq_len):
    cos_h, sin_h = _rope_tables(seq_len, HEAD_DIM)
    lane = jnp.arange(LANES)
    sin_h_signed = jnp.where(lane < HEAD_DIM // 2, -sin_h, sin_h)
    cos_d, sin_d = _rope_tables(seq_len, DIFF_DIM)
    cos_d2 = jnp.concatenate([cos_d, cos_d], axis=-1)
    sin_d2 = jnp.concatenate([sin_d, sin_d], axis=-1)
    low = (lane % DIFF_DIM) < DIFF_DIM // 2
    sin_d_low = jnp.where(low, -sin_d2, 0.0)
    sin_d_high = jnp.where(low, 0.0, sin_d2)
    return cos_h, sin_h_signed, cos_d2, sin_d_low, sin_d_high


def _qkv_kernel(a_ref, w_ref, cos_h, sin_h, cos_d, sin_dl, sin_dh, o_ref, *, rope128_blocks, rope64_blocks):
    j = pl.program_id(1)
    acc = jnp.dot(a_ref[...], w_ref[...], preferred_element_type=F32)
    n_chunks = acc.shape[1] // LANES
    in128 = (j >= rope128_blocks[0]) & (j < rope128_blocks[1])
    in64 = (j >= rope64_blocks[0]) & (j < rope64_blocks[1])

    @pl.when(in128)
    def _():
        for c in range(n_chunks):
            x = acc[:, c * LANES:(c + 1) * LANES]
            y = x * cos_h[...] + pltpu.roll(x, HEAD_DIM // 2, 1) * sin_h[...]
            o_ref[:, c * LANES:(c + 1) * LANES] = y.astype(o_ref.dtype)

    @pl.when(in64)
    def _():
        for c in range(n_chunks):
            x = acc[:, c * LANES:(c + 1) * LANES]
            y = (x * cos_d[...] + pltpu.roll(x, LANES - DIFF_DIM // 2, 1) * sin_dl[...]
                 + pltpu.roll(x, DIFF_DIM // 2, 1) * sin_dh[...])
            o_ref[:, c * LANES:(c + 1) * LANES] = y.astype(o_ref.dtype)

    @pl.when(jnp.logical_not(in128 | in64))
    def _():
        o_ref[...] = acc.astype(o_ref.dtype)


def qkv_projection(h, w, rope_ops, seq_len, rope128_cols, rope64_cols, tm=1024, tn=512):
    m, k = h.shape
    n = w.shape[1]
    tm = min(tm, seq_len)
    assert seq_len % tm == 0 and m % tm == 0 and n % tn == 0
    for lo, hi in (rope128_cols, rope64_cols):
        assert lo % tn == 0 and hi % tn == 0
    seq_blocks = seq_len // tm
    tab_spec = pl.BlockSpec((tm, LANES), lambda i, j: (i % seq_blocks, 0))
    kern = functools.partial(
        _qkv_kernel,
        rope128_blocks=(rope128_cols[0] // tn, rope128_cols[1] // tn),
        rope64_blocks=(rope64_cols[0] // tn, rope64_cols[1] // tn))
    return pl.pallas_call(
        kern,
        grid=(m // tm, n // tn),
        in_specs=[pl.BlockSpec((tm, k), lambda i, j: (i, 0)),
                  pl.BlockSpec((k, tn), lambda i, j: (0, j)),
                  tab_spec, tab_spec, tab_spec, tab_spec, tab_spec],
        out_specs=pl.BlockSpec((tm, tn), lambda i, j: (i, j)),
        out_shape=jax.ShapeDtypeStruct((m, n), BF16),
        compiler_params=_params(("parallel", "arbitrary")),
        name="qkv_projection",
    )(h, w, *rope_ops)


def _matmul_kernel(a_ref, w_ref, o_ref):
    o_ref[...] = jnp.dot(a_ref[...], w_ref[...], preferred_element_type=F32).astype(o_ref.dtype)


def _matmul_residual_kernel(a_ref, w_ref, r_ref, o_ref):
    o_ref[...] = r_ref[...] + jnp.dot(a_ref[...], w_ref[...], preferred_element_type=F32)


def matmul(a, w, out_dtype, residual=None, tm=1024, tn=512):
    m, k = a.shape
    n = w.shape[1]
    tm, tn = min(tm, m), min(tn, n)
    assert m % tm == 0 and n % tn == 0
    in_specs = [pl.BlockSpec((tm, k), lambda i, j: (i, 0)), pl.BlockSpec((k, tn), lambda i, j: (0, j))]
    args = [a, w]
    kern = _matmul_kernel
    if residual is not None:
        in_specs.append(pl.BlockSpec((tm, tn), lambda i, j: (i, j)))
        args.append(residual)
        kern = _matmul_residual_kernel
    return pl.pallas_call(
        kern,
        grid=(m // tm, n // tn),
        in_specs=in_specs,
        out_specs=pl.BlockSpec((tm, tn), lambda i, j: (i, j)),
        out_shape=jax.ShapeDtypeStruct((m, n), out_dtype),
        compiler_params=_params(("parallel", "arbitrary")),
        name="matmul_residual" if residual is not None else "matmul",
    )(*args)


LOG2_E = math.log2(math.e)
SIGN_BIT = 0x80000000


def _softplus2(u):
    neg_abs = lax.bitcast_convert_type(lax.bitcast_convert_type(u, jnp.uint32) | jnp.uint32(SIGN_BIT), F32)
    return jnp.maximum(u, 0.0) + jnp.log2(1.0 + jnp.exp2(neg_abs))


def _sb_kernel(suffix_ref, q_ref, k_ref, v_ref, o_ref, *, tq, heads, scale2):
    i = pl.program_id(2)
    tk = 2 * tq
    suffix = suffix_ref[...]
    row = lax.broadcasted_iota(jnp.int32, (tq, tk), 0)
    col = lax.broadcasted_iota(jnp.int32, (tq, tk), 1)
    kd = lax.shift_right_logical(i, 1)
    before = col < row + (i - 2 * kd) * tq
    dims = (((1,), (1,)), ((), ()))
    qs = [q_ref[:, h * HEAD_DIM:(h + 1) * HEAD_DIM] for h in range(heads)]

    def block(kb, state, diagonal):
        start = pl.multiple_of(kb * tk, tk)
        out = []
        for h in range(heads):
            carry, acc = state[h]
            sl = slice(h * HEAD_DIM, (h + 1) * HEAD_DIM)
            k = k_ref[pl.ds(start, tk), sl]
            v = v_ref[pl.ds(start, tk), sl]
            z = lax.dot_general(qs[h], k, dims, preferred_element_type=F32) * scale2
            sp = _softplus2(z)
            spm = jnp.where(before, sp, 0.0) if diagonal else sp
            spb = spm.astype(BF16)
            sum_lo = jnp.sum(spm[:, :tq], axis=1, keepdims=True)
            sum_hi = jnp.sum(spm[:, tq:], axis=1, keepdims=True)
            later_lo = jnp.dot(spb[:, :tq], suffix, preferred_element_type=F32) + (carry + sum_hi)
            later_hi = jnp.dot(spb[:, tq:], suffix, preferred_element_type=F32) + carry
            w = jnp.exp2(z - sp - jnp.concatenate([later_lo, later_hi], axis=1))
            if diagonal:
                w = jnp.where(before, w, 0.0)
            acc = acc + jnp.dot(w.astype(BF16), v, preferred_element_type=F32)
            out.append((carry + (sum_lo + sum_hi), acc))
        return tuple(out)

    state = tuple((jnp.zeros((tq, 1), F32), jnp.zeros((tq, HEAD_DIM), F32)) for _ in range(heads))
    state = block(kd, state, True)
    state = lax.fori_loop(0, kd, lambda n, st: block(kd - 1 - n, st, False), state)
    for h in range(heads):
        o_ref[:, h * HEAD_DIM:(h + 1) * HEAD_DIM] = state[h][1].astype(o_ref.dtype)


def stick_breaking_attention(proj, seq_len, col0, tq=256, heads=4):
    b = proj.shape[0]
    tq = min(tq, seq_len // 2)
    hw = heads * HEAD_DIM
    assert seq_len % (2 * tq) == 0 and SB_HEADS % heads == 0 and col0 % hw == 0
    c0 = col0 // hw
    per = SB_HEADS // heads
    idx = jnp.arange(tq)
    suffix = (idx[:, None] > idx[None, :]).astype(BF16)
    kern = functools.partial(_sb_kernel, tq=tq, heads=heads, scale2=HEAD_DIM ** -0.5 * LOG2_E)
    return pl.pallas_call(
        kern,
        grid=(b, per, seq_len // tq),
        in_specs=[pl.BlockSpec((tq, tq), lambda bi, h, i: (0, 0)),
                  pl.BlockSpec((None, tq, hw), lambda bi, h, i: (bi, i, c0 + h)),
                  pl.BlockSpec((None, seq_len, hw), lambda bi, h, i: (bi, 0, c0 + per + h)),
                  pl.BlockSpec((None, seq_len, hw), lambda bi, h, i: (bi, 0, c0 + 2 * per + h))],
        out_specs=pl.BlockSpec((None, tq, hw), lambda bi, h, i: (bi, i, h)),
        out_shape=jax.ShapeDtypeStruct((b, seq_len, SB_HEADS * HEAD_DIM), BF16),
        compiler_params=_params(("parallel", "parallel", "arbitrary")),
        name="stick_breaking_attention",
    )(suffix, proj, proj, proj)


def _dil_kernel(q_ref, kp_ref, kc_ref, vp_ref, vc_ref, o_ref, lse_ref, *, tq, scale):
    i = pl.program_id(2)
    row = lax.broadcasted_iota(jnp.int32, (tq, tq), 0)
    col = lax.broadcasted_iota(jnp.int32, (tq, tq), 1)
    cur_ok = col <= row
    prev_ok = col >= row + jnp.where(i > 0, 0, tq)
    dims = (((1,), (1,)), ((), ()))
    for h in range(DIL_HEADS_PER_GROUP):
        sl = slice(h * HEAD_DIM, (h + 1) * HEAD_DIM)
        q = q_ref[:, sl]
        s_cur = lax.dot_general(q, kc_ref[:, sl], dims, preferred_element_type=F32) * scale
        s_prev = lax.dot_general(q, kp_ref[:, sl], dims, preferred_element_type=F32) * scale
        s_cur = jnp.where(cur_ok, s_cur, NEG_BIG)
        s_prev = jnp.where(prev_ok, s_prev, NEG_BIG)
        m = jnp.maximum(jnp.max(s_cur, axis=1, keepdims=True), jnp.max(s_prev, axis=1, keepdims=True))
        p_cur = jnp.exp(s_cur - m)
        p_prev = jnp.exp(s_prev - m)
        l = jnp.sum(p_cur, axis=1, keepdims=True) + jnp.sum(p_prev, axis=1, keepdims=True)
        o = (jnp.dot(p_cur.astype(BF16), vc_ref[:, sl], preferred_element_type=F32)
             + jnp.dot(p_prev.astype(BF16), vp_ref[:, sl], preferred_element_type=F32))
        o_ref[:, sl] = o / l
        lse_ref[:, sl] = jnp.broadcast_to(m + jnp.log(l), (tq, HEAD_DIM))


def _regroup_kernel(q_ref, k_ref, v_ref, qo_ref, ko_ref, vo_ref, scr, *, dilation):
    rows, gw = q_ref.shape
    n = rows // dilation
    for src, dst in ((q_ref, qo_ref), (k_ref, ko_ref), (v_ref, vo_ref)):
        for j in range(gw // LANES):
            scr[j] = src[:, j * LANES:(j + 1) * LANES].astype(F32)
        for c in range(dilation):
            for j in range(gw // LANES):
                dst[:, c * gw + j * LANES:c * gw + (j + 1) * LANES] = (
                    scr[j, pl.ds(c, n, stride=dilation), :].astype(dst.dtype))


def dilated_regroup(proj, dilation, q_col, k_col, v_col, rows=512):
    t, cols = proj.shape
    gw = DIL_HEADS_PER_GROUP * HEAD_DIM
    rows = min(rows, t)
    assert t % rows == 0 and rows % (16 * dilation) == 0
    in_specs = [pl.BlockSpec((rows, gw), functools.partial(lambda i, blk: (i, blk), blk=col // gw))
                for col in (q_col, k_col, v_col)]
    out_spec = pl.BlockSpec((rows // dilation, dilation * gw), lambda i: (i, 0))
    out_sds = jax.ShapeDtypeStruct((t // dilation, dilation * gw), proj.dtype)
    return pl.pallas_call(
        functools.partial(_regroup_kernel, dilation=dilation),
        grid=(t // rows,),
        in_specs=in_specs,
        out_specs=[out_spec] * 3,
        out_shape=[out_sds] * 3,
        scratch_shapes=[pltpu.VMEM((gw // LANES, rows, LANES), F32)],
        compiler_params=_params(("parallel",)),
        name=f"dilated_regroup_r{dilation}",
    )(proj, proj, proj)


def dilated_group_attention(q_arr, k_arr, v_arr, batch, seq_len, group, dilation, q_col, k_col, v_col):
    gw = DIL_HEADS_PER_GROUP * HEAD_DIM
    tq = DIL_GROUPS[group][0] // dilation
    sub_len = seq_len // dilation
    assert sub_len % tq == 0
    views, blocks, per_row = [], [], []
    for arr, col in ((q_arr, q_col), (k_arr, k_col), (v_arr, v_col)):
        width = arr.shape[1] // dilation
        assert width % gw == 0 and col % gw == 0
        views.append(arr.reshape(batch, sub_len, dilation * width))
        blocks.append(col // gw)
        per_row.append(width // gw)
    prev = lambda i: jnp.maximum(i - 1, 0)

    def spec(which, row_of):
        return pl.BlockSpec((None, tq, gw),
                            lambda bi, c, i: (bi, row_of(i), c * per_row[which] + blocks[which]))

    out_sds = jax.ShapeDtypeStruct((batch, sub_len, dilation * gw), F32)
    out_spec = pl.BlockSpec((None, tq, gw), lambda bi, c, i: (bi, i, c))
    kern = functools.partial(_dil_kernel, tq=tq, scale=HEAD_DIM ** -0.5)
    o, lse = pl.pallas_call(
        kern,
        grid=(batch, dilation, sub_len // tq),
        in_specs=[spec(0, lambda i: i), spec(1, prev), spec(1, lambda i: i), spec(2, prev), spec(2, lambda i: i)],
        out_specs=[out_spec, out_spec],
        out_shape=[out_sds, out_sds],
        compiler_params=_params(("parallel", "parallel", "arbitrary")),
        name=f"dilated_attention_g{group}",
    )(views[0], views[1], views[1], views[2], views[2])
    return o.reshape(batch * sub_len, dilation * gw), lse.reshape(batch * sub_len, dilation * gw)


def _dil_merge_kernel(*refs, dilations):
    n = len(dilations)
    o_refs, l_refs, out_ref, scratch = refs[:n], refs[n:2 * n], refs[2 * n], refs[2 * n + 1:]
    rows, gw = out_ref.shape
    outs, lses = [], []
    for g, dilation in enumerate(dilations):
        vals = []
        for src, scr in ((o_refs[g], scratch[2 * g]), (l_refs[g], scratch[2 * g + 1])):
            if dilation == 1:
                vals.append(src[...])
            else:
                for c in range(dilation):
                    for j in range(gw // LANES):
                        scr[j, pl.ds(c, rows // dilation, stride=dilation), :] = (
                            src[:, c * gw + j * LANES:c * gw + (j + 1) * LANES])
                vals.append(jnp.concatenate([scr[j] for j in range(gw // LANES)], axis=1))
        outs.append(vals[0])
        lses.append(vals[1])
    m = functools.reduce(jnp.maximum, lses)
    es = [jnp.exp(l - m) for l in lses]
    num = functools.reduce(lambda a, b: a + b, [e * o for e, o in zip(es, outs)])
    den = functools.reduce(lambda a, b: a + b, es)
    out_ref[...] = (num / den).astype(out_ref.dtype)


def dilated_merge(outs, lses, dilations, rows=512):
    gw = DIL_HEADS_PER_GROUP * HEAD_DIM
    t = outs[0].shape[0] * dilations[0]
    rows = min(rows, t)
    specs = [pl.BlockSpec((rows // r, r * gw), lambda i: (i, 0)) for r in dilations]
    return pl.pallas_call(
        functools.partial(_dil_merge_kernel, dilations=tuple(dilations)),
        grid=(t // rows,),
        in_specs=specs + specs,
        out_specs=pl.BlockSpec((rows, gw), lambda i: (i, 0)),
        out_shape=jax.ShapeDtypeStruct((t, gw), BF16),
        scratch_shapes=[pltpu.VMEM((gw // LANES, rows, LANES), F32) for _ in range(2 * len(dilations))],
        compiler_params=_params(("parallel",)),
        name="dilated_merge",
    )(*outs, *lses)


def _diff_kernel(lam_ref, g_ref, q_ref, k_ref, v_ref, o_ref, *, tq, heads, scale2, lam_init):
    i = pl.program_id(2)
    tk = 2 * tq
    lp = lam_ref[...]
    lam = (jnp.exp(jnp.sum(lp[0:1] * lp[1:2], axis=1, keepdims=True))
           - jnp.exp(jnp.sum(lp[2:3] * lp[3:4], axis=1, keepdims=True)) + lam_init)
    lane = lax.broadcasted_iota(jnp.int32, (tq, LANES), 1)
    qqs = []
    for h in range(heads):
        q = q_ref[:, h * LANES:(h + 1) * LANES].astype(F32)
        qqs.append(jnp.concatenate([jnp.where(lane < DIFF_DIM, q, 0.0), jnp.where(lane >= DIFF_DIM, q, 0.0)],
                                   axis=0).astype(BF16))
    row = lax.broadcasted_iota(jnp.int32, (2 * tq, tk), 0)
    row = jnp.where(row >= tq, row - tq, row)
    col = lax.broadcasted_iota(jnp.int32, (2 * tq, tk), 1)
    kd = lax.shift_right_logical(i, 1)
    causal = col <= row + (i - 2 * kd) * tq
    dims = (((1,), (1,)), ((), ()))

    def block(kb, state, diagonal):
        start = pl.multiple_of(kb * tk, tk)
        out = []
        for h in range(heads):
            m, l, acc = state[h]
            sl = slice(h * LANES, (h + 1) * LANES)
            k = k_ref[pl.ds(start, tk), sl]
            v = v_ref[pl.ds(start, tk), sl]
            s = lax.dot_general(qqs[h], k, dims, preferred_element_type=F32) * scale2
            if diagonal:
                s = jnp.where(causal, s, NEG_BIG)
            m_new = jnp.maximum(m, jnp.max(s, axis=1, keepdims=True))
            alpha = jnp.exp2(m - m_new)
            p = jnp.exp2(s - m_new)
            l = alpha * l + jnp.sum(p, axis=1, keepdims=True)
            acc = alpha * acc + jnp.dot(p.astype(BF16), v, preferred_element_type=F32)
            out.append((m_new, l, acc))
        return tuple(out)

    state = tuple((jnp.full((2 * tq, 1), NEG_BIG, F32), jnp.zeros((2 * tq, 1), F32),
                   jnp.zeros((2 * tq, LANES), F32)) for _ in range(heads))
    state = block(kd, state, True)
    state = lax.fori_loop(0, kd, lambda n, st: block(n, st, False), state)
    for h in range(heads):
        _, l, acc = state[h]
        o_all = acc / l
        o = o_all[:tq] - lam * o_all[tq:]
        ms = jnp.mean(o * o, axis=-1, keepdims=True)
        y = o * lax.rsqrt(ms + RMS_EPS) * g_ref[...]
        o_ref[:, h * LANES:(h + 1) * LANES] = (y * (1.0 - lam_init)).astype(o_ref.dtype)


def differential_attention(proj, lam_params, subln_g, seq_len, q_col, k_col, v_col, lam_init, tq=256, heads=4):
    b = proj.shape[0]
    tq = min(tq, seq_len // 2)
    hw = heads * LANES
    assert seq_len % (2 * tq) == 0 and DIFF_HEADS % heads == 0
    assert q_col % hw == 0 and k_col % hw == 0 and v_col % hw == 0
    qb, kb, vb = q_col // hw, k_col // hw, v_col // hw
    kern = functools.partial(_diff_kernel, tq=tq, heads=heads, scale2=DIFF_DIM ** -0.5 * LOG2_E, lam_init=lam_init)
    return pl.pallas_call(
        kern,
        grid=(b, DIFF_HEADS // heads, seq_len // tq),
        in_specs=[pl.BlockSpec((4, DIFF_DIM), lambda bi, h, i: (0, 0)),
                  pl.BlockSpec((1, 2 * DIFF_DIM), lambda bi, h, i: (0, 0)),
                  pl.BlockSpec((None, tq, hw), lambda bi, h, i: (bi, i, qb + h)),
                  pl.BlockSpec((None, seq_len, hw), lambda bi, h, i: (bi, 0, kb + h)),
                  pl.BlockSpec((None, seq_len, hw), lambda bi, h, i: (bi, 0, vb + h))],
        out_specs=pl.BlockSpec((None, tq, hw), lambda bi, h, i: (bi, i, h)),
        out_shape=jax.ShapeDtypeStruct((b, seq_len, DIFF_HEADS * 2 * DIFF_DIM), BF16),
        compiler_params=_params(("parallel", "parallel", "arbitrary")),
        name="differential_attention",
    )(lam_params, subln_g.reshape(1, -1), proj, proj, proj)


def _gate_merge_kernel(h_ref, wg0, wg1, wg2, o0, o1, o2, wb0, wb1, wb2, out_ref):
    h = h_ref[...]
    acc = None
    for wg, o, wb in ((wg0, o0, wb0), (wg1, o1, wb1), (wg2, o2, wb2)):
        gate = 1.0 / (1.0 + jnp.exp(-jnp.dot(h, wg[...], preferred_element_type=F32)))
        term = gate * jnp.dot(o[...], wb[...], preferred_element_type=F32)
        acc = term if acc is None else acc + term
    out_ref[...] = acc.astype(out_ref.dtype)


def gate_merge(h, w_gate, branch_outs, branch_ws, tm=1024, tn=256):
    t, d = h.shape
    tm, tn = min(tm, t), min(tn, d)
    nj = d // tn
    in_specs = [pl.BlockSpec((tm, d), lambda i, j: (i, 0))]
    in_specs += [pl.BlockSpec((d, tn), functools.partial(lambda i, j, b: (0, b * nj + j), b=b))
                 for b in range(N_BRANCH)]
    in_specs += [pl.BlockSpec((tm, o.shape[1]), lambda i, j: (i, 0)) for o in branch_outs]
    in_specs += [pl.BlockSpec((w.shape[0], tn), lambda i, j: (0, j)) for w in branch_ws]
    return pl.pallas_call(
        _gate_merge_kernel,
        grid=(t // tm, nj),
        in_specs=in_specs,
        out_specs=pl.BlockSpec((tm, tn), lambda i, j: (i, j)),
        out_shape=jax.ShapeDtypeStruct((t, d), BF16),
        compiler_params=_params(("parallel", "arbitrary")),
        name="gate_merge",
    )(h, w_gate, w_gate, w_gate, *branch_outs, *branch_ws)


STAT_TAU, STAT_MAX1, STAT_MAX2, STAT_INVZ = 0, 1, 2, 3
STAT_ROWS = 8


def _top_values(x, scr, count):
    for kk in range(count):
        m = jnp.max(x, axis=0, keepdims=True)
        scr[kk:kk + 1, :] = m
        x = jnp.where(x == m, -jnp.inf, x)


def _peer_route_kernel(q_ref, keys_ref, s1_ref, s2_ref, stat_ref, a_scr, b_scr, c_scr, t_scr):
    dims = (((1,), (1,)), ((), ()))
    s1 = lax.dot_general(keys_ref[0], q_ref[:, :PEER_HALF_QDIM], dims, preferred_element_type=F32)
    s2 = lax.dot_general(keys_ref[1], q_ref[:, PEER_HALF_QDIM:], dims, preferred_element_type=F32)
    s1_ref[...] = s1
    s2_ref[...] = s2
    _top_values(s1, a_scr, PEER_TOPK)
    _top_values(s2, b_scr, PEER_TOPK)
    half = PEER_TOPK // 2
    c_scr[0:PEER_TOPK, :] = a_scr[0:1, :] + b_scr[...]
    for ii in range(1, half):
        c_scr[PEER_TOPK + (ii - 1) * half:PEER_TOPK + ii * half, :] = a_scr[ii:ii + 1, :] + b_scr[0:half, :]
    c_scr[PEER_TOPK + (half - 1) * half:PEER_CANDIDATES, :] = a_scr[half:PEER_TOPK, :] + b_scr[0:1, :]
    cand = c_scr[...]
    _top_values(cand, t_scr, PEER_TOPK)
    tau = t_scr[PEER_TOPK - 1:PEER_TOPK, :]
    best = t_scr[0:1, :]
    z = jnp.sum(jnp.where(cand >= tau, jnp.exp(cand - best), 0.0), axis=0, keepdims=True)
    stat_ref[...] = jnp.zeros_like(stat_ref)
    stat_ref[STAT_TAU:STAT_TAU + 1, :] = tau
    stat_ref[STAT_MAX1:STAT_MAX1 + 1, :] = a_scr[0:1, :]
    stat_ref[STAT_MAX2:STAT_MAX2 + 1, :] = b_scr[0:1, :]
    stat_ref[STAT_INVZ:STAT_INVZ + 1, :] = 1.0 / z


def peer_route(q, sub_keys, tb=256):
    t = q.shape[0]
    tb = min(tb, t)
    score_sds = jax.ShapeDtypeStruct((PEER_HEADS, PEER_NKEYS, t), F32)
    score_spec = pl.BlockSpec((None, PEER_NKEYS, tb), lambda i, h: (h, 0, i))
    return pl.pallas_call(
        _peer_route_kernel,
        grid=(t // tb, PEER_HEADS),
        in_specs=[pl.BlockSpec((tb, 2 * PEER_HALF_QDIM), lambda i, h: (i, h)),
                  pl.BlockSpec((None, 2, PEER_NKEYS, PEER_HALF_QDIM), lambda i, h: (h, 0, 0, 0))],
        out_specs=[score_spec, score_spec, pl.BlockSpec((None, STAT_ROWS, tb), lambda i, h: (h, 0, i))],
        out_shape=[score_sds, score_sds, jax.ShapeDtypeStruct((PEER_HEADS, STAT_ROWS, t), F32)],
        scratch_shapes=[pltpu.VMEM((PEER_TOPK, tb), F32), pltpu.VMEM((PEER_TOPK, tb), F32),
                        pltpu.VMEM((PEER_CANDIDATES, tb), F32), pltpu.VMEM((PEER_TOPK, tb), F32)],
        compiler_params=_params(("parallel", "arbitrary")),
        name="peer_route",
    )(q, sub_keys)


def _gelu(a):
    return 0.5 * a * (1.0 + lax.erf(a * (2.0 ** -0.5)))


PEER_STAGES = 3
PEER_EXPERT_BLOCK = 512
SCORE_PIECES = 4


def _peer_dense_kernel(ht_ref, u_ref, vt_ref, s1_ref, s2_ref, stat_ref, out_ref, e2_scr,
                       act_a, act_b, coef_a, coef_b, *, te, n_blocks):
    e = pl.program_id(1)

    @pl.when(e == 0)
    def _():
        out_ref[...] = jnp.zeros_like(out_ref)
        for scr in (act_a, act_b, coef_a, coef_b):
            scr[...] = jnp.zeros_like(scr)
        for h in range(PEER_HEADS):
            e2_scr[h] = jnp.exp(s2_ref[h] - stat_ref[h, STAT_MAX2:STAT_MAX2 + 1, :])

    n_sub = te // PEER_NKEYS
    d_model, tb = ht_ref.shape
    gate_block = jnp.clip(e - 1, 0, n_blocks - 1)

    def stages(act_cur, act_prev, coef_cur, coef_prev):
        n_lane = tb // LANES
        n_slices = 2 * n_lane
        kc = d_model // n_slices
        def score_piece(p):
            ks = slice(p * (d_model // SCORE_PIECES), (p + 1) * (d_model // SCORE_PIECES))
            part = jnp.dot(u_ref[:, ks], ht_ref[ks, :], preferred_element_type=F32)
            if p == 0:
                act_cur[...] = part
            else:
                act_cur[...] += part

        score_piece(0)
        assert n_slices >= SCORE_PIECES
        piece_after_slice = {p * n_slices // SCORE_PIECES - 1: p for p in range(1, SCORE_PIECES)}
        s1_rows, e1_rows = [], []
        for sub in range(n_sub):
            i_idx = gate_block * n_sub + sub
            s1_rows.append([s1_ref[h, pl.ds(i_idx, 1), :] for h in range(PEER_HEADS)])
            e1_rows.append([jnp.exp(s1_rows[sub][h] - stat_ref[h, STAT_MAX1:STAT_MAX1 + 1, :])
                            * stat_ref[h, STAT_INVZ:STAT_INVZ + 1, :] for h in range(PEER_HEADS)])
        half = PEER_NKEYS // 2
        for r in range(n_slices):
            c, jh = divmod(r, 2)
            lanes = slice(c * LANES, (c + 1) * LANES)
            keys = slice(jh * half, (jh + 1) * half)
            gates = [None] * n_sub
            for h in range(PEER_HEADS):
                s2_tile = s2_ref[h, keys, lanes]
                e2_tile = e2_scr[h, keys, lanes]
                tau = stat_ref[h, STAT_TAU:STAT_TAU + 1, lanes]
                for sub in range(n_sub):
                    term = jnp.where(s2_tile + s1_rows[sub][h][:, lanes] >= tau,
                                     e2_tile * e1_rows[sub][h][:, lanes], 0.0)
                    gates[sub] = term if gates[sub] is None else gates[sub] + term
            for sub in range(n_sub):
                rows = slice(sub * PEER_NKEYS + jh * half, sub * PEER_NKEYS + (jh + 1) * half)
                coef_prev[rows, lanes] = (gates[sub] * _gelu(act_prev[rows, lanes])).astype(BF16)
            chunk = slice(r * kc, (r + 1) * kc)
            out_ref[chunk, :] += jnp.dot(vt_ref[chunk, :], coef_cur[...], preferred_element_type=F32)
            if r in piece_after_slice:
                score_piece(piece_after_slice[r])

    parity = lax.rem(e, 2)

    @pl.when(parity == 0)
    def _():
        stages(act_a, act_b, coef_a, coef_b)

    @pl.when(parity == 1)
    def _():
        stages(act_b, act_a, coef_b, coef_a)


def peer_dense(ht, u_tab, vt_blocks, s1, s2, stats, tb=512):
    d, t = ht.shape
    n_blocks, _, te = vt_blocks.shape
    tb = min(tb, t)
    assert t % tb == 0 and u_tab.shape[0] == n_blocks * te and te % PEER_NKEYS == 0
    once = pl.Buffered(1)
    tok_spec = pl.BlockSpec((PEER_HEADS, PEER_NKEYS, tb), lambda i, e: (0, 0, i), pipeline_mode=once)
    last = n_blocks - 1
    kern = functools.partial(_peer_dense_kernel, te=te, n_blocks=n_blocks)
    return pl.pallas_call(
        kern,
        grid=(t // tb, n_blocks + PEER_STAGES - 1),
        in_specs=[pl.BlockSpec((d, tb), lambda i, e: (0, i), pipeline_mode=once),
                  pl.BlockSpec((te, d), lambda i, e: (jnp.minimum(e, last), 0)),
                  pl.BlockSpec((None, d, te), lambda i, e: (jnp.clip(e - 2, 0, last), 0, 0)),
                  tok_spec, tok_spec,
                  pl.BlockSpec((PEER_HEADS, STAT_ROWS, tb), lambda i, e: (0, 0, i), pipeline_mode=once)],
        out_specs=pl.BlockSpec((d, tb), lambda i, e: (0, i)),
        out_shape=jax.ShapeDtypeStruct((d, t), F32),
        scratch_shapes=[pltpu.VMEM((PEER_HEADS, PEER_NKEYS, tb), F32),
                        pltpu.VMEM((te, tb), F32), pltpu.VMEM((te, tb), F32),
                        pltpu.VMEM((te, tb), BF16), pltpu.VMEM((te, tb), BF16)],
        compiler_params=_params(("parallel", "arbitrary")),
        name="peer_dense",
    )(ht, u_tab, vt_blocks, s1, s2, stats)


def kernel(x, attn_norm_g, ffn_norm_g, final_norm_g, w_qkv, w_gate, w_branch_sb, w_branch_dil,
           w_branch_diff, w_out, diff_lambda, diff_subln_g, peer_w_q, peer_sub_keys, peer_u, peer_v):
    b, s, d = x.shape
    t = b * s
    depth = w_qkv.shape[0]
    sb_w = SB_HEADS * HEAD_DIM
    dil_w = DIL_HEADS_PER_GROUP * len(DIL_GROUPS) * HEAD_DIM
    diff_w = DIFF_HEADS * 2 * DIFF_DIM
    dl_q, dl_k, dl_v = 3 * sb_w, 3 * sb_w + dil_w, 3 * sb_w + 2 * dil_w
    df_q = 3 * sb_w + 3 * dil_w
    df_k, df_v = df_q + diff_w, df_q + 2 * diff_w
    qkv_cols = df_v + diff_w
    rope_ops = _rope_operands(s)

    xt = x.reshape(t, d)
    for layer in range(depth):
        h = rmsnorm(xt, attn_norm_g[layer], BF16)
        proj = qkv_projection(h, w_qkv[layer].astype(BF16), rope_ops, s,
                              rope128_cols=(dl_q, dl_v), rope64_cols=(df_q, df_v))
        proj3 = proj.reshape(b, s, qkv_cols)
        o_sb = stick_breaking_attention(proj3, s, 0).reshape(t, sb_w)
        dil = []
        gw = DIL_HEADS_PER_GROUP * HEAD_DIM
        for g, (_, dilation) in enumerate(DIL_GROUPS):
            cols = (dl_q + g * gw, dl_k + g * gw, dl_v + g * gw)
            if dilation == 1:
                dil.append(dilated_group_attention(proj, proj, proj, b, s, g, dilation, *cols))
            else:
                qkv_views = dilated_regroup(proj, dilation, *cols)
                dil.append(dilated_group_attention(*qkv_views, b, s, g, dilation, 0, 0, 0))
        o_dl = dilated_merge([o for o, _ in dil], [l for _, l in dil], [r for _, r in DIL_GROUPS])
        lam_init = 0.8 - 0.6 * math.exp(-0.3 * layer)
        o_df = differential_attention(proj3, diff_lambda[layer], diff_subln_g[layer], s,
                                      df_q, df_k, df_v, lam_init).reshape(t, diff_w)
        merged = gate_merge(h, w_gate[layer].astype(BF16), (o_sb, o_dl, o_df),
                            (w_branch_sb[layer].astype(BF16), w_branch_dil[layer].astype(BF16),
                             w_branch_diff[layer].astype(BF16)))
        xt = matmul(merged, w_out[layer].astype(BF16), F32, residual=xt)

        h2, h2_t = rmsnorm(xt, ffn_norm_g[layer], BF16, with_transpose=True)
        q = matmul(h2, peer_w_q[layer].astype(BF16), BF16)
        s1, s2, stats = peer_route(q, peer_sub_keys[layer].astype(BF16))
        vt_blocks = peer_v[layer].reshape(-1, PEER_EXPERT_BLOCK, d).transpose(0, 2, 1).astype(BF16)
        out_t = peer_dense(h2_t, peer_u[layer].astype(BF16), vt_blocks, s1, s2, stats)
        xt = xt + out_t.T
    return rmsnorm(xt, final_norm_g, F32).reshape(b, s, d)
```

```python
import functools
import math

import jax
import jax.numpy as jnp
from jax import lax
from jax.experimental import pallas as pl
from jax.experimental.pallas import tpu as pltpu

F32 = jnp.float32
BF16 = jnp.bfloat16

HEAD_DIM = 128
ROPE_THETA = 10000.0
RMS_EPS = 1e-6
NEG_BIG = -1e30

SB_HEADS = 8
DIL_GROUPS = ((128, 1), (512, 4), (2048, 16))
DIL_HEADS_PER_GROUP = 4
DIFF_HEADS = 8
DIFF_DIM = 64
N_BRANCH = 3

PEER_HEADS = 8
PEER_NKEYS = 128
PEER_HALF_QDIM = 128
PEER_TOPK = 16
PEER_CANDIDATES = 16 + 7 * 8 + 8

LANES = 128
VMEM_LIMIT = 56 * 1024 * 1024


def _params(sem, vmem=VMEM_LIMIT):
    return pltpu.CompilerParams(dimension_semantics=sem, vmem_limit_bytes=vmem)


def _rmsnorm_kernel(x_ref, g_ref, o_ref):
    x = x_ref[...]
    ms = jnp.mean(x * x, axis=-1, keepdims=True)
    o_ref[...] = (x * lax.rsqrt(ms + RMS_EPS) * g_ref[...]).astype(o_ref.dtype)


def _rmsnorm_both_kernel(x_ref, g_ref, o_ref, ot_ref):
    x = x_ref[...]
    ms = jnp.mean(x * x, axis=-1, keepdims=True)
    y = x * lax.rsqrt(ms + RMS_EPS) * g_ref[...]
    o_ref[...] = y.astype(o_ref.dtype)
    ot_ref[...] = y.T.astype(ot_ref.dtype)


def rmsnorm(x, g, out_dtype, rows=256, with_transpose=False):
    t, d = x.shape
    rows = min(rows, t)
    row_spec = pl.BlockSpec((rows, d), lambda i: (i, 0))
    out_specs, out_shape = row_spec, jax.ShapeDtypeStruct((t, d), out_dtype)
    if with_transpose:
        out_specs = [row_spec, pl.BlockSpec((d, rows), lambda i: (0, i))]
        out_shape = [out_shape, jax.ShapeDtypeStruct((d, t), out_dtype)]
    return pl.pallas_call(
        _rmsnorm_both_kernel if with_transpose else _rmsnorm_kernel,
        grid=(t // rows,),
        in_specs=[row_spec, pl.BlockSpec((1, d), lambda i: (0, 0))],
        out_specs=out_specs,
        out_shape=out_shape,
        compiler_params=_params(("parallel",)),
        name="rmsnorm_transposed" if with_transpose else "rmsnorm",
    )(x, g.reshape(1, d))


def _rope_tables(seq_len, dim):
    inv_freq = 1.0 / (ROPE_THETA ** (jnp.arange(0, dim, 2, dtype=F32) / dim))
    ang = jnp.arange(seq_len, dtype=F32)[:, None] * inv_freq[None, :]
    ang = jnp.concatenate([ang, ang], axis=-1)
    return jnp.cos(ang), jnp.sin(ang)


def _rope_operands(seq_len):
    cos_h, sin_h = _rope_tables(seq_len, HEAD_DIM)
    lane = jnp.arange(LANES)
    sin_h_signed = jnp.where(lane < HEAD_DIM // 2, -sin_h, sin_h)
    cos_d, sin_d = _rope_tables(seq_len, DIFF_DIM)
    cos_d2 = jnp.concatenate([cos_d, cos_d], axis=-1)
    sin_d2 = jnp.concatenate([sin_d, sin_d], axis=-1)
    low = (lane % DIFF_DIM) < DIFF_DIM // 2
    sin_d_low = jnp.where(low, -sin_d2, 0.0)
    sin_d_high = jnp.where(low, 0.0, sin_d2)
    return cos_h, sin_h_signed, cos_d2, sin_d_low, sin_d_high


def _qkv_kernel(a_ref, w_ref, cos_h, sin_h, cos_d, sin_dl, sin_dh, o_ref, *, rope128_blocks, rope64_blocks):
    j = pl.program_id(1)
    acc = jnp.dot(a_ref[...], w_ref[...], preferred_element_type=F32)
    n_chunks = acc.shape[1] // LANES
    in128 = (j >= rope128_blocks[0]) & (j < rope128_blocks[1])
    in64 = (j >= rope64_blocks[0]) & (j < rope64_blocks[1])

    @pl.when(in128)
    def _():
        for c in range(n_chunks):
            x = acc[:, c * LANES:(c + 1) * LANES]
            y = x * cos_h[...] + pltpu.roll(x, HEAD_DIM // 2, 1) * sin_h[...]
            o_ref[:, c * LANES:(c + 1) * LANES] = y.astype(o_ref.dtype)

    @pl.when(in64)
    def _():
        for c in range(n_chunks):
            x = acc[:, c * LANES:(c + 1) * LANES]
            y = (x * cos_d[...] + pltpu.roll(x, LANES - DIFF_DIM // 2, 1) * sin_dl[...]
                 + pltpu.roll(x, DIFF_DIM // 2, 1) * sin_dh[...])
            o_ref[:, c * LANES:(c + 1) * LANES] = y.astype(o_ref.dtype)

    @pl.when(jnp.logical_not(in128 | in64))
    def _():
        o_ref[...] = acc.astype(o_ref.dtype)


def qkv_projection(h, w, layer, rope_ops, seq_len, rope128_cols, rope64_cols, tm=1024, tn=512):
    m, k = h.shape
    n = w.shape[2]
    tm = min(tm, seq_len)
    assert seq_len % tm == 0 and m % tm == 0 and n % tn == 0
    for lo, hi in (rope128_cols, rope64_cols):
        assert lo % tn == 0 and hi % tn == 0
    seq_blocks = seq_len // tm
    tab_spec = pl.BlockSpec((tm, LANES), lambda i, j: (i % seq_blocks, 0))
    kern = functools.partial(
        _qkv_kernel,
        rope128_blocks=(rope128_cols[0] // tn, rope128_cols[1] // tn),
        rope64_blocks=(rope64_cols[0] // tn, rope64_cols[1] // tn))
    return pl.pallas_call(
        kern,
        grid=(m // tm, n // tn),
        in_specs=[pl.BlockSpec((tm, k), lambda i, j: (i, 0)),
                  pl.BlockSpec((None, k, tn), lambda i, j: (layer, 0, j)),
                  tab_spec, tab_spec, tab_spec, tab_spec, tab_spec],
        out_specs=pl.BlockSpec((tm, tn), lambda i, j: (i, j)),
        out_shape=jax.ShapeDtypeStruct((m, n), BF16),
        compiler_params=_params(("parallel", "arbitrary")),
        name="qkv_projection",
    )(h, w, *rope_ops)


def _matmul_kernel(a_ref, w_ref, o_ref):
    o_ref[...] = jnp.dot(a_ref[...], w_ref[...], preferred_element_type=F32).astype(o_ref.dtype)


def _matmul_residual_kernel(a_ref, w_ref, r_ref, o_ref):
    o_ref[...] = r_ref[...] + jnp.dot(a_ref[...], w_ref[...], preferred_element_type=F32)


def matmul(a, w, layer, out_dtype, residual=None, tm=1024, tn=512):
    m, k = a.shape
    n = w.shape[2]
    tm, tn = min(tm, m), min(tn, n)
    assert m % tm == 0 and n % tn == 0
    in_specs = [pl.BlockSpec((tm, k), lambda i, j: (i, 0)),
                pl.BlockSpec((None, k, tn), lambda i, j: (layer, 0, j))]
    args = [a, w]
    kern = _matmul_kernel
    if residual is not None:
        in_specs.append(pl.BlockSpec((tm, tn), lambda i, j: (i, j)))
        args.append(residual)
        kern = _matmul_residual_kernel
    return pl.pallas_call(
        kern,
        grid=(m // tm, n // tn),
        in_specs=in_specs,
        out_specs=pl.BlockSpec((tm, tn), lambda i, j: (i, j)),
        out_shape=jax.ShapeDtypeStruct((m, n), out_dtype),
        compiler_params=_params(("parallel", "arbitrary")),
        name="matmul_residual" if residual is not None else "matmul",
    )(*args)


LOG2_E = math.log2(math.e)
SIGN_BIT = 0x80000000


def _softplus2(u):
    neg_abs = lax.bitcast_convert_type(lax.bitcast_convert_type(u, jnp.uint32) | jnp.uint32(SIGN_BIT), F32)
    return jnp.maximum(u, 0.0) + jnp.log2(1.0 + jnp.exp2(neg_abs))


def _sb_kernel(suffix_ref, q_ref, k_ref, v_ref, o_ref, *, tq, heads, scale2):
    i = pl.program_id(2)
    tk = 2 * tq
    suffix = suffix_ref[...]
    row = lax.broadcasted_iota(jnp.int32, (tq, tk), 0)
    col = lax.broadcasted_iota(jnp.int32, (tq, tk), 1)
    kd = lax.shift_right_logical(i, 1)
    before = col < row + (i - 2 * kd) * tq
    dims = (((1,), (1,)), ((), ()))
    qs = [q_ref[:, h * HEAD_DIM:(h + 1) * HEAD_DIM] for h in range(heads)]

    def block(kb, state, diagonal):
        start = pl.multiple_of(kb * tk, tk)
        out = []
        for h in range(heads):
            carry, acc = state[h]
            sl = slice(h * HEAD_DIM, (h + 1) * HEAD_DIM)
            k = k_ref[pl.ds(start, tk), sl]
            v = v_ref[pl.ds(start, tk), sl]
            z = lax.dot_general(qs[h], k, dims, preferred_element_type=F32) * scale2
            sp = _softplus2(z)
            spm = jnp.where(before, sp, 0.0) if diagonal else sp
            spb = spm.astype(BF16)
            sum_lo = jnp.sum(spm[:, :tq], axis=1, keepdims=True)
            sum_hi = jnp.sum(spm[:, tq:], axis=1, keepdims=True)
            later_lo = jnp.dot(spb[:, :tq], suffix, preferred_element_type=F32) + (carry + sum_hi)
            later_hi = jnp.dot(spb[:, tq:], suffix, preferred_element_type=F32) + carry
            w = jnp.exp2(z - sp - jnp.concatenate([later_lo, later_hi], axis=1))
            if diagonal:
                w = jnp.where(before, w, 0.0)
            acc = acc + jnp.dot(w.astype(BF16), v, preferred_element_type=F32)
            out.append((carry + (sum_lo + sum_hi), acc))
        return tuple(out)

    state = tuple((jnp.zeros((tq, 1), F32), jnp.zeros((tq, HEAD_DIM), F32)) for _ in range(heads))
    state = block(kd, state, True)
    state = lax.fori_loop(0, kd, lambda n, st: block(kd - 1 - n, st, False), state)
    for h in range(heads):
        o_ref[:, h * HEAD_DIM:(h + 1) * HEAD_DIM] = state[h][1].astype(o_ref.dtype)


def stick_breaking_attention(proj, seq_len, col0, tq=512, heads=4):
    b = proj.shape[0]
    tq = min(tq, seq_len // 2)
    hw = heads * HEAD_DIM
    assert seq_len % (2 * tq) == 0 and SB_HEADS % heads == 0 and col0 % hw == 0
    c0 = col0 // hw
    per = SB_HEADS // heads
    idx = jnp.arange(tq)
    suffix = (idx[:, None] > idx[None, :]).astype(BF16)
    kern = functools.partial(_sb_kernel, tq=tq, heads=heads, scale2=HEAD_DIM ** -0.5 * LOG2_E)
    return pl.pallas_call(
        kern,
        grid=(b, per, seq_len // tq),
        in_specs=[pl.BlockSpec((tq, tq), lambda bi, h, i: (0, 0)),
                  pl.BlockSpec((None, tq, hw), lambda bi, h, i: (bi, i, c0 + h)),
                  pl.BlockSpec((None, seq_len, hw), lambda bi, h, i: (bi, 0, c0 + per + h)),
                  pl.BlockSpec((None, seq_len, hw), lambda bi, h, i: (bi, 0, c0 + 2 * per + h))],
        out_specs=pl.BlockSpec((None, tq, hw), lambda bi, h, i: (bi, i, h)),
        out_shape=jax.ShapeDtypeStruct((b, seq_len, SB_HEADS * HEAD_DIM), BF16),
        compiler_params=_params(("parallel", "parallel", "arbitrary")),
        name="stick_breaking_attention",
    )(suffix, proj, proj, proj)


def _dil_kernel(q_ref, kp_ref, kc_ref, vp_ref, vc_ref, o_ref, lse_ref, *, tq, scale):
    i = pl.program_id(2)
    row = lax.broadcasted_iota(jnp.int32, (tq, tq), 0)
    col = lax.broadcasted_iota(jnp.int32, (tq, tq), 1)
    cur_ok = col <= row
    prev_ok = col >= row + jnp.where(i > 0, 0, tq)
    dims = (((1,), (1,)), ((), ()))
    for h in range(DIL_HEADS_PER_GROUP):
        sl = slice(h * HEAD_DIM, (h + 1) * HEAD_DIM)
        q = q_ref[:, sl]
        s_cur = lax.dot_general(q, kc_ref[:, sl], dims, preferred_element_type=F32) * scale
        s_prev = lax.dot_general(q, kp_ref[:, sl], dims, preferred_element_type=F32) * scale
        s_cur = jnp.where(cur_ok, s_cur, NEG_BIG)
        s_prev = jnp.where(prev_ok, s_prev, NEG_BIG)
        m = jnp.maximum(jnp.max(s_cur, axis=1, keepdims=True), jnp.max(s_prev, axis=1, keepdims=True))
        p_cur = jnp.exp(s_cur - m)
        p_prev = jnp.exp(s_prev - m)
        l = jnp.sum(p_cur, axis=1, keepdims=True) + jnp.sum(p_prev, axis=1, keepdims=True)
        o = (jnp.dot(p_cur.astype(BF16), vc_ref[:, sl], preferred_element_type=F32)
             + jnp.dot(p_prev.astype(BF16), vp_ref[:, sl], preferred_element_type=F32))
        o_ref[:, sl] = o / l
        lse_ref[:, sl] = jnp.broadcast_to(m + jnp.log(l), (tq, HEAD_DIM))


def _regroup_kernel(q_ref, k_ref, v_ref, qo_ref, ko_ref, vo_ref, scr, *, dilation):
    rows, gw = q_ref.shape
    n = rows // dilation
    for src, dst in ((q_ref, qo_ref), (k_ref, ko_ref), (v_ref, vo_ref)):
        for j in range(gw // LANES):
            scr[j] = src[:, j * LANES:(j + 1) * LANES].astype(F32)
        for c in range(dilation):
            for j in range(gw // LANES):
                dst[:, c * gw + j * LANES:c * gw + (j + 1) * LANES] = (
                    scr[j, pl.ds(c, n, stride=dilation), :].astype(dst.dtype))


def dilated_regroup(proj, dilation, q_col, k_col, v_col, rows=512):
    t, cols = proj.shape
    gw = DIL_HEADS_PER_GROUP * HEAD_DIM
    rows = min(rows, t)
    assert t % rows == 0 and rows % (16 * dilation) == 0
    in_specs = [pl.BlockSpec((rows, gw), functools.partial(lambda i, blk: (i, blk), blk=col // gw))
                for col in (q_col, k_col, v_col)]
    out_spec = pl.BlockSpec((rows // dilation, dilation * gw), lambda i: (i, 0))
    out_sds = jax.ShapeDtypeStruct((t // dilation, dilation * gw), proj.dtype)
    return pl.pallas_call(
        functools.partial(_regroup_kernel, dilation=dilation),
        grid=(t // rows,),
        in_specs=in_specs,
        out_specs=[out_spec] * 3,
        out_shape=[out_sds] * 3,
        scratch_shapes=[pltpu.VMEM((gw // LANES, rows, LANES), F32)],
        compiler_params=_params(("parallel",)),
        name=f"dilated_regroup_r{dilation}",
    )(proj, proj, proj)


def dilated_group_attention(q_arr, k_arr, v_arr, batch, seq_len, group, dilation, q_col, k_col, v_col):
    gw = DIL_HEADS_PER_GROUP * HEAD_DIM
    tq = DIL_GROUPS[group][0] // dilation
    sub_len = seq_len // dilation
    assert sub_len % tq == 0
    views, blocks, per_row = [], [], []
    for arr, col in ((q_arr, q_col), (k_arr, k_col), (v_arr, v_col)):
        width = arr.shape[1] // dilation
        assert width % gw == 0 and col % gw == 0
        views.append(arr.reshape(batch, sub_len, dilation * width))
        blocks.append(col // gw)
        per_row.append(width // gw)
    prev = lambda i: jnp.maximum(i - 1, 0)

    def spec(which, row_of):
        return pl.BlockSpec((None, tq, gw),
                            lambda bi, c, i: (bi, row_of(i), c * per_row[which] + blocks[which]))

    out_sds = jax.ShapeDtypeStruct((batch, sub_len, dilation * gw), F32)
    out_spec = pl.BlockSpec((None, tq, gw), lambda bi, c, i: (bi, i, c))
    kern = functools.partial(_dil_kernel, tq=tq, scale=HEAD_DIM ** -0.5)
    o, lse = pl.pallas_call(
        kern,
        grid=(batch, dilation, sub_len // tq),
        in_specs=[spec(0, lambda i: i), spec(1, prev), spec(1, lambda i: i), spec(2, prev), spec(2, lambda i: i)],
        out_specs=[out_spec, out_spec],
        out_shape=[out_sds, out_sds],
        compiler_params=_params(("parallel", "parallel", "arbitrary")),
        name=f"dilated_attention_g{group}",
    )(views[0], views[1], views[1], views[2], views[2])
    return o.reshape(batch * sub_len, dilation * gw), lse.reshape(batch * sub_len, dilation * gw)


def _dil_merge_kernel(*refs, dilations):
    n = len(dilations)
    o_refs, l_refs, out_ref, scratch = refs[:n], refs[n:2 * n], refs[2 * n], refs[2 * n + 1:]
    rows, gw = out_ref.shape
    outs, lses = [], []
    for g, dilation in enumerate(dilations):
        vals = []
        for src, scr in ((o_refs[g], scratch[2 * g]), (l_refs[g], scratch[2 * g + 1])):
            if dilation == 1:
                vals.append(src[...])
            else:
                for c in range(dilation):
                    for j in range(gw // LANES):
                        scr[j, pl.ds(c, rows // dilation, stride=dilation), :] = (
                            src[:, c * gw + j * LANES:c * gw + (j + 1) * LANES])
                vals.append(jnp.concatenate([scr[j] for j in range(gw // LANES)], axis=1))
        outs.append(vals[0])
        lses.append(vals[1])
    m = functools.reduce(jnp.maximum, lses)
    es = [jnp.exp(l - m) for l in lses]
    num = functools.reduce(lambda a, b: a + b, [e * o for e, o in zip(es, outs)])
    den = functools.reduce(lambda a, b: a + b, es)
    out_ref[...] = (num / den).astype(out_ref.dtype)


def dilated_merge(outs, lses, dilations, rows=512):
    gw = DIL_HEADS_PER_GROUP * HEAD_DIM
    t = outs[0].shape[0] * dilations[0]
    rows = min(rows, t)
    specs = [pl.BlockSpec((rows // r, r * gw), lambda i: (i, 0)) for r in dilations]
    return pl.pallas_call(
        functools.partial(_dil_merge_kernel, dilations=tuple(dilations)),
        grid=(t // rows,),
        in_specs=specs + specs,
        out_specs=pl.BlockSpec((rows, gw), lambda i: (i, 0)),
        out_shape=jax.ShapeDtypeStruct((t, gw), BF16),
        scratch_shapes=[pltpu.VMEM((gw // LANES, rows, LANES), F32) for _ in range(2 * len(dilations))],
        compiler_params=_params(("parallel",)),
        name="dilated_merge",
    )(*outs, *lses)


def _diff_kernel(lam_ref, g_ref, q_ref, k_ref, v_ref, o_ref, *, tq, heads, key_blocks, scale2, lam_init):
    i = pl.program_id(2)
    tk = key_blocks * tq
    lp = lam_ref[...]
    lam = (jnp.exp(jnp.sum(lp[0:1] * lp[1:2], axis=1, keepdims=True))
           - jnp.exp(jnp.sum(lp[2:3] * lp[3:4], axis=1, keepdims=True)) + lam_init)
    lane = lax.broadcasted_iota(jnp.int32, (tq, LANES), 1)
    qqs = []
    for h in range(heads):
        q = q_ref[:, h * LANES:(h + 1) * LANES].astype(F32)
        qqs.append(jnp.concatenate([jnp.where(lane < DIFF_DIM, q, 0.0), jnp.where(lane >= DIFF_DIM, q, 0.0)],
                                   axis=0).astype(BF16))
    row = lax.broadcasted_iota(jnp.int32, (2 * tq, tk), 0)
    row = jnp.where(row >= tq, row - tq, row)
    col = lax.broadcasted_iota(jnp.int32, (2 * tq, tk), 1)
    kd = i // key_blocks
    causal = col <= row + (i - key_blocks * kd) * tq
    dims = (((1,), (1,)), ((), ()))

    def block(kb, state, diagonal):
        start = pl.multiple_of(kb * tk, tk)
        out = []
        for h in range(heads):
            m, l, acc = state[h]
            sl = slice(h * LANES, (h + 1) * LANES)
            k = k_ref[pl.ds(start, tk), sl]
            v = v_ref[pl.ds(start, tk), sl]
            s = lax.dot_general(qqs[h], k, dims, preferred_element_type=F32) * scale2
            if diagonal:
                s = jnp.where(causal, s, NEG_BIG)
            m_new = jnp.maximum(m, jnp.max(s, axis=1, keepdims=True))
            alpha = jnp.exp2(m - m_new)
            p = jnp.exp2(s - m_new)
            l = alpha * l + jnp.sum(p, axis=1, keepdims=True)
            acc = alpha * acc + jnp.dot(p.astype(BF16), v, preferred_element_type=F32)
            out.append((m_new, l, acc))
        return tuple(out)

    state = tuple((jnp.full((2 * tq, 1), NEG_BIG, F32), jnp.zeros((2 * tq, 1), F32),
                   jnp.zeros((2 * tq, LANES), F32)) for _ in range(heads))
    state = block(kd, state, True)
    state = lax.fori_loop(0, kd, lambda n, st: block(n, st, False), state)
    for h in range(heads):
        _, l, acc = state[h]
        o_all = acc / l
        o = o_all[:tq] - lam * o_all[tq:]
        ms = jnp.mean(o * o, axis=-1, keepdims=True)
        y = o * lax.rsqrt(ms + RMS_EPS) * g_ref[...]
        o_ref[:, h * LANES:(h + 1) * LANES] = (y * (1.0 - lam_init)).astype(o_ref.dtype)


def differential_attention(proj, lam_params, subln_g, seq_len, q_col, k_col, v_col, lam_init, tq=1024, heads=2,
                           key_blocks=1):
    b = proj.shape[0]
    tq = min(tq, seq_len // key_blocks)
    hw = heads * LANES
    assert seq_len % (key_blocks * tq) == 0 and DIFF_HEADS % heads == 0
    assert q_col % hw == 0 and k_col % hw == 0 and v_col % hw == 0
    qb, kb, vb = q_col // hw, k_col // hw, v_col // hw
    kern = functools.partial(_diff_kernel, tq=tq, heads=heads, key_blocks=key_blocks,
                             scale2=DIFF_DIM ** -0.5 * LOG2_E, lam_init=lam_init)
    return pl.pallas_call(
        kern,
        grid=(b, DIFF_HEADS // heads, seq_len // tq),
        in_specs=[pl.BlockSpec((4, DIFF_DIM), lambda bi, h, i: (0, 0)),
                  pl.BlockSpec((1, 2 * DIFF_DIM), lambda bi, h, i: (0, 0)),
                  pl.BlockSpec((None, tq, hw), lambda bi, h, i: (bi, i, qb + h)),
                  pl.BlockSpec((None, seq_len, hw), lambda bi, h, i: (bi, 0, kb + h)),
                  pl.BlockSpec((None, seq_len, hw), lambda bi, h, i: (bi, 0, vb + h))],
        out_specs=pl.BlockSpec((None, tq, hw), lambda bi, h, i: (bi, i, h)),
        out_shape=jax.ShapeDtypeStruct((b, seq_len, DIFF_HEADS * 2 * DIFF_DIM), BF16),
        compiler_params=_params(("parallel", "parallel", "arbitrary")),
        name="differential_attention",
    )(lam_params, subln_g.reshape(1, -1), proj, proj, proj)


def _gate_merge_kernel(h_ref, wg0, wg1, wg2, o0, o1, o2, wb0, wb1, wb2, out_ref):
    h = h_ref[...]
    acc = None
    for wg, o, wb in ((wg0, o0, wb0), (wg1, o1, wb1), (wg2, o2, wb2)):
        gate = 1.0 / (1.0 + jnp.exp(-jnp.dot(h, wg[...], preferred_element_type=F32)))
        term = gate * jnp.dot(o[...], wb[...], preferred_element_type=F32)
        acc = term if acc is None else acc + term
    out_ref[...] = acc.astype(out_ref.dtype)


def gate_merge(h, w_gate, branch_outs, branch_ws, layer, tm=1024, tn=256):
    t, d = h.shape
    tm, tn = min(tm, t), min(tn, d)
    nj = d // tn
    in_specs = [pl.BlockSpec((tm, d), lambda i, j: (i, 0))]
    in_specs += [pl.BlockSpec((None, d, tn), functools.partial(lambda i, j, b: (layer, 0, b * nj + j), b=b))
                 for b in range(N_BRANCH)]
    in_specs += [pl.BlockSpec((tm, o.shape[1]), lambda i, j: (i, 0)) for o in branch_outs]
    in_specs += [pl.BlockSpec((None, w.shape[1], tn), lambda i, j: (layer, 0, j)) for w in branch_ws]
    return pl.pallas_call(
        _gate_merge_kernel,
        grid=(t // tm, nj),
        in_specs=in_specs,
        out_specs=pl.BlockSpec((tm, tn), lambda i, j: (i, j)),
        out_shape=jax.ShapeDtypeStruct((t, d), BF16),
        compiler_params=_params(("parallel", "arbitrary")),
        name="gate_merge",
    )(h, w_gate, w_gate, w_gate, *branch_outs, *branch_ws)


STAT_TAU, STAT_MAX1, STAT_MAX2, STAT_INVZ = 0, 1, 2, 3
STAT_ROWS = 8


def _top_values(x, scr, count):
    for kk in range(count):
        m = jnp.max(x, axis=0, keepdims=True)
        scr[kk:kk + 1, :] = m
        x = jnp.where(x == m, -jnp.inf, x)


def _peer_route_kernel(q_ref, keys_ref, s1_ref, s2_ref, stat_ref, a_scr, b_scr, c_scr, t_scr):
    dims = (((1,), (1,)), ((), ()))
    s1 = lax.dot_general(keys_ref[0], q_ref[:, :PEER_HALF_QDIM], dims, preferred_element_type=F32)
    s2 = lax.dot_general(keys_ref[1], q_ref[:, PEER_HALF_QDIM:], dims, preferred_element_type=F32)
    s1_ref[...] = s1
    s2_ref[...] = s2
    _top_values(s1, a_scr, PEER_TOPK)
    _top_values(s2, b_scr, PEER_TOPK)
    half = PEER_TOPK // 2
    c_scr[0:PEER_TOPK, :] = a_scr[0:1, :] + b_scr[...]
    for ii in range(1, half):
        c_scr[PEER_TOPK + (ii - 1) * half:PEER_TOPK + ii * half, :] = a_scr[ii:ii + 1, :] + b_scr[0:half, :]
    c_scr[PEER_TOPK + (half - 1) * half:PEER_CANDIDATES, :] = a_scr[half:PEER_TOPK, :] + b_scr[0:1, :]
    cand = c_scr[...]
    _top_values(cand, t_scr, PEER_TOPK)
    tau = t_scr[PEER_TOPK - 1:PEER_TOPK, :]
    best = t_scr[0:1, :]
    z = jnp.sum(jnp.where(cand >= tau, jnp.exp(cand - best), 0.0), axis=0, keepdims=True)
    stat_ref[...] = jnp.zeros_like(stat_ref)
    stat_ref[STAT_TAU:STAT_TAU + 1, :] = tau
    stat_ref[STAT_MAX1:STAT_MAX1 + 1, :] = a_scr[0:1, :]
    stat_ref[STAT_MAX2:STAT_MAX2 + 1, :] = b_scr[0:1, :]
    stat_ref[STAT_INVZ:STAT_INVZ + 1, :] = 1.0 / z


def peer_route(q, sub_keys, layer, tb=256):
    t = q.shape[0]
    tb = min(tb, t)
    score_sds = jax.ShapeDtypeStruct((PEER_HEADS, PEER_NKEYS, t), F32)
    score_spec = pl.BlockSpec((None, PEER_NKEYS, tb), lambda i, h: (h, 0, i))
    return pl.pallas_call(
        _peer_route_kernel,
        grid=(t // tb, PEER_HEADS),
        in_specs=[pl.BlockSpec((tb, 2 * PEER_HALF_QDIM), lambda i, h: (i, h)),
                  pl.BlockSpec((None, None, 2, PEER_NKEYS, PEER_HALF_QDIM), lambda i, h: (layer, h, 0, 0, 0))],
        out_specs=[score_spec, score_spec, pl.BlockSpec((None, STAT_ROWS, tb), lambda i, h: (h, 0, i))],
        out_shape=[score_sds, score_sds, jax.ShapeDtypeStruct((PEER_HEADS, STAT_ROWS, t), F32)],
        scratch_shapes=[pltpu.VMEM((PEER_TOPK, tb), F32), pltpu.VMEM((PEER_TOPK, tb), F32),
                        pltpu.VMEM((PEER_CANDIDATES, tb), F32), pltpu.VMEM((PEER_TOPK, tb), F32)],
        compiler_params=_params(("parallel", "arbitrary")),
        name="peer_route",
    )(q, sub_keys)


def _gelu(a):
    return 0.5 * a * (1.0 + lax.erf(a * (2.0 ** -0.5)))


PEER_STAGES = 3
PEER_EXPERT_BLOCK = 512
SCORE_PIECES = 4


def _peer_dense_kernel(ht_ref, u_ref, vt_ref, s1_ref, s2_ref, stat_ref, out_ref, e2_scr,
                       act_a, act_b, coef_a, coef_b, *, te, n_blocks):
    e = pl.program_id(1)

    @pl.when(e == 0)
    def _():
        out_ref[...] = jnp.zeros_like(out_ref)
        for scr in (act_a, act_b, coef_a, coef_b):
            scr[...] = jnp.zeros_like(scr)
        for h in range(PEER_HEADS):
            e2_scr[h] = jnp.exp(s2_ref[h] - stat_ref[h, STAT_MAX2:STAT_MAX2 + 1, :])

    n_sub = te // PEER_NKEYS
    d_model, tb = ht_ref.shape
    gate_block = jnp.clip(e - 1, 0, n_blocks - 1)

    def stages(act_cur, act_prev, coef_cur, coef_prev):
        n_lane = tb // LANES
        n_slices = 2 * n_lane
        kc = d_model // n_slices
        def score_piece(p):
            ks = slice(p * (d_model // SCORE_PIECES), (p + 1) * (d_model // SCORE_PIECES))
            part = jnp.dot(u_ref[:, ks], ht_ref[ks, :], preferred_element_type=F32)
            if p == 0:
                act_cur[...] = part
            else:
                act_cur[...] += part

        score_piece(0)
        assert n_slices >= SCORE_PIECES
        piece_after_slice = {p * n_slices // SCORE_PIECES - 1: p for p in range(1, SCORE_PIECES)}
        s1_rows, e1_rows = [], []
        for sub in range(n_sub):
            i_idx = gate_block * n_sub + sub
            s1_rows.append([s1_ref[h, pl.ds(i_idx, 1), :] for h in range(PEER_HEADS)])
            e1_rows.append([jnp.exp(s1_rows[sub][h] - stat_ref[h, STAT_MAX1:STAT_MAX1 + 1, :])
                            * stat_ref[h, STAT_INVZ:STAT_INVZ + 1, :] for h in range(PEER_HEADS)])
        half = PEER_NKEYS // 2
        for r in range(n_slices):
            c, jh = divmod(r, 2)
            lanes = slice(c * LANES, (c + 1) * LANES)
            keys = slice(jh * half, (jh + 1) * half)
            gates = [None] * n_sub
            for h in range(PEER_HEADS):
                s2_tile = s2_ref[h, keys, lanes]
                e2_tile = e2_scr[h, keys, lanes]
                tau = stat_ref[h, STAT_TAU:STAT_TAU + 1, lanes]
                for sub in range(n_sub):
                    term = jnp.where(s2_tile + s1_rows[sub][h][:, lanes] >= tau,
                                     e2_tile * e1_rows[sub][h][:, lanes], 0.0)
                    gates[sub] = term if gates[sub] is None else gates[sub] + term
            for sub in range(n_sub):
                rows = slice(sub * PEER_NKEYS + jh * half, sub * PEER_NKEYS + (jh + 1) * half)
                coef_prev[rows, lanes] = (gates[sub] * _gelu(act_prev[rows, lanes])).astype(BF16)
            chunk = slice(r * kc, (r + 1) * kc)
            out_ref[chunk, :] += jnp.dot(vt_ref[chunk, :], coef_cur[...], preferred_element_type=F32)
            if r in piece_after_slice:
                score_piece(piece_after_slice[r])

    parity = lax.rem(e, 2)

    @pl.when(parity == 0)
    def _():
        stages(act_a, act_b, coef_a, coef_b)

    @pl.when(parity == 1)
    def _():
        stages(act_b, act_a, coef_b, coef_a)


def peer_dense(ht, u_tab, vt_blocks, layer, s1, s2, stats, tb=512):
    d, t = ht.shape
    _, n_blocks, _, te = vt_blocks.shape
    tb = min(tb, t)
    assert t % tb == 0 and u_tab.shape[1] == n_blocks * te and te % PEER_NKEYS == 0
    once = pl.Buffered(1)
    tok_spec = pl.BlockSpec((PEER_HEADS, PEER_NKEYS, tb), lambda i, e: (0, 0, i), pipeline_mode=once)
    last = n_blocks - 1
    kern = functools.partial(_peer_dense_kernel, te=te, n_blocks=n_blocks)
    return pl.pallas_call(
        kern,
        grid=(t // tb, n_blocks + PEER_STAGES - 1),
        in_specs=[pl.BlockSpec((d, tb), lambda i, e: (0, i), pipeline_mode=once),
                  pl.BlockSpec((None, te, d), lambda i, e: (layer, jnp.minimum(e, last), 0)),
                  pl.BlockSpec((None, None, d, te), lambda i, e: (layer, jnp.clip(e - 2, 0, last), 0, 0)),
                  tok_spec, tok_spec,
                  pl.BlockSpec((PEER_HEADS, STAT_ROWS, tb), lambda i, e: (0, 0, i), pipeline_mode=once)],
        out_specs=pl.BlockSpec((d, tb), lambda i, e: (0, i)),
        out_shape=jax.ShapeDtypeStruct((d, t), F32),
        scratch_shapes=[pltpu.VMEM((PEER_HEADS, PEER_NKEYS, tb), F32),
                        pltpu.VMEM((te, tb), F32), pltpu.VMEM((te, tb), F32),
                        pltpu.VMEM((te, tb), BF16), pltpu.VMEM((te, tb), BF16)],
        compiler_params=_params(("parallel", "arbitrary")),
        name="peer_dense",
    )(ht, u_tab, vt_blocks, s1, s2, stats)


def kernel(x, attn_norm_g, ffn_norm_g, final_norm_g, w_qkv, w_gate, w_branch_sb, w_branch_dil,
           w_branch_diff, w_out, diff_lambda, diff_subln_g, peer_w_q, peer_sub_keys, peer_u, peer_v):
    b, s, d = x.shape
    t = b * s
    depth = w_qkv.shape[0]
    sb_w = SB_HEADS * HEAD_DIM
    dil_w = DIL_HEADS_PER_GROUP * len(DIL_GROUPS) * HEAD_DIM
    diff_w = DIFF_HEADS * 2 * DIFF_DIM
    dl_q, dl_k, dl_v = 3 * sb_w, 3 * sb_w + dil_w, 3 * sb_w + 2 * dil_w
    df_q = 3 * sb_w + 3 * dil_w
    df_k, df_v = df_q + diff_w, df_q + 2 * diff_w
    qkv_cols = df_v + diff_w
    rope_ops = _rope_operands(s)
    w_qkv_b, w_gate_b, w_out_b, w_pq_b = (w.astype(BF16) for w in (w_qkv, w_gate, w_out, peer_w_q))
    w_branch_b = tuple(w.astype(BF16) for w in (w_branch_sb, w_branch_dil, w_branch_diff))
    keys_b, u_b = peer_sub_keys.astype(BF16), peer_u.astype(BF16)
    vt_blocks = peer_v.reshape(depth, -1, PEER_EXPERT_BLOCK, d).transpose(0, 1, 3, 2).astype(BF16)

    xt = x.reshape(t, d)
    for layer in range(depth):
        h = rmsnorm(xt, attn_norm_g[layer], BF16)
        proj = qkv_projection(h, w_qkv_b, layer, rope_ops, s,
                              rope128_cols=(dl_q, dl_v), rope64_cols=(df_q, df_v))
        proj3 = proj.reshape(b, s, qkv_cols)
        o_sb = stick_breaking_attention(proj3, s, 0).reshape(t, sb_w)
        dil = []
        gw = DIL_HEADS_PER_GROUP * HEAD_DIM
        for g, (_, dilation) in enumerate(DIL_GROUPS):
            cols = (dl_q + g * gw, dl_k + g * gw, dl_v + g * gw)
            if dilation == 1:
                dil.append(dilated_group_attention(proj, proj, proj, b, s, g, dilation, *cols))
            else:
                qkv_views = dilated_regroup(proj, dilation, *cols)
                dil.append(dilated_group_attention(*qkv_views, b, s, g, dilation, 0, 0, 0))
        o_dl = dilated_merge([o for o, _ in dil], [l for _, l in dil], [r for _, r in DIL_GROUPS])
        lam_init = 0.8 - 0.6 * math.exp(-0.3 * layer)
        o_df = differential_attention(proj3, diff_lambda[layer], diff_subln_g[layer], s,
                                      df_q, df_k, df_v, lam_init).reshape(t, diff_w)
        merged = gate_merge(h, w_gate_b, (o_sb, o_dl, o_df), w_branch_b, layer)
        xt = matmul(merged, w_out_b, layer, F32, residual=xt)

        h2, h2_t = rmsnorm(xt, ffn_norm_g[layer], BF16, with_transpose=True)
        q = matmul(h2, w_pq_b, layer, BF16)
        s1, s2, stats = peer_route(q, keys_b, layer)
        out_t = peer_dense(h2_t, u_b, vt_blocks, layer, s1, s2, stats)
        xt = xt + out_t.T
    return rmsnorm(xt, final_norm_g, F32).reshape(b, s, d)
```

```python
import functools
import math

import jax
import jax.numpy as jnp
from jax import lax
from jax.experimental import pallas as pl
from jax.experimental.pallas import tpu as pltpu

F32 = jnp.float32
BF16 = jnp.bfloat16

HEAD_DIM = 128
ROPE_THETA = 10000.0
RMS_EPS = 1e-6
NEG_BIG = -1e30

SB_HEADS = 8
DIL_GROUPS = ((128, 1), (512, 4), (2048, 16))
DIL_HEADS_PER_GROUP = 4
DIFF_HEADS = 8
DIFF_DIM = 64
N_BRANCH = 3

PEER_HEADS = 8
PEER_NKEYS = 128
PEER_HALF_QDIM = 128
PEER_TOPK = 16
PEER_CANDIDATES = 16 + 7 * 8 + 8

LANES = 128
VMEM_LIMIT = 56 * 1024 * 1024


def _params(sem, vmem=VMEM_LIMIT):
    return pltpu.CompilerParams(dimension_semantics=sem, vmem_limit_bytes=vmem)


def _rmsnorm_kernel(x_ref, g_ref, o_ref):
    x = x_ref[...]
    ms = jnp.mean(x * x, axis=-1, keepdims=True)
    o_ref[...] = (x * lax.rsqrt(ms + RMS_EPS) * g_ref[...]).astype(o_ref.dtype)


def _rmsnorm_both_kernel(x_ref, g_ref, o_ref, ot_ref):
    x = x_ref[...]
    ms = jnp.mean(x * x, axis=-1, keepdims=True)
    y = x * lax.rsqrt(ms + RMS_EPS) * g_ref[...]
    o_ref[...] = y.astype(o_ref.dtype)
    ot_ref[...] = y.T.astype(ot_ref.dtype)


def _residual_rmsnorm_kernel(x_ref, dt_ref, g_ref, o_ref, *sum_ref):
    x = x_ref[...] + dt_ref[...].T
    for ref in sum_ref:
        ref[...] = x
    ms = jnp.mean(x * x, axis=-1, keepdims=True)
    o_ref[...] = (x * lax.rsqrt(ms + RMS_EPS) * g_ref[...]).astype(o_ref.dtype)


def residual_rmsnorm(x, delta_t, g, out_dtype, return_sum, rows=256):
    t, d = x.shape
    rows = min(rows, t)
    row_spec = pl.BlockSpec((rows, d), lambda i: (i, 0))
    out_specs, out_shape = [row_spec], [jax.ShapeDtypeStruct((t, d), out_dtype)]
    if return_sum:
        out_specs.append(row_spec)
        out_shape.append(jax.ShapeDtypeStruct((t, d), x.dtype))
    return pl.pallas_call(
        _residual_rmsnorm_kernel,
        grid=(t // rows,),
        in_specs=[row_spec, pl.BlockSpec((d, rows), lambda i: (0, i)), pl.BlockSpec((1, d), lambda i: (0, 0))],
        out_specs=out_specs,
        out_shape=out_shape,
        compiler_params=_params(("parallel",)),
        name="residual_rmsnorm",
    )(x, delta_t, g.reshape(1, d))


def rmsnorm(x, g, out_dtype, rows=256, with_transpose=False):
    t, d = x.shape
    rows = min(rows, t)
    row_spec = pl.BlockSpec((rows, d), lambda i: (i, 0))
    out_specs, out_shape = row_spec, jax.ShapeDtypeStruct((t, d), out_dtype)
    if with_transpose:
        out_specs = [row_spec, pl.BlockSpec((d, rows), lambda i: (0, i))]
        out_shape = [out_shape, jax.ShapeDtypeStruct((d, t), out_dtype)]
    return pl.pallas_call(
        _rmsnorm_both_kernel if with_transpose else _rmsnorm_kernel,
        grid=(t // rows,),
        in_specs=[row_spec, pl.BlockSpec((1, d), lambda i: (0, 0))],
        out_specs=out_specs,
        out_shape=out_shape,
        compiler_params=_params(("parallel",)),
        name="rmsnorm_transposed" if with_transpose else "rmsnorm",
    )(x, g.reshape(1, d))


def _rope_tables(seq_len, dim):
    inv_freq = 1.0 / (ROPE_THETA ** (jnp.arange(0, dim, 2, dtype=F32) / dim))
    ang = jnp.arange(seq_len, dtype=F32)[:, None] * inv_freq[None, :]
    ang = jnp.concatenate([ang, ang], axis=-1)
    return jnp.cos(ang), jnp.sin(ang)


def _rope_operands(seq_len):
    cos_h, sin_h = _rope_tables(seq_len, HEAD_DIM)
    lane = jnp.arange(LANES)
    sin_h_signed = jnp.where(lane < HEAD_DIM // 2, -sin_h, sin_h)
    cos_d, sin_d = _rope_tables(seq_len, DIFF_DIM)
    cos_d2 = jnp.concatenate([cos_d, cos_d], axis=-1)
    sin_d2 = jnp.concatenate([sin_d, sin_d], axis=-1)
    low = (lane % DIFF_DIM) < DIFF_DIM // 2
    sin_d_low = jnp.where(low, -sin_d2, 0.0)
    sin_d_high = jnp.where(low, 0.0, sin_d2)
    return cos_h, sin_h_signed, cos_d2, sin_d_low, sin_d_high


def _qkv_kernel(a_ref, w_ref, cos_h, sin_h, cos_d, sin_dl, sin_dh, o_ref, *, rope128_blocks, rope64_blocks):
    j = pl.program_id(1)
    acc = jnp.dot(a_ref[...], w_ref[...], preferred_element_type=F32)
    n_chunks = acc.shape[1] // LANES
    in128 = (j >= rope128_blocks[0]) & (j < rope128_blocks[1])
    in64 = (j >= rope64_blocks[0]) & (j < rope64_blocks[1])

    @pl.when(in128)
    def _():
        for c in range(n_chunks):
            x = acc[:, c * LANES:(c + 1) * LANES]
            y = x * cos_h[...] + pltpu.roll(x, HEAD_DIM // 2, 1) * sin_h[...]
            o_ref[:, c * LANES:(c + 1) * LANES] = y.astype(o_ref.dtype)

    @pl.when(in64)
    def _():
        for c in range(n_chunks):
            x = acc[:, c * LANES:(c + 1) * LANES]
            y = (x * cos_d[...] + pltpu.roll(x, LANES - DIFF_DIM // 2, 1) * sin_dl[...]
                 + pltpu.roll(x, DIFF_DIM // 2, 1) * sin_dh[...])
            o_ref[:, c * LANES:(c + 1) * LANES] = y.astype(o_ref.dtype)

    @pl.when(jnp.logical_not(in128 | in64))
    def _():
        o_ref[...] = acc.astype(o_ref.dtype)


def qkv_projection(h, w, layer, rope_ops, seq_len, rope128_cols, rope64_cols, tm=1024, tn=512):
    m, k = h.shape
    n = w.shape[2]
    tm = min(tm, seq_len)
    assert seq_len % tm == 0 and m % tm == 0 and n % tn == 0
    for lo, hi in (rope128_cols, rope64_cols):
        assert lo % tn == 0 and hi % tn == 0
    seq_blocks = seq_len // tm
    tab_spec = pl.BlockSpec((tm, LANES), lambda i, j: (i % seq_blocks, 0))
    kern = functools.partial(
        _qkv_kernel,
        rope128_blocks=(rope128_cols[0] // tn, rope128_cols[1] // tn),
        rope64_blocks=(rope64_cols[0] // tn, rope64_cols[1] // tn))
    return pl.pallas_call(
        kern,
        grid=(m // tm, n // tn),
        in_specs=[pl.BlockSpec((tm, k), lambda i, j: (i, 0)),
                  pl.BlockSpec((None, k, tn), lambda i, j: (layer, 0, j)),
                  tab_spec, tab_spec, tab_spec, tab_spec, tab_spec],
        out_specs=pl.BlockSpec((tm, tn), lambda i, j: (i, j)),
        out_shape=jax.ShapeDtypeStruct((m, n), BF16),
        compiler_params=_params(("parallel", "arbitrary")),
        name="qkv_projection",
    )(h, w, *rope_ops)


def _matmul_kernel(a_ref, w_ref, o_ref):
    o_ref[...] = jnp.dot(a_ref[...], w_ref[...], preferred_element_type=F32).astype(o_ref.dtype)


def _matmul_residual_kernel(a_ref, w_ref, r_ref, o_ref):
    o_ref[...] = r_ref[...] + jnp.dot(a_ref[...], w_ref[...], preferred_element_type=F32)


def matmul(a, w, layer, out_dtype, residual=None, tm=1024, tn=512):
    m, k = a.shape
    n = w.shape[2]
    tm, tn = min(tm, m), min(tn, n)
    assert m % tm == 0 and n % tn == 0
    in_specs = [pl.BlockSpec((tm, k), lambda i, j: (i, 0)),
                pl.BlockSpec((None, k, tn), lambda i, j: (layer, 0, j))]
    args = [a, w]
    kern = _matmul_kernel
    if residual is not None:
        in_specs.append(pl.BlockSpec((tm, tn), lambda i, j: (i, j)))
        args.append(residual)
        kern = _matmul_residual_kernel
    return pl.pallas_call(
        kern,
        grid=(m // tm, n // tn),
        in_specs=in_specs,
        out_specs=pl.BlockSpec((tm, tn), lambda i, j: (i, j)),
        out_shape=jax.ShapeDtypeStruct((m, n), out_dtype),
        compiler_params=_params(("parallel", "arbitrary")),
        name="matmul_residual" if residual is not None else "matmul",
    )(*args)


LOG2_E = math.log2(math.e)
SIGN_BIT = 0x80000000


def _softplus2(u):
    neg_abs = lax.bitcast_convert_type(lax.bitcast_convert_type(u, jnp.uint32) | jnp.uint32(SIGN_BIT), F32)
    return jnp.maximum(u, 0.0) + jnp.log2(1.0 + jnp.exp2(neg_abs))


def _sb_kernel(suffix_ref, q_ref, k_ref, v_ref, o_ref, *, tq, heads, scale2):
    i = pl.program_id(2)
    tk = 2 * tq
    suffix = suffix_ref[...]
    row = lax.broadcasted_iota(jnp.int32, (tq, tk), 0)
    col = lax.broadcasted_iota(jnp.int32, (tq, tk), 1)
    kd = lax.shift_right_logical(i, 1)
    before = col < row + (i - 2 * kd) * tq
    dims = (((1,), (1,)), ((), ()))
    qs = [q_ref[:, h * HEAD_DIM:(h + 1) * HEAD_DIM] for h in range(heads)]

    def block(kb, state, diagonal):
        start = pl.multiple_of(kb * tk, tk)
        out = []
        for h in range(heads):
            carry, acc = state[h]
            sl = slice(h * HEAD_DIM, (h + 1) * HEAD_DIM)
            k = k_ref[pl.ds(start, tk), sl]
            v = v_ref[pl.ds(start, tk), sl]
            z = lax.dot_general(qs[h], k, dims, preferred_element_type=F32) * scale2
            sp = _softplus2(z)
            spm = jnp.where(before, sp, 0.0) if diagonal else sp
            spb = spm.astype(BF16)
            sum_lo = jnp.sum(spm[:, :tq], axis=1, keepdims=True)
            sum_hi = jnp.sum(spm[:, tq:], axis=1, keepdims=True)
            later_lo = jnp.dot(spb[:, :tq], suffix, preferred_element_type=F32) + (carry + sum_hi)
            later_hi = jnp.dot(spb[:, tq:], suffix, preferred_element_type=F32) + carry
            w = jnp.exp2(z - sp - jnp.concatenate([later_lo, later_hi], axis=1))
            if diagonal:
                w = jnp.where(before, w, 0.0)
            acc = acc + jnp.dot(w.astype(BF16), v, preferred_element_type=F32)
            out.append((carry + (sum_lo + sum_hi), acc))
        return tuple(out)

    state = tuple((jnp.zeros((tq, 1), F32), jnp.zeros((tq, HEAD_DIM), F32)) for _ in range(heads))
    state = block(kd, state, True)
    state = lax.fori_loop(0, kd, lambda n, st: block(kd - 1 - n, st, False), state)
    for h in range(heads):
        o_ref[:, h * HEAD_DIM:(h + 1) * HEAD_DIM] = state[h][1].astype(o_ref.dtype)


def stick_breaking_attention(proj, seq_len, col0, tq=512, heads=4):
    b = proj.shape[0]
    tq = min(tq, seq_len // 2)
    hw = heads * HEAD_DIM
    assert seq_len % (2 * tq) == 0 and SB_HEADS % heads == 0 and col0 % hw == 0
    c0 = col0 // hw
    per = SB_HEADS // heads
    idx = jnp.arange(tq)
    suffix = (idx[:, None] > idx[None, :]).astype(BF16)
    kern = functools.partial(_sb_kernel, tq=tq, heads=heads, scale2=HEAD_DIM ** -0.5 * LOG2_E)
    return pl.pallas_call(
        kern,
        grid=(b, per, seq_len // tq),
        in_specs=[pl.BlockSpec((tq, tq), lambda bi, h, i: (0, 0)),
                  pl.BlockSpec((None, tq, hw), lambda bi, h, i: (bi, i, c0 + h)),
                  pl.BlockSpec((None, seq_len, hw), lambda bi, h, i: (bi, 0, c0 + per + h)),
                  pl.BlockSpec((None, seq_len, hw), lambda bi, h, i: (bi, 0, c0 + 2 * per + h))],
        out_specs=pl.BlockSpec((None, tq, hw), lambda bi, h, i: (bi, i, h)),
        out_shape=jax.ShapeDtypeStruct((b, seq_len, SB_HEADS * HEAD_DIM), BF16),
        compiler_params=_params(("parallel", "parallel", "arbitrary")),
        name="stick_breaking_attention",
    )(suffix, proj, proj, proj)


def _dil_kernel(q_ref, kp_ref, kc_ref, vp_ref, vc_ref, o_ref, lse_ref, *, tq, scale):
    i = pl.program_id(2)
    row = lax.broadcasted_iota(jnp.int32, (tq, tq), 0)
    col = lax.broadcasted_iota(jnp.int32, (tq, tq), 1)
    cur_ok = col <= row
    prev_ok = col >= row + jnp.where(i > 0, 0, tq)
    dims = (((1,), (1,)), ((), ()))
    for h in range(DIL_HEADS_PER_GROUP):
        sl = slice(h * HEAD_DIM, (h + 1) * HEAD_DIM)
        q = q_ref[:, sl]
        s_cur = lax.dot_general(q, kc_ref[:, sl], dims, preferred_element_type=F32) * scale
        s_prev = lax.dot_general(q, kp_ref[:, sl], dims, preferred_element_type=F32) * scale
        s_cur = jnp.where(cur_ok, s_cur, NEG_BIG)
        s_prev = jnp.where(prev_ok, s_prev, NEG_BIG)
        m = jnp.maximum(jnp.max(s_cur, axis=1, keepdims=True), jnp.max(s_prev, axis=1, keepdims=True))
        p_cur = jnp.exp(s_cur - m)
        p_prev = jnp.exp(s_prev - m)
        l = jnp.sum(p_cur, axis=1, keepdims=True) + jnp.sum(p_prev, axis=1, keepdims=True)
        o = (jnp.dot(p_cur.astype(BF16), vc_ref[:, sl], preferred_element_type=F32)
             + jnp.dot(p_prev.astype(BF16), vp_ref[:, sl], preferred_element_type=F32))
        o_ref[:, sl] = o / l
        lse_ref[:, sl] = jnp.broadcast_to(m + jnp.log(l), (tq, HEAD_DIM))


def _regroup_kernel(q_ref, k_ref, v_ref, qo_ref, ko_ref, vo_ref, scr, *, dilation):
    rows, gw = q_ref.shape
    n = rows // dilation
    for src, dst in ((q_ref, qo_ref), (k_ref, ko_ref), (v_ref, vo_ref)):
        for j in range(gw // LANES):
            scr[j] = src[:, j * LANES:(j + 1) * LANES].astype(F32)
        for c in range(dilation):
            for j in range(gw // LANES):
                dst[:, c * gw + j * LANES:c * gw + (j + 1) * LANES] = (
                    scr[j, pl.ds(c, n, stride=dilation), :].astype(dst.dtype))


def dilated_regroup(proj, dilation, q_col, k_col, v_col, rows=512):
    t, cols = proj.shape
    gw = DIL_HEADS_PER_GROUP * HEAD_DIM
    rows = min(rows, t)
    assert t % rows == 0 and rows % (16 * dilation) == 0
    in_specs = [pl.BlockSpec((rows, gw), functools.partial(lambda i, blk: (i, blk), blk=col // gw))
                for col in (q_col, k_col, v_col)]
    out_spec = pl.BlockSpec((rows // dilation, dilation * gw), lambda i: (i, 0))
    out_sds = jax.ShapeDtypeStruct((t // dilation, dilation * gw), proj.dtype)
    return pl.pallas_call(
        functools.partial(_regroup_kernel, dilation=dilation),
        grid=(t // rows,),
        in_specs=in_specs,
        out_specs=[out_spec] * 3,
        out_shape=[out_sds] * 3,
        scratch_shapes=[pltpu.VMEM((gw // LANES, rows, LANES), F32)],
        compiler_params=_params(("parallel",)),
        name=f"dilated_regroup_r{dilation}",
    )(proj, proj, proj)


def dilated_group_attention(q_arr, k_arr, v_arr, batch, seq_len, group, dilation, q_col, k_col, v_col):
    gw = DIL_HEADS_PER_GROUP * HEAD_DIM
    tq = DIL_GROUPS[group][0] // dilation
    sub_len = seq_len // dilation
    assert sub_len % tq == 0
    views, blocks, per_row = [], [], []
    for arr, col in ((q_arr, q_col), (k_arr, k_col), (v_arr, v_col)):
        width = arr.shape[1] // dilation
        assert width % gw == 0 and col % gw == 0
        views.append(arr.reshape(batch, sub_len, dilation * width))
        blocks.append(col // gw)
        per_row.append(width // gw)
    prev = lambda i: jnp.maximum(i - 1, 0)

    def spec(which, row_of):
        return pl.BlockSpec((None, tq, gw),
                            lambda bi, c, i: (bi, row_of(i), c * per_row[which] + blocks[which]))

    out_sds = jax.ShapeDtypeStruct((batch, sub_len, dilation * gw), F32)
    out_spec = pl.BlockSpec((None, tq, gw), lambda bi, c, i: (bi, i, c))
    kern = functools.partial(_dil_kernel, tq=tq, scale=HEAD_DIM ** -0.5)
    o, lse = pl.pallas_call(
        kern,
        grid=(batch, dilation, sub_len // tq),
        in_specs=[spec(0, lambda i: i), spec(1, prev), spec(1, lambda i: i), spec(2, prev), spec(2, lambda i: i)],
        out_specs=[out_spec, out_spec],
        out_shape=[out_sds, out_sds],
        compiler_params=_params(("parallel", "parallel", "arbitrary")),
        name=f"dilated_attention_g{group}",
    )(views[0], views[1], views[1], views[2], views[2])
    return o.reshape(batch * sub_len, dilation * gw), lse.reshape(batch * sub_len, dilation * gw)


def _dil_merge_kernel(*refs, dilations):
    n = len(dilations)
    o_refs, l_refs, out_ref, scratch = refs[:n], refs[n:2 * n], refs[2 * n], refs[2 * n + 1:]
    rows, gw = out_ref.shape
    outs, lses = [], []
    for g, dilation in enumerate(dilations):
        vals = []
        for src, scr in ((o_refs[g], scratch[2 * g]), (l_refs[g], scratch[2 * g + 1])):
            if dilation == 1:
                vals.append(src[...])
            else:
                for c in range(dilation):
                    for j in range(gw // LANES):
                        scr[j, pl.ds(c, rows // dilation, stride=dilation), :] = (
                            src[:, c * gw + j * LANES:c * gw + (j + 1) * LANES])
                vals.append(jnp.concatenate([scr[j] for j in range(gw // LANES)], axis=1))
        outs.append(vals[0])
        lses.append(vals[1])
    m = functools.reduce(jnp.maximum, lses)
    es = [jnp.exp(l - m) for l in lses]
    num = functools.reduce(lambda a, b: a + b, [e * o for e, o in zip(es, outs)])
    den = functools.reduce(lambda a, b: a + b, es)
    out_ref[...] = (num / den).astype(out_ref.dtype)


def dilated_merge(outs, lses, dilations, rows=512):
    gw = DIL_HEADS_PER_GROUP * HEAD_DIM
    t = outs[0].shape[0] * dilations[0]
    rows = min(rows, t)
    specs = [pl.BlockSpec((rows // r, r * gw), lambda i: (i, 0)) for r in dilations]
    return pl.pallas_call(
        functools.partial(_dil_merge_kernel, dilations=tuple(dilations)),
        grid=(t // rows,),
        in_specs=specs + specs,
        out_specs=pl.BlockSpec((rows, gw), lambda i: (i, 0)),
        out_shape=jax.ShapeDtypeStruct((t, gw), BF16),
        scratch_shapes=[pltpu.VMEM((gw // LANES, rows, LANES), F32) for _ in range(2 * len(dilations))],
        compiler_params=_params(("parallel",)),
        name="dilated_merge",
    )(*outs, *lses)


def _diff_kernel(lam_ref, g_ref, q_ref, k_ref, v_ref, o_ref, *, tq, heads, key_blocks, scale2, lam_init):
    i = pl.program_id(2)
    tk = key_blocks * tq
    lp = lam_ref[...]
    lam = (jnp.exp(jnp.sum(lp[0:1] * lp[1:2], axis=1, keepdims=True))
           - jnp.exp(jnp.sum(lp[2:3] * lp[3:4], axis=1, keepdims=True)) + lam_init)
    lane = lax.broadcasted_iota(jnp.int32, (tq, LANES), 1)
    qqs = []
    for h in range(heads):
        q = q_ref[:, h * LANES:(h + 1) * LANES].astype(F32)
        qqs.append(jnp.concatenate([jnp.where(lane < DIFF_DIM, q, 0.0), jnp.where(lane >= DIFF_DIM, q, 0.0)],
                                   axis=0).astype(BF16))
    row = lax.broadcasted_iota(jnp.int32, (2 * tq, tk), 0)
    row = jnp.where(row >= tq, row - tq, row)
    col = lax.broadcasted_iota(jnp.int32, (2 * tq, tk), 1)
    kd = i // key_blocks
    causal = col <= row + (i - key_blocks * kd) * tq
    dims = (((1,), (1,)), ((), ()))

    def block(kb, state, diagonal):
        start = pl.multiple_of(kb * tk, tk)
        out = []
        for h in range(heads):
            m, l, acc = state[h]
            sl = slice(h * LANES, (h + 1) * LANES)
            k = k_ref[pl.ds(start, tk), sl]
            v = v_ref[pl.ds(start, tk), sl]
            s = lax.dot_general(qqs[h], k, dims, preferred_element_type=F32) * scale2
            if diagonal:
                s = jnp.where(causal, s, NEG_BIG)
            m_new = jnp.maximum(m, jnp.max(s, axis=1, keepdims=True))
            alpha = jnp.exp2(m - m_new)
            p = jnp.exp2(s - m_new)
            l = alpha * l + jnp.sum(p, axis=1, keepdims=True)
            acc = alpha * acc + jnp.dot(p.astype(BF16), v, preferred_element_type=F32)
            out.append((m_new, l, acc))
        return tuple(out)

    state = tuple((jnp.full((2 * tq, 1), NEG_BIG, F32), jnp.zeros((2 * tq, 1), F32),
                   jnp.zeros((2 * tq, LANES), F32)) for _ in range(heads))
    state = block(kd, state, True)
    state = lax.fori_loop(0, kd, lambda n, st: block(n, st, False), state)
    for h in range(heads):
        _, l, acc = state[h]
        o_all = acc / l
        o = o_all[:tq] - lam * o_all[tq:]
        ms = jnp.mean(o * o, axis=-1, keepdims=True)
        y = o * lax.rsqrt(ms + RMS_EPS) * g_ref[...]
        o_ref[:, h * LANES:(h + 1) * LANES] = (y * (1.0 - lam_init)).astype(o_ref.dtype)


def differential_attention(proj, lam_params, subln_g, seq_len, q_col, k_col, v_col, lam_init, tq=1024, heads=2,
                           key_blocks=1):
    b = proj.shape[0]
    tq = min(tq, seq_len // key_blocks)
    hw = heads * LANES
    assert seq_len % (key_blocks * tq) == 0 and DIFF_HEADS % heads == 0
    assert q_col % hw == 0 and k_col % hw == 0 and v_col % hw == 0
    qb, kb, vb = q_col // hw, k_col // hw, v_col // hw
    kern = functools.partial(_diff_kernel, tq=tq, heads=heads, key_blocks=key_blocks,
                             scale2=DIFF_DIM ** -0.5 * LOG2_E, lam_init=lam_init)
    return pl.pallas_call(
        kern,
        grid=(b, DIFF_HEADS // heads, seq_len // tq),
        in_specs=[pl.BlockSpec((4, DIFF_DIM), lambda bi, h, i: (0, 0)),
                  pl.BlockSpec((1, 2 * DIFF_DIM), lambda bi, h, i: (0, 0)),
                  pl.BlockSpec((None, tq, hw), lambda bi, h, i: (bi, i, qb + h)),
                  pl.BlockSpec((None, seq_len, hw), lambda bi, h, i: (bi, 0, kb + h)),
                  pl.BlockSpec((None, seq_len, hw), lambda bi, h, i: (bi, 0, vb + h))],
        out_specs=pl.BlockSpec((None, tq, hw), lambda bi, h, i: (bi, i, h)),
        out_shape=jax.ShapeDtypeStruct((b, seq_len, DIFF_HEADS * 2 * DIFF_DIM), BF16),
        compiler_params=_params(("parallel", "parallel", "arbitrary")),
        name="differential_attention",
    )(lam_params, subln_g.reshape(1, -1), proj, proj, proj)


def _gate_merge_kernel(h_ref, wg0, wg1, wg2, o0, o1, o2, wb0, wb1, wb2, out_ref):
    h = h_ref[...]
    acc = None
    for wg, o, wb in ((wg0, o0, wb0), (wg1, o1, wb1), (wg2, o2, wb2)):
        gate = 1.0 / (1.0 + jnp.exp(-jnp.dot(h, wg[...], preferred_element_type=F32)))
        term = gate * jnp.dot(o[...], wb[...], preferred_element_type=F32)
        acc = term if acc is None else acc + term
    out_ref[...] = acc.astype(out_ref.dtype)


def gate_merge(h, w_gate, branch_outs, branch_ws, layer, tm=1024, tn=256):
    t, d = h.shape
    tm, tn = min(tm, t), min(tn, d)
    nj = d // tn
    in_specs = [pl.BlockSpec((tm, d), lambda i, j: (i, 0))]
    in_specs += [pl.BlockSpec((None, d, tn), functools.partial(lambda i, j, b: (layer, 0, b * nj + j), b=b))
                 for b in range(N_BRANCH)]
    in_specs += [pl.BlockSpec((tm, o.shape[1]), lambda i, j: (i, 0)) for o in branch_outs]
    in_specs += [pl.BlockSpec((None, w.shape[1], tn), lambda i, j: (layer, 0, j)) for w in branch_ws]
    return pl.pallas_call(
        _gate_merge_kernel,
        grid=(t // tm, nj),
        in_specs=in_specs,
        out_specs=pl.BlockSpec((tm, tn), lambda i, j: (i, j)),
        out_shape=jax.ShapeDtypeStruct((t, d), BF16),
        compiler_params=_params(("parallel", "arbitrary")),
        name="gate_merge",
    )(h, w_gate, w_gate, w_gate, *branch_outs, *branch_ws)


STAT_TAU, STAT_MAX1, STAT_MAX2, STAT_INVZ = 0, 1, 2, 3
STAT_ROWS = 8


def _top_values(x, scr, count):
    for kk in range(count):
        m = jnp.max(x, axis=0, keepdims=True)
        scr[kk:kk + 1, :] = m
        x = jnp.where(x == m, -jnp.inf, x)


def _peer_route_kernel(q_ref, keys_ref, s1_ref, s2_ref, stat_ref, a_scr, b_scr, c_scr, t_scr):
    dims = (((1,), (1,)), ((), ()))
    s1 = lax.dot_general(keys_ref[0], q_ref[:, :PEER_HALF_QDIM], dims, preferred_element_type=F32)
    s2 = lax.dot_general(keys_ref[1], q_ref[:, PEER_HALF_QDIM:], dims, preferred_element_type=F32)
    s1_ref[...] = s1
    s2_ref[...] = s2
    _top_values(s1, a_scr, PEER_TOPK)
    _top_values(s2, b_scr, PEER_TOPK)
    half = PEER_TOPK // 2
    c_scr[0:PEER_TOPK, :] = a_scr[0:1, :] + b_scr[...]
    for ii in range(1, half):
        c_scr[PEER_TOPK + (ii - 1) * half:PEER_TOPK + ii * half, :] = a_scr[ii:ii + 1, :] + b_scr[0:half, :]
    c_scr[PEER_TOPK + (half - 1) * half:PEER_CANDIDATES, :] = a_scr[half:PEER_TOPK, :] + b_scr[0:1, :]
    cand = c_scr[...]
    _top_values(cand, t_scr, PEER_TOPK)
    tau = t_scr[PEER_TOPK - 1:PEER_TOPK, :]
    best = t_scr[0:1, :]
    z = jnp.sum(jnp.where(cand >= tau, jnp.exp(cand - best), 0.0), axis=0, keepdims=True)
    stat_ref[...] = jnp.zeros_like(stat_ref)
    stat_ref[STAT_TAU:STAT_TAU + 1, :] = tau
    stat_ref[STAT_MAX1:STAT_MAX1 + 1, :] = a_scr[0:1, :]
    stat_ref[STAT_MAX2:STAT_MAX2 + 1, :] = b_scr[0:1, :]
    stat_ref[STAT_INVZ:STAT_INVZ + 1, :] = 1.0 / z


def peer_route(q, sub_keys, layer, tb=256):
    t = q.shape[0]
    tb = min(tb, t)
    score_sds = jax.ShapeDtypeStruct((PEER_HEADS, PEER_NKEYS, t), F32)
    score_spec = pl.BlockSpec((None, PEER_NKEYS, tb), lambda i, h: (h, 0, i))
    return pl.pallas_call(
        _peer_route_kernel,
        grid=(t // tb, PEER_HEADS),
        in_specs=[pl.BlockSpec((tb, 2 * PEER_HALF_QDIM), lambda i, h: (i, h)),
                  pl.BlockSpec((None, None, 2, PEER_NKEYS, PEER_HALF_QDIM), lambda i, h: (layer, h, 0, 0, 0))],
        out_specs=[score_spec, score_spec, pl.BlockSpec((None, STAT_ROWS, tb), lambda i, h: (h, 0, i))],
        out_shape=[score_sds, score_sds, jax.ShapeDtypeStruct((PEER_HEADS, STAT_ROWS, t), F32)],
        scratch_shapes=[pltpu.VMEM((PEER_TOPK, tb), F32), pltpu.VMEM((PEER_TOPK, tb), F32),
                        pltpu.VMEM((PEER_CANDIDATES, tb), F32), pltpu.VMEM((PEER_TOPK, tb), F32)],
        compiler_params=_params(("parallel", "arbitrary")),
        name="peer_route",
    )(q, sub_keys)


def _gelu(a):
    return 0.5 * a * (1.0 + lax.erf(a * (2.0 ** -0.5)))


PEER_STAGES = 3
PEER_EXPERT_BLOCK = 512
SCORE_PIECES = 4


def _peer_dense_kernel(ht_ref, u_ref, vt_ref, s1_ref, s2_ref, stat_ref, out_ref, e2_scr,
                       act_a, act_b, coef_a, coef_b, *, te, n_blocks):
    e = pl.program_id(1)

    @pl.when(e == 0)
    def _():
        out_ref[...] = jnp.zeros_like(out_ref)
        for h in range(PEER_HEADS):
            e2_scr[h] = jnp.exp(s2_ref[h] - stat_ref[h, STAT_MAX2:STAT_MAX2 + 1, :])

    n_sub = te // PEER_NKEYS
    d_model, tb = ht_ref.shape
    gate_block = e - 1

    def stages(buffers, scores=True, gates=True, output=True):
        act_cur, act_prev, coef_cur, coef_prev = buffers
        n_lane = tb // LANES
        n_slices = 2 * n_lane
        kc = d_model // n_slices
        def score_piece(p):
            ks = slice(p * (d_model // SCORE_PIECES), (p + 1) * (d_model // SCORE_PIECES))
            part = jnp.dot(u_ref[:, ks], ht_ref[ks, :], preferred_element_type=F32)
            if p == 0:
                act_cur[...] = part
            else:
                act_cur[...] += part

        assert n_slices >= SCORE_PIECES
        piece_after_slice = {p * n_slices // SCORE_PIECES - 1: p for p in range(1, SCORE_PIECES)}
        if scores:
            score_piece(0)
        s1_rows, e1_rows = [], []
        for sub in range(n_sub if gates else 0):
            i_idx = gate_block * n_sub + sub
            s1_rows.append([s1_ref[h, pl.ds(i_idx, 1), :] for h in range(PEER_HEADS)])
            e1_rows.append([jnp.exp(s1_rows[sub][h] - stat_ref[h, STAT_MAX1:STAT_MAX1 + 1, :])
                            * stat_ref[h, STAT_INVZ:STAT_INVZ + 1, :] for h in range(PEER_HEADS)])
        half = PEER_NKEYS // 2
        for r in range(n_slices):
            c, jh = divmod(r, 2)
            lanes = slice(c * LANES, (c + 1) * LANES)
            keys = slice(jh * half, (jh + 1) * half)
            tiles = [None] * n_sub
            for h in range(PEER_HEADS if gates else 0):
                s2_tile = s2_ref[h, keys, lanes]
                e2_tile = e2_scr[h, keys, lanes]
                tau = stat_ref[h, STAT_TAU:STAT_TAU + 1, lanes]
                for sub in range(n_sub):
                    term = jnp.where(s2_tile + s1_rows[sub][h][:, lanes] >= tau,
                                     e2_tile * e1_rows[sub][h][:, lanes], 0.0)
                    tiles[sub] = term if tiles[sub] is None else tiles[sub] + term
            for sub in range(n_sub if gates else 0):
                rows = slice(sub * PEER_NKEYS + jh * half, sub * PEER_NKEYS + (jh + 1) * half)
                coef_prev[rows, lanes] = (tiles[sub] * _gelu(act_prev[rows, lanes])).astype(BF16)
            if output:
                chunk = slice(r * kc, (r + 1) * kc)
                out_ref[chunk, :] += jnp.dot(vt_ref[chunk, :], coef_cur[...], preferred_element_type=F32)
            if scores and r in piece_after_slice:
                score_piece(piece_after_slice[r])

    even, odd = (act_a, act_b, coef_a, coef_b), (act_b, act_a, coef_b, coef_a)
    final = n_blocks + PEER_STAGES - 2
    assert final >= PEER_STAGES
    edge_steps = {0: dict(gates=False, output=False), 1: dict(output=False),
                  final - 1: dict(scores=False), final: dict(scores=False, gates=False)}
    for step, flags in edge_steps.items():
        pl.when(e == step)(functools.partial(stages, even if step % 2 == 0 else odd, **flags))
    interior = (e > 1) & (e < final - 1)
    parity = lax.rem(e, 2)
    pl.when(interior & (parity == 0))(functools.partial(stages, even))
    pl.when(interior & (parity == 1))(functools.partial(stages, odd))


def peer_dense(ht, u_tab, vt_blocks, layer, s1, s2, stats, tb=512):
    d, t = ht.shape
    _, n_blocks, _, te = vt_blocks.shape
    tb = min(tb, t)
    assert t % tb == 0 and u_tab.shape[1] == n_blocks * te and te % PEER_NKEYS == 0
    once = pl.Buffered(1)
    tok_spec = pl.BlockSpec((PEER_HEADS, PEER_NKEYS, tb), lambda i, e: (0, 0, i), pipeline_mode=once)
    last = n_blocks - 1
    kern = functools.partial(_peer_dense_kernel, te=te, n_blocks=n_blocks)
    return pl.pallas_call(
        kern,
        grid=(t // tb, n_blocks + PEER_STAGES - 1),
        in_specs=[pl.BlockSpec((d, tb), lambda i, e: (0, i), pipeline_mode=once),
                  pl.BlockSpec((None, te, d), lambda i, e: (layer, jnp.minimum(e, last), 0)),
                  pl.BlockSpec((None, None, d, te), lambda i, e: (layer, jnp.clip(e - 2, 0, last), 0, 0)),
                  tok_spec, tok_spec,
                  pl.BlockSpec((PEER_HEADS, STAT_ROWS, tb), lambda i, e: (0, 0, i), pipeline_mode=once)],
        out_specs=pl.BlockSpec((d, tb), lambda i, e: (0, i)),
        out_shape=jax.ShapeDtypeStruct((d, t), F32),
        scratch_shapes=[pltpu.VMEM((PEER_HEADS, PEER_NKEYS, tb), F32),
                        pltpu.VMEM((te, tb), F32), pltpu.VMEM((te, tb), F32),
                        pltpu.VMEM((te, tb), BF16), pltpu.VMEM((te, tb), BF16)],
        compiler_params=_params(("parallel", "arbitrary")),
        name="peer_dense",
    )(ht, u_tab, vt_blocks, s1, s2, stats)


def kernel(x, attn_norm_g, ffn_norm_g, final_norm_g, w_qkv, w_gate, w_branch_sb, w_branch_dil,
           w_branch_diff, w_out, diff_lambda, diff_subln_g, peer_w_q, peer_sub_keys, peer_u, peer_v):
    b, s, d = x.shape
    t = b * s
    depth = w_qkv.shape[0]
    sb_w = SB_HEADS * HEAD_DIM
    dil_w = DIL_HEADS_PER_GROUP * len(DIL_GROUPS) * HEAD_DIM
    diff_w = DIFF_HEADS * 2 * DIFF_DIM
    dl_q, dl_k, dl_v = 3 * sb_w, 3 * sb_w + dil_w, 3 * sb_w + 2 * dil_w
    df_q = 3 * sb_w + 3 * dil_w
    df_k, df_v = df_q + diff_w, df_q + 2 * diff_w
    qkv_cols = df_v + diff_w
    rope_ops = _rope_operands(s)
    w_qkv_b, w_gate_b, w_out_b, w_pq_b = (w.astype(BF16) for w in (w_qkv, w_gate, w_out, peer_w_q))
    w_branch_b = tuple(w.astype(BF16) for w in (w_branch_sb, w_branch_dil, w_branch_diff))
    keys_b, u_b = peer_sub_keys.astype(BF16), peer_u.astype(BF16)
    vt_blocks = peer_v.reshape(depth, -1, PEER_EXPERT_BLOCK, d).transpose(0, 1, 3, 2).astype(BF16)

    xt = x.reshape(t, d)
    peer_out_t = None
    for layer in range(depth):
        if peer_out_t is None:
            h = rmsnorm(xt, attn_norm_g[layer], BF16)
        else:
            h, xt = residual_rmsnorm(xt, peer_out_t, attn_norm_g[layer], BF16, return_sum=True)
        proj = qkv_projection(h, w_qkv_b, layer, rope_ops, s,
                              rope128_cols=(dl_q, dl_v), rope64_cols=(df_q, df_v))
        proj3 = proj.reshape(b, s, qkv_cols)
        o_sb = stick_breaking_attention(proj3, s, 0).reshape(t, sb_w)
        dil = []
        gw = DIL_HEADS_PER_GROUP * HEAD_DIM
        for g, (_, dilation) in enumerate(DIL_GROUPS):
            cols = (dl_q + g * gw, dl_k + g * gw, dl_v + g * gw)
            if dilation == 1:
                dil.append(dilated_group_attention(proj, proj, proj, b, s, g, dilation, *cols))
            else:
                qkv_views = dilated_regroup(proj, dilation, *cols)
                dil.append(dilated_group_attention(*qkv_views, b, s, g, dilation, 0, 0, 0))
        o_dl = dilated_merge([o for o, _ in dil], [l for _, l in dil], [r for _, r in DIL_GROUPS])
        lam_init = 0.8 - 0.6 * math.exp(-0.3 * layer)
        o_df = differential_attention(proj3, diff_lambda[layer], diff_subln_g[layer], s,
                                      df_q, df_k, df_v, lam_init).reshape(t, diff_w)
        merged = gate_merge(h, w_gate_b, (o_sb, o_dl, o_df), w_branch_b, layer)
        xt = matmul(merged, w_out_b, layer, F32, residual=xt)

        h2, h2_t = rmsnorm(xt, ffn_norm_g[layer], BF16, with_transpose=True)
        q = matmul(h2, w_pq_b, layer, BF16)
        s1, s2, stats = peer_route(q, keys_b, layer)
        peer_out_t = peer_dense(h2_t, u_b, vt_blocks, layer, s1, s2, stats)
    (out,) = residual_rmsnorm(xt, peer_out_t, final_norm_g, F32, return_sum=False)
    return out.reshape(b, s, d)
```

```python
import functools
import math

import jax
import jax.numpy as jnp
from jax import lax
from jax.experimental import pallas as pl
from jax.experimental.pallas import tpu as pltpu

F32 = jnp.float32
BF16 = jnp.bfloat16

HEAD_DIM = 128
ROPE_THETA = 10000.0
RMS_EPS = 1e-6
NEG_BIG = -1e30

SB_HEADS = 8
DIL_GROUPS = ((128, 1), (512, 4), (2048, 16))
DIL_HEADS_PER_GROUP = 4
DIFF_HEADS = 8
DIFF_DIM = 64
N_BRANCH = 3

PEER_HEADS = 8
PEER_NKEYS = 128
PEER_HALF_QDIM = 128
PEER_TOPK = 16
PEER_CANDIDATES = 16 + 7 * 8 + 8

LANES = 128
VMEM_LIMIT = 56 * 1024 * 1024


def _params(sem, vmem=VMEM_LIMIT):
    return pltpu.CompilerParams(dimension_semantics=sem, vmem_limit_bytes=vmem)


def _rmsnorm_kernel(x_ref, g_ref, o_ref):
    x = x_ref[...]
    ms = jnp.mean(x * x, axis=-1, keepdims=True)
    o_ref[...] = (x * lax.rsqrt(ms + RMS_EPS) * g_ref[...]).astype(o_ref.dtype)


def _rmsnorm_both_kernel(x_ref, g_ref, o_ref, ot_ref):
    x = x_ref[...]
    ms = jnp.mean(x * x, axis=-1, keepdims=True)
    y = x * lax.rsqrt(ms + RMS_EPS) * g_ref[...]
    o_ref[...] = y.astype(o_ref.dtype)
    ot_ref[...] = y.T.astype(ot_ref.dtype)


def _residual_rmsnorm_kernel(x_ref, dt_ref, g_ref, o_ref, *sum_ref):
    x = x_ref[...] + dt_ref[...].T
    for ref in sum_ref:
        ref[...] = x
    ms = jnp.mean(x * x, axis=-1, keepdims=True)
    o_ref[...] = (x * lax.rsqrt(ms + RMS_EPS) * g_ref[...]).astype(o_ref.dtype)


def residual_rmsnorm(x, delta_t, g, out_dtype, return_sum, rows=256):
    t, d = x.shape
    rows = min(rows, t)
    row_spec = pl.BlockSpec((rows, d), lambda i: (i, 0))
    out_specs, out_shape = [row_spec], [jax.ShapeDtypeStruct((t, d), out_dtype)]
    if return_sum:
        out_specs.append(row_spec)
        out_shape.append(jax.ShapeDtypeStruct((t, d), x.dtype))
    return pl.pallas_call(
        _residual_rmsnorm_kernel,
        grid=(t // rows,),
        in_specs=[row_spec, pl.BlockSpec((d, rows), lambda i: (0, i)), pl.BlockSpec((1, d), lambda i: (0, 0))],
        out_specs=out_specs,
        out_shape=out_shape,
        compiler_params=_params(("parallel",)),
        name="residual_rmsnorm",
    )(x, delta_t, g.reshape(1, d))


def rmsnorm(x, g, out_dtype, rows=256, with_transpose=False):
    t, d = x.shape
    rows = min(rows, t)
    row_spec = pl.BlockSpec((rows, d), lambda i: (i, 0))
    out_specs, out_shape = row_spec, jax.ShapeDtypeStruct((t, d), out_dtype)
    if with_transpose:
        out_specs = [row_spec, pl.BlockSpec((d, rows), lambda i: (0, i))]
        out_shape = [out_shape, jax.ShapeDtypeStruct((d, t), out_dtype)]
    return pl.pallas_call(
        _rmsnorm_both_kernel if with_transpose else _rmsnorm_kernel,
        grid=(t // rows,),
        in_specs=[row_spec, pl.BlockSpec((1, d), lambda i: (0, 0))],
        out_specs=out_specs,
        out_shape=out_shape,
        compiler_params=_params(("parallel",)),
        name="rmsnorm_transposed" if with_transpose else "rmsnorm",
    )(x, g.reshape(1, d))


def _rope_tables(seq_len, dim):
    inv_freq = 1.0 / (ROPE_THETA ** (jnp.arange(0, dim, 2, dtype=F32) / dim))
    ang = jnp.arange(seq_len, dtype=F32)[:, None] * inv_freq[None, :]
    ang = jnp.concatenate([ang, ang], axis=-1)
    return jnp.cos(ang), jnp.sin(ang)


def _rope_operands(seq_len):
    cos_h, sin_h = _rope_tables(seq_len, HEAD_DIM)
    lane = jnp.arange(LANES)
    sin_h_signed = jnp.where(lane < HEAD_DIM // 2, -sin_h, sin_h)
    cos_d, sin_d = _rope_tables(seq_len, DIFF_DIM)
    cos_d2 = jnp.concatenate([cos_d, cos_d], axis=-1)
    sin_d2 = jnp.concatenate([sin_d, sin_d], axis=-1)
    low = (lane % DIFF_DIM) < DIFF_DIM // 2
    sin_d_low = jnp.where(low, -sin_d2, 0.0)
    sin_d_high = jnp.where(low, 0.0, sin_d2)
    return cos_h, sin_h_signed, cos_d2, sin_d_low, sin_d_high


def _qkv_kernel(a_ref, w_ref, cos_h, sin_h, cos_d, sin_dl, sin_dh, o_ref, *, rope128_blocks, rope64_blocks):
    j = pl.program_id(1)
    acc = jnp.dot(a_ref[...], w_ref[...], preferred_element_type=F32)
    n_chunks = acc.shape[1] // LANES
    in128 = (j >= rope128_blocks[0]) & (j < rope128_blocks[1])
    in64 = (j >= rope64_blocks[0]) & (j < rope64_blocks[1])

    @pl.when(in128)
    def _():
        for c in range(n_chunks):
            x = acc[:, c * LANES:(c + 1) * LANES]
            y = x * cos_h[...] + pltpu.roll(x, HEAD_DIM // 2, 1) * sin_h[...]
            o_ref[:, c * LANES:(c + 1) * LANES] = y.astype(o_ref.dtype)

    @pl.when(in64)
    def _():
        for c in range(n_chunks):
            x = acc[:, c * LANES:(c + 1) * LANES]
            y = (x * cos_d[...] + pltpu.roll(x, LANES - DIFF_DIM // 2, 1) * sin_dl[...]
                 + pltpu.roll(x, DIFF_DIM // 2, 1) * sin_dh[...])
            o_ref[:, c * LANES:(c + 1) * LANES] = y.astype(o_ref.dtype)

    @pl.when(jnp.logical_not(in128 | in64))
    def _():
        o_ref[...] = acc.astype(o_ref.dtype)


def qkv_projection(h, w, layer, rope_ops, seq_len, rope128_cols, rope64_cols, tm=1024, tn=512):
    m, k = h.shape
    n = w.shape[2]
    tm = min(tm, seq_len)
    assert seq_len % tm == 0 and m % tm == 0 and n % tn == 0
    for lo, hi in (rope128_cols, rope64_cols):
        assert lo % tn == 0 and hi % tn == 0
    seq_blocks = seq_len // tm
    tab_spec = pl.BlockSpec((tm, LANES), lambda i, j: (i % seq_blocks, 0))
    kern = functools.partial(
        _qkv_kernel,
        rope128_blocks=(rope128_cols[0] // tn, rope128_cols[1] // tn),
        rope64_blocks=(rope64_cols[0] // tn, rope64_cols[1] // tn))
    return pl.pallas_call(
        kern,
        grid=(m // tm, n // tn),
        in_specs=[pl.BlockSpec((tm, k), lambda i, j: (i, 0)),
                  pl.BlockSpec((None, k, tn), lambda i, j: (layer, 0, j)),
                  tab_spec, tab_spec, tab_spec, tab_spec, tab_spec],
        out_specs=pl.BlockSpec((tm, tn), lambda i, j: (i, j)),
        out_shape=jax.ShapeDtypeStruct((m, n), BF16),
        compiler_params=_params(("parallel", "arbitrary")),
        name="qkv_projection",
    )(h, w, *rope_ops)


def _matmul_kernel(a_ref, w_ref, o_ref):
    o_ref[...] = jnp.dot(a_ref[...], w_ref[...], preferred_element_type=F32).astype(o_ref.dtype)


def _matmul_residual_kernel(a_ref, w_ref, r_ref, o_ref):
    o_ref[...] = r_ref[...] + jnp.dot(a_ref[...], w_ref[...], preferred_element_type=F32)


def matmul(a, w, layer, out_dtype, residual=None, tm=1024, tn=512):
    m, k = a.shape
    n = w.shape[2]
    tm, tn = min(tm, m), min(tn, n)
    assert m % tm == 0 and n % tn == 0
    in_specs = [pl.BlockSpec((tm, k), lambda i, j: (i, 0)),
                pl.BlockSpec((None, k, tn), lambda i, j: (layer, 0, j))]
    args = [a, w]
    kern = _matmul_kernel
    if residual is not None:
        in_specs.append(pl.BlockSpec((tm, tn), lambda i, j: (i, j)))
        args.append(residual)
        kern = _matmul_residual_kernel
    return pl.pallas_call(
        kern,
        grid=(m // tm, n // tn),
        in_specs=in_specs,
        out_specs=pl.BlockSpec((tm, tn), lambda i, j: (i, j)),
        out_shape=jax.ShapeDtypeStruct((m, n), out_dtype),
        compiler_params=_params(("parallel", "arbitrary")),
        name="matmul_residual" if residual is not None else "matmul",
    )(*args)


LOG2_E = math.log2(math.e)
SIGN_BIT = 0x80000000
SB_SUFFIX_BLOCK = 256


def _softplus2(u):
    neg_abs = lax.bitcast_convert_type(lax.bitcast_convert_type(u, jnp.uint32) | jnp.uint32(SIGN_BIT), F32)
    return jnp.maximum(u, 0.0) + jnp.log2(1.0 + jnp.exp2(neg_abs))


def _sb_kernel(suffix_ref, q_ref, k_ref, v_ref, o_ref, *, tq, heads, scale2):
    i = pl.program_id(2)
    tk = 2 * tq
    n_sub = tk // SB_SUFFIX_BLOCK
    suffix = suffix_ref[...]
    row = lax.broadcasted_iota(jnp.int32, (tq, tk), 0)
    col = lax.broadcasted_iota(jnp.int32, (tq, tk), 1)
    kd = lax.shift_right_logical(i, 1)
    before = col < row + (i - 2 * kd) * tq
    dims = (((1,), (1,)), ((), ()))
    qs = [q_ref[:, h * HEAD_DIM:(h + 1) * HEAD_DIM] for h in range(heads)]

    def block(kb, state, diagonal):
        start = pl.multiple_of(kb * tk, tk)
        out = []
        for h in range(heads):
            carry, acc = state[h]
            sl = slice(h * HEAD_DIM, (h + 1) * HEAD_DIM)
            k = k_ref[pl.ds(start, tk), sl]
            v = v_ref[pl.ds(start, tk), sl]
            z = lax.dot_general(qs[h], k, dims, preferred_element_type=F32) * scale2
            sp = _softplus2(z)
            spm = jnp.where(before, sp, 0.0) if diagonal else sp
            spb = spm.astype(BF16)
            later = [None] * n_sub
            for j in reversed(range(n_sub)):
                cols = slice(j * SB_SUFFIX_BLOCK, (j + 1) * SB_SUFFIX_BLOCK)
                later[j] = jnp.dot(spb[:, cols], suffix, preferred_element_type=F32) + carry
                carry = carry + jnp.sum(spm[:, cols], axis=1, keepdims=True)
            w = jnp.exp2(z - sp - jnp.concatenate(later, axis=1))
            if diagonal:
                w = jnp.where(before, w, 0.0)
            acc = acc + jnp.dot(w.astype(BF16), v, preferred_element_type=F32)
            out.append((carry, acc))
        return tuple(out)

    state = tuple((jnp.zeros((tq, 1), F32), jnp.zeros((tq, HEAD_DIM), F32)) for _ in range(heads))
    state = block(kd, state, True)
    state = lax.fori_loop(0, kd, lambda n, st: block(kd - 1 - n, st, False), state)
    for h in range(heads):
        o_ref[:, h * HEAD_DIM:(h + 1) * HEAD_DIM] = state[h][1].astype(o_ref.dtype)


def stick_breaking_attention(proj, seq_len, col0, tq=512, heads=4):
    b = proj.shape[0]
    tq = min(tq, seq_len // 2)
    hw = heads * HEAD_DIM
    assert seq_len % (2 * tq) == 0 and SB_HEADS % heads == 0 and col0 % hw == 0
    c0 = col0 // hw
    per = SB_HEADS // heads
    assert (2 * tq) % SB_SUFFIX_BLOCK == 0
    idx = jnp.arange(SB_SUFFIX_BLOCK)
    suffix = (idx[:, None] > idx[None, :]).astype(BF16)
    kern = functools.partial(_sb_kernel, tq=tq, heads=heads, scale2=HEAD_DIM ** -0.5 * LOG2_E)
    return pl.pallas_call(
        kern,
        grid=(b, per, seq_len // tq),
        in_specs=[pl.BlockSpec((SB_SUFFIX_BLOCK, SB_SUFFIX_BLOCK), lambda bi, h, i: (0, 0)),
                  pl.BlockSpec((None, tq, hw), lambda bi, h, i: (bi, i, c0 + h)),
                  pl.BlockSpec((None, seq_len, hw), lambda bi, h, i: (bi, 0, c0 + per + h)),
                  pl.BlockSpec((None, seq_len, hw), lambda bi, h, i: (bi, 0, c0 + 2 * per + h))],
        out_specs=pl.BlockSpec((None, tq, hw), lambda bi, h, i: (bi, i, h)),
        out_shape=jax.ShapeDtypeStruct((b, seq_len, SB_HEADS * HEAD_DIM), BF16),
        compiler_params=_params(("parallel", "parallel", "arbitrary")),
        name="stick_breaking_attention",
    )(suffix, proj, proj, proj)


def _dil_kernel(q_ref, kp_ref, kc_ref, vp_ref, vc_ref, o_ref, lse_ref, *, tq, scale):
    i = pl.program_id(2)
    row = lax.broadcasted_iota(jnp.int32, (tq, tq), 0)
    col = lax.broadcasted_iota(jnp.int32, (tq, tq), 1)
    cur_ok = col <= row
    prev_ok = col >= row + jnp.where(i > 0, 0, tq)
    dims = (((1,), (1,)), ((), ()))
    for h in range(DIL_HEADS_PER_GROUP):
        sl = slice(h * HEAD_DIM, (h + 1) * HEAD_DIM)
        q = q_ref[:, sl]
        s_cur = lax.dot_general(q, kc_ref[:, sl], dims, preferred_element_type=F32) * scale
        s_prev = lax.dot_general(q, kp_ref[:, sl], dims, preferred_element_type=F32) * scale
        s_cur = jnp.where(cur_ok, s_cur, NEG_BIG)
        s_prev = jnp.where(prev_ok, s_prev, NEG_BIG)
        m = jnp.maximum(jnp.max(s_cur, axis=1, keepdims=True), jnp.max(s_prev, axis=1, keepdims=True))
        p_cur = jnp.exp(s_cur - m)
        p_prev = jnp.exp(s_prev - m)
        l = jnp.sum(p_cur, axis=1, keepdims=True) + jnp.sum(p_prev, axis=1, keepdims=True)
        o = (jnp.dot(p_cur.astype(BF16), vc_ref[:, sl], preferred_element_type=F32)
             + jnp.dot(p_prev.astype(BF16), vp_ref[:, sl], preferred_element_type=F32))
        o_ref[:, sl] = o / l
        lse_ref[:, sl] = jnp.broadcast_to(m + jnp.log(l), (tq, HEAD_DIM))


def _regroup_kernel(q_ref, k_ref, v_ref, qo_ref, ko_ref, vo_ref, scr, *, dilation):
    rows, gw = q_ref.shape
    n = rows // dilation
    for src, dst in ((q_ref, qo_ref), (k_ref, ko_ref), (v_ref, vo_ref)):
        for j in range(gw // LANES):
            scr[j] = src[:, j * LANES:(j + 1) * LANES].astype(F32)
        for c in range(dilation):
            for j in range(gw // LANES):
                dst[:, c * gw + j * LANES:c * gw + (j + 1) * LANES] = (
                    scr[j, pl.ds(c, n, stride=dilation), :].astype(dst.dtype))


def dilated_regroup(proj, dilation, q_col, k_col, v_col, rows=512):
    t, cols = proj.shape
    gw = DIL_HEADS_PER_GROUP * HEAD_DIM
    rows = min(rows, t)
    assert t % rows == 0 and rows % (16 * dilation) == 0
    in_specs = [pl.BlockSpec((rows, gw), functools.partial(lambda i, blk: (i, blk), blk=col // gw))
                for col in (q_col, k_col, v_col)]
    out_spec = pl.BlockSpec((rows // dilation, dilation * gw), lambda i: (i, 0))
    out_sds = jax.ShapeDtypeStruct((t // dilation, dilation * gw), proj.dtype)
    return pl.pallas_call(
        functools.partial(_regroup_kernel, dilation=dilation),
        grid=(t // rows,),
        in_specs=in_specs,
        out_specs=[out_spec] * 3,
        out_shape=[out_sds] * 3,
        scratch_shapes=[pltpu.VMEM((gw // LANES, rows, LANES), F32)],
        compiler_params=_params(("parallel",)),
        name=f"dilated_regroup_r{dilation}",
    )(proj, proj, proj)


def dilated_group_attention(q_arr, k_arr, v_arr, batch, seq_len, group, dilation, q_col, k_col, v_col):
    gw = DIL_HEADS_PER_GROUP * HEAD_DIM
    tq = DIL_GROUPS[group][0] // dilation
    sub_len = seq_len // dilation
    assert sub_len % tq == 0
    views, blocks, per_row = [], [], []
    for arr, col in ((q_arr, q_col), (k_arr, k_col), (v_arr, v_col)):
        width = arr.shape[1] // dilation
        assert width % gw == 0 and col % gw == 0
        views.append(arr.reshape(batch, sub_len, dilation * width))
        blocks.append(col // gw)
        per_row.append(width // gw)
    prev = lambda i: jnp.maximum(i - 1, 0)

    def spec(which, row_of):
        return pl.BlockSpec((None, tq, gw),
                            lambda bi, c, i: (bi, row_of(i), c * per_row[which] + blocks[which]))

    out_sds = jax.ShapeDtypeStruct((batch, sub_len, dilation * gw), F32)
    out_spec = pl.BlockSpec((None, tq, gw), lambda bi, c, i: (bi, i, c))
    kern = functools.partial(_dil_kernel, tq=tq, scale=HEAD_DIM ** -0.5)
    o, lse = pl.pallas_call(
        kern,
        grid=(batch, dilation, sub_len // tq),
        in_specs=[spec(0, lambda i: i), spec(1, prev), spec(1, lambda i: i), spec(2, prev), spec(2, lambda i: i)],
        out_specs=[out_spec, out_spec],
        out_shape=[out_sds, out_sds],
        compiler_params=_params(("parallel", "parallel", "arbitrary")),
        name=f"dilated_attention_g{group}",
    )(views[0], views[1], views[1], views[2], views[2])
    return o.reshape(batch * sub_len, dilation * gw), lse.reshape(batch * sub_len, dilation * gw)


def _dil_merge_kernel(*refs, dilations):
    n = len(dilations)
    o_refs, l_refs, out_ref, scratch = refs[:n], refs[n:2 * n], refs[2 * n], refs[2 * n + 1:]
    rows, gw = out_ref.shape
    outs, lses = [], []
    for g, dilation in enumerate(dilations):
        vals = []
        for src, scr in ((o_refs[g], scratch[2 * g]), (l_refs[g], scratch[2 * g + 1])):
            if dilation == 1:
                vals.append(src[...])
            else:
                for c in range(dilation):
                    for j in range(gw // LANES):
                        scr[j, pl.ds(c, rows // dilation, stride=dilation), :] = (
                            src[:, c * gw + j * LANES:c * gw + (j + 1) * LANES])
                vals.append(jnp.concatenate([scr[j] for j in range(gw // LANES)], axis=1))
        outs.append(vals[0])
        lses.append(vals[1])
    m = functools.reduce(jnp.maximum, lses)
    es = [jnp.exp(l - m) for l in lses]
    num = functools.reduce(lambda a, b: a + b, [e * o for e, o in zip(es, outs)])
    den = functools.reduce(lambda a, b: a + b, es)
    out_ref[...] = (num / den).astype(out_ref.dtype)


def dilated_merge(outs, lses, dilations, rows=512):
    gw = DIL_HEADS_PER_GROUP * HEAD_DIM
    t = outs[0].shape[0] * dilations[0]
    rows = min(rows, t)
    specs = [pl.BlockSpec((rows // r, r * gw), lambda i: (i, 0)) for r in dilations]
    return pl.pallas_call(
        functools.partial(_dil_merge_kernel, dilations=tuple(dilations)),
        grid=(t // rows,),
        in_specs=specs + specs,
        out_specs=pl.BlockSpec((rows, gw), lambda i: (i, 0)),
        out_shape=jax.ShapeDtypeStruct((t, gw), BF16),
        scratch_shapes=[pltpu.VMEM((gw // LANES, rows, LANES), F32) for _ in range(2 * len(dilations))],
        compiler_params=_params(("parallel",)),
        name="dilated_merge",
    )(*outs, *lses)


def _diff_kernel(lam_ref, g_ref, q_ref, k_ref, v_ref, o_ref, *, tq, heads, key_blocks, scale2, lam_init):
    i = pl.program_id(2)
    tk = key_blocks * tq
    lp = lam_ref[...]
    lam = (jnp.exp(jnp.sum(lp[0:1] * lp[1:2], axis=1, keepdims=True))
           - jnp.exp(jnp.sum(lp[2:3] * lp[3:4], axis=1, keepdims=True)) + lam_init)
    lane = lax.broadcasted_iota(jnp.int32, (tq, LANES), 1)
    qqs = []
    for h in range(heads):
        q = q_ref[:, h * LANES:(h + 1) * LANES].astype(F32)
        qqs.append(jnp.concatenate([jnp.where(lane < DIFF_DIM, q, 0.0), jnp.where(lane >= DIFF_DIM, q, 0.0)],
                                   axis=0).astype(BF16))
    row = lax.broadcasted_iota(jnp.int32, (2 * tq, tk), 0)
    row = jnp.where(row >= tq, row - tq, row)
    col = lax.broadcasted_iota(jnp.int32, (2 * tq, tk), 1)
    kd = i // key_blocks
    causal = col <= row + (i - key_blocks * kd) * tq
    dims = (((1,), (1,)), ((), ()))

    def block(kb, state, diagonal):
        start = pl.multiple_of(kb * tk, tk)
        out = []
        for h in range(heads):
            m, l, acc = state[h]
            sl = slice(h * LANES, (h + 1) * LANES)
            k = k_ref[pl.ds(start, tk), sl]
            v = v_ref[pl.ds(start, tk), sl]
            s = lax.dot_general(qqs[h], k, dims, preferred_element_type=F32) * scale2
            if diagonal:
                s = jnp.where(causal, s, NEG_BIG)
            m_new = jnp.maximum(m, jnp.max(s, axis=1, keepdims=True))
            alpha = jnp.exp2(m - m_new)
            p = jnp.exp2(s - m_new)
            l = alpha * l + jnp.sum(p, axis=1, keepdims=True)
            acc = alpha * acc + jnp.dot(p.astype(BF16), v, preferred_element_type=F32)
            out.append((m_new, l, acc))
        return tuple(out)

    state = tuple((jnp.full((2 * tq, 1), NEG_BIG, F32), jnp.zeros((2 * tq, 1), F32),
                   jnp.zeros((2 * tq, LANES), F32)) for _ in range(heads))
    state = block(kd, state, True)
    state = lax.fori_loop(0, kd, lambda n, st: block(n, st, False), state)
    for h in range(heads):
        _, l, acc = state[h]
        o_all = acc / l
        o = o_all[:tq] - lam * o_all[tq:]
        ms = jnp.mean(o * o, axis=-1, keepdims=True)
        y = o * lax.rsqrt(ms + RMS_EPS) * g_ref[...]
        o_ref[:, h * LANES:(h + 1) * LANES] = (y * (1.0 - lam_init)).astype(o_ref.dtype)


def differential_attention(proj, lam_params, subln_g, seq_len, q_col, k_col, v_col, lam_init, tq=1024, heads=2,
                           key_blocks=1):
    b = proj.shape[0]
    tq = min(tq, seq_len // key_blocks)
    hw = heads * LANES
    assert seq_len % (key_blocks * tq) == 0 and DIFF_HEADS % heads == 0
    assert q_col % hw == 0 and k_col % hw == 0 and v_col % hw == 0
    qb, kb, vb = q_col // hw, k_col // hw, v_col // hw
    kern = functools.partial(_diff_kernel, tq=tq, heads=heads, key_blocks=key_blocks,
                             scale2=DIFF_DIM ** -0.5 * LOG2_E, lam_init=lam_init)
    return pl.pallas_call(
        kern,
        grid=(b, DIFF_HEADS // heads, seq_len // tq),
        in_specs=[pl.BlockSpec((4, DIFF_DIM), lambda bi, h, i: (0, 0)),
                  pl.BlockSpec((1, 2 * DIFF_DIM), lambda bi, h, i: (0, 0)),
                  pl.BlockSpec((None, tq, hw), lambda bi, h, i: (bi, i, qb + h)),
                  pl.BlockSpec((None, seq_len, hw), lambda bi, h, i: (bi, 0, kb + h)),
                  pl.BlockSpec((None, seq_len, hw), lambda bi, h, i: (bi, 0, vb + h))],
        out_specs=pl.BlockSpec((None, tq, hw), lambda bi, h, i: (bi, i, h)),
        out_shape=jax.ShapeDtypeStruct((b, seq_len, DIFF_HEADS * 2 * DIFF_DIM), BF16),
        compiler_params=_params(("parallel", "parallel", "arbitrary")),
        name="differential_attention",
    )(lam_params, subln_g.reshape(1, -1), proj, proj, proj)


def _gate_merge_kernel(h_ref, wg0, wg1, wg2, o0, o1, o2, wb0, wb1, wb2, out_ref):
    h = h_ref[...]
    acc = None
    for wg, o, wb in ((wg0, o0, wb0), (wg1, o1, wb1), (wg2, o2, wb2)):
        gate = 1.0 / (1.0 + jnp.exp(-jnp.dot(h, wg[...], preferred_element_type=F32)))
        term = gate * jnp.dot(o[...], wb[...], preferred_element_type=F32)
        acc = term if acc is None else acc + term
    out_ref[...] = acc.astype(out_ref.dtype)


def gate_merge(h, w_gate, branch_outs, branch_ws, layer, tm=1024, tn=256):
    t, d = h.shape
    tm, tn = min(tm, t), min(tn, d)
    nj = d // tn
    in_specs = [pl.BlockSpec((tm, d), lambda i, j: (i, 0))]
    in_specs += [pl.BlockSpec((None, d, tn), functools.partial(lambda i, j, b: (layer, 0, b * nj + j), b=b))
                 for b in range(N_BRANCH)]
    in_specs += [pl.BlockSpec((tm, o.shape[1]), lambda i, j: (i, 0)) for o in branch_outs]
    in_specs += [pl.BlockSpec((None, w.shape[1], tn), lambda i, j: (layer, 0, j)) for w in branch_ws]
    return pl.pallas_call(
        _gate_merge_kernel,
        grid=(t // tm, nj),
        in_specs=in_specs,
        out_specs=pl.BlockSpec((tm, tn), lambda i, j: (i, j)),
        out_shape=jax.ShapeDtypeStruct((t, d), BF16),
        compiler_params=_params(("parallel", "arbitrary")),
        name="gate_merge",
    )(h, w_gate, w_gate, w_gate, *branch_outs, *branch_ws)


STAT_TAU, STAT_MAX1, STAT_MAX2, STAT_INVZ = 0, 1, 2, 3
STAT_ROWS = 8


def _top_values(x, scr, count):
    for kk in range(count):
        m = jnp.max(x, axis=0, keepdims=True)
        scr[kk:kk + 1, :] = m
        x = jnp.where(x == m, -jnp.inf, x)


def _peer_route_kernel(q_ref, keys_ref, s1_ref, s2_ref, stat_ref, a_scr, b_scr, c_scr, t_scr):
    dims = (((1,), (1,)), ((), ()))
    s1 = lax.dot_general(keys_ref[0], q_ref[:, :PEER_HALF_QDIM], dims, preferred_element_type=F32)
    s2 = lax.dot_general(keys_ref[1], q_ref[:, PEER_HALF_QDIM:], dims, preferred_element_type=F32)
    s1_ref[...] = s1
    s2_ref[...] = s2
    _top_values(s1, a_scr, PEER_TOPK)
    _top_values(s2, b_scr, PEER_TOPK)
    half = PEER_TOPK // 2
    c_scr[0:PEER_TOPK, :] = a_scr[0:1, :] + b_scr[...]
    for ii in range(1, half):
        c_scr[PEER_TOPK + (ii - 1) * half:PEER_TOPK + ii * half, :] = a_scr[ii:ii + 1, :] + b_scr[0:half, :]
    c_scr[PEER_TOPK + (half - 1) * half:PEER_CANDIDATES, :] = a_scr[half:PEER_TOPK, :] + b_scr[0:1, :]
    cand = c_scr[...]
    _top_values(cand, t_scr, PEER_TOPK)
    tau = t_scr[PEER_TOPK - 1:PEER_TOPK, :]
    best = t_scr[0:1, :]
    z = jnp.sum(jnp.where(cand >= tau, jnp.exp(cand - best), 0.0), axis=0, keepdims=True)
    stat_ref[...] = jnp.zeros_like(stat_ref)
    stat_ref[STAT_TAU:STAT_TAU + 1, :] = tau
    stat_ref[STAT_MAX1:STAT_MAX1 + 1, :] = a_scr[0:1, :]
    stat_ref[STAT_MAX2:STAT_MAX2 + 1, :] = b_scr[0:1, :]
    stat_ref[STAT_INVZ:STAT_INVZ + 1, :] = 1.0 / z


def peer_route(q, sub_keys, layer, tb=256):
    t = q.shape[0]
    tb = min(tb, t)
    score_sds = jax.ShapeDtypeStruct((PEER_HEADS, PEER_NKEYS, t), F32)
    score_spec = pl.BlockSpec((None, PEER_NKEYS, tb), lambda i, h: (h, 0, i))
    return pl.pallas_call(
        _peer_route_kernel,
        grid=(t // tb, PEER_HEADS),
        in_specs=[pl.BlockSpec((tb, 2 * PEER_HALF_QDIM), lambda i, h: (i, h)),
                  pl.BlockSpec((None, None, 2, PEER_NKEYS, PEER_HALF_QDIM), lambda i, h: (layer, h, 0, 0, 0))],
        out_specs=[score_spec, score_spec, pl.BlockSpec((None, STAT_ROWS, tb), lambda i, h: (h, 0, i))],
        out_shape=[score_sds, score_sds, jax.ShapeDtypeStruct((PEER_HEADS, STAT_ROWS, t), F32)],
        scratch_shapes=[pltpu.VMEM((PEER_TOPK, tb), F32), pltpu.VMEM((PEER_TOPK, tb), F32),
                        pltpu.VMEM((PEER_CANDIDATES, tb), F32), pltpu.VMEM((PEER_TOPK, tb), F32)],
        compiler_params=_params(("parallel", "arbitrary")),
        name="peer_route",
    )(q, sub_keys)


def _gelu(a):
    return 0.5 * a * (1.0 + lax.erf(a * (2.0 ** -0.5)))


PEER_STAGES = 3
PEER_EXPERT_BLOCK = 512
SCORE_PIECES = 4


def _peer_dense_kernel(ht_ref, u_ref, vt_ref, s1_ref, s2_ref, stat_ref, out_ref, e2_scr,
                       act_a, act_b, coef_a, coef_b, *, te, n_blocks):
    e = pl.program_id(1)

    @pl.when(e == 0)
    def _():
        out_ref[...] = jnp.zeros_like(out_ref)
        for h in range(PEER_HEADS):
            e2_scr[h] = jnp.exp(s2_ref[h] - stat_ref[h, STAT_MAX2:STAT_MAX2 + 1, :])

    n_sub = te // PEER_NKEYS
    d_model, tb = ht_ref.shape
    gate_block = e - 1

    def stages(buffers, scores=True, gates=True, output=True):
        act_cur, act_prev, coef_cur, coef_prev = buffers
        n_lane = tb // LANES
        n_slices = 2 * n_lane
        kc = d_model // n_slices
        def score_piece(p):
            ks = slice(p * (d_model // SCORE_PIECES), (p + 1) * (d_model // SCORE_PIECES))
            part = jnp.dot(u_ref[:, ks], ht_ref[ks, :], preferred_element_type=F32)
            if p == 0:
                act_cur[...] = part
            else:
                act_cur[...] += part

        assert n_slices >= SCORE_PIECES
        piece_after_slice = {p * n_slices // SCORE_PIECES - 1: p for p in range(1, SCORE_PIECES)}
        if scores:
            score_piece(0)
        s1_rows, e1_rows = [], []
        for sub in range(n_sub if gates else 0):
            i_idx = gate_block * n_sub + sub
            s1_rows.append([s1_ref[h, pl.ds(i_idx, 1), :] for h in range(PEER_HEADS)])
            e1_rows.append([jnp.exp(s1_rows[sub][h] - stat_ref[h, STAT_MAX1:STAT_MAX1 + 1, :])
                            * stat_ref[h, STAT_INVZ:STAT_INVZ + 1, :] for h in range(PEER_HEADS)])
        half = PEER_NKEYS // 2
        for r in range(n_slices):
            c, jh = divmod(r, 2)
            lanes = slice(c * LANES, (c + 1) * LANES)
            keys = slice(jh * half, (jh + 1) * half)
            tiles = [None] * n_sub
            for h in range(PEER_HEADS if gates else 0):
                s2_tile = s2_ref[h, keys, lanes]
                e2_tile = e2_scr[h, keys, lanes]
                tau = stat_ref[h, STAT_TAU:STAT_TAU + 1, lanes]
                for sub in range(n_sub):
                    term = jnp.where(s2_tile + s1_rows[sub][h][:, lanes] >= tau,
                                     e2_tile * e1_rows[sub][h][:, lanes], 0.0)
                    tiles[sub] = term if tiles[sub] is None else tiles[sub] + term
            for sub in range(n_sub if gates else 0):
                rows = slice(sub * PEER_NKEYS + jh * half, sub * PEER_NKEYS + (jh + 1) * half)
                coef_prev[rows, lanes] = (tiles[sub] * _gelu(act_prev[rows, lanes])).astype(BF16)
            if output:
                chunk = slice(r * kc, (r + 1) * kc)
                out_ref[chunk, :] += jnp.dot(vt_ref[chunk, :], coef_cur[...], preferred_element_type=F32)
            if scores and r in piece_after_slice:
                score_piece(piece_after_slice[r])

    even, odd = (act_a, act_b, coef_a, coef_b), (act_b, act_a, coef_b, coef_a)
    final = n_blocks + PEER_STAGES - 2
    assert final >= PEER_STAGES
    edge_steps = {0: dict(gates=False, output=False), 1: dict(output=False),
                  final - 1: dict(scores=False), final: dict(scores=False, gates=False)}
    for step, flags in edge_steps.items():
        pl.when(e == step)(functools.partial(stages, even if step % 2 == 0 else odd, **flags))
    interior = (e > 1) & (e < final - 1)
    parity = lax.rem(e, 2)
    pl.when(interior & (parity == 0))(functools.partial(stages, even))
    pl.when(interior & (parity == 1))(functools.partial(stages, odd))


def peer_dense(ht, u_tab, vt_blocks, layer, s1, s2, stats, tb=512):
    d, t = ht.shape
    _, n_blocks, _, te = vt_blocks.shape
    tb = min(tb, t)
    assert t % tb == 0 and u_tab.shape[1] == n_blocks * te and te % PEER_NKEYS == 0
    once = pl.Buffered(1)
    tok_spec = pl.BlockSpec((PEER_HEADS, PEER_NKEYS, tb), lambda i, e: (0, 0, i), pipeline_mode=once)
    last = n_blocks - 1
    kern = functools.partial(_peer_dense_kernel, te=te, n_blocks=n_blocks)
    return pl.pallas_call(
        kern,
        grid=(t // tb, n_blocks + PEER_STAGES - 1),
        in_specs=[pl.BlockSpec((d, tb), lambda i, e: (0, i), pipeline_mode=once),
                  pl.BlockSpec((None, te, d), lambda i, e: (layer, jnp.minimum(e, last), 0)),
                  pl.BlockSpec((None, None, d, te), lambda i, e: (layer, jnp.clip(e - 2, 0, last), 0, 0)),
                  tok_spec, tok_spec,
                  pl.BlockSpec((PEER_HEADS, STAT_ROWS, tb), lambda i, e: (0, 0, i), pipeline_mode=once)],
        out_specs=pl.BlockSpec((d, tb), lambda i, e: (0, i)),
        out_shape=jax.ShapeDtypeStruct((d, t), F32),
        scratch_shapes=[pltpu.VMEM((PEER_HEADS, PEER_NKEYS, tb), F32),
                        pltpu.VMEM((te, tb), F32), pltpu.VMEM((te, tb), F32),
                        pltpu.VMEM((te, tb), BF16), pltpu.VMEM((te, tb), BF16)],
        compiler_params=_params(("parallel", "arbitrary")),
        name="peer_dense",
    )(ht, u_tab, vt_blocks, s1, s2, stats)


def kernel(x, attn_norm_g, ffn_norm_g, final_norm_g, w_qkv, w_gate, w_branch_sb, w_branch_dil,
           w_branch_diff, w_out, diff_lambda, diff_subln_g, peer_w_q, peer_sub_keys, peer_u, peer_v):
    b, s, d = x.shape
    t = b * s
    depth = w_qkv.shape[0]
    sb_w = SB_HEADS * HEAD_DIM
    dil_w = DIL_HEADS_PER_GROUP * len(DIL_GROUPS) * HEAD_DIM
    diff_w = DIFF_HEADS * 2 * DIFF_DIM
    dl_q, dl_k, dl_v = 3 * sb_w, 3 * sb_w + dil_w, 3 * sb_w + 2 * dil_w
    df_q = 3 * sb_w + 3 * dil_w
    df_k, df_v = df_q + diff_w, df_q + 2 * diff_w
    qkv_cols = df_v + diff_w
    rope_ops = _rope_operands(s)
    w_qkv_b, w_gate_b, w_out_b, w_pq_b = (w.astype(BF16) for w in (w_qkv, w_gate, w_out, peer_w_q))
    w_branch_b = tuple(w.astype(BF16) for w in (w_branch_sb, w_branch_dil, w_branch_diff))
    keys_b, u_b = peer_sub_keys.astype(BF16), peer_u.astype(BF16)
    vt_blocks = peer_v.reshape(depth, -1, PEER_EXPERT_BLOCK, d).transpose(0, 1, 3, 2).astype(BF16)

    xt = x.reshape(t, d)
    peer_out_t = None
    for layer in range(depth):
        if peer_out_t is None:
            h = rmsnorm(xt, attn_norm_g[layer], BF16)
        else:
            h, xt = residual_rmsnorm(xt, peer_out_t, attn_norm_g[layer], BF16, return_sum=True)
        proj = qkv_projection(h, w_qkv_b, layer, rope_ops, s,
                              rope128_cols=(dl_q, dl_v), rope64_cols=(df_q, df_v))
        proj3 = proj.reshape(b, s, qkv_cols)
        o_sb = stick_breaking_attention(proj3, s, 0).reshape(t, sb_w)
        dil = []
        gw = DIL_HEADS_PER_GROUP * HEAD_DIM
        for g, (_, dilation) in enumerate(DIL_GROUPS):
            cols = (dl_q + g * gw, dl_k + g * gw, dl_v + g * gw)
            if dilation == 1:
                dil.append(dilated_group_attention(proj, proj, proj, b, s, g, dilation, *cols))
            else:
                qkv_views = dilated_regroup(proj, dilation, *cols)
                dil.append(dilated_group_attention(*qkv_views, b, s, g, dilation, 0, 0, 0))
        o_dl = dilated_merge([o for o, _ in dil], [l for _, l in dil], [r for _, r in DIL_GROUPS])
        lam_init = 0.8 - 0.6 * math.exp(-0.3 * layer)
        o_df = differential_attention(proj3, diff_lambda[layer], diff_subln_g[layer], s,
                                      df_q, df_k, df_v, lam_init).reshape(t, diff_w)
        merged = gate_merge(h, w_gate_b, (o_sb, o_dl, o_df), w_branch_b, layer)
        xt = matmul(merged, w_out_b, layer, F32, residual=xt)

        h2, h2_t = rmsnorm(xt, ffn_norm_g[layer], BF16, with_transpose=True)
        q = matmul(h2, w_pq_b, layer, BF16)
        s1, s2, stats = peer_route(q, keys_b, layer)
        peer_out_t = peer_dense(h2_t, u_b, vt_blocks, layer, s1, s2, stats)
    (out,) = residual_rmsnorm(xt, peer_out_t, final_norm_g, F32, return_sum=False)
    return out.reshape(b, s, d)
```

```python
import functools
import math

import jax
import jax.numpy as jnp
from jax import lax
from jax.experimental import pallas as pl
from jax.experimental.pallas import tpu as pltpu

F32 = jnp.float32
BF16 = jnp.bfloat16

HEAD_DIM = 128
ROPE_THETA = 10000.0
RMS_EPS = 1e-6
NEG_BIG = -1e30

SB_HEADS = 8
DIL_GROUPS = ((128, 1), (512, 4), (2048, 16))
DIL_HEADS_PER_GROUP = 4
DIFF_HEADS = 8
DIFF_DIM = 64
N_BRANCH = 3

PEER_HEADS = 8
PEER_NKEYS = 128
PEER_HALF_QDIM = 128
PEER_TOPK = 16
PEER_CANDIDATES = 16 + 7 * 8 + 8

LANES = 128
VMEM_LIMIT = 56 * 1024 * 1024


def _params(sem, vmem=VMEM_LIMIT):
    return pltpu.CompilerParams(dimension_semantics=sem, vmem_limit_bytes=vmem)


def _rmsnorm_kernel(x_ref, g_ref, o_ref):
    x = x_ref[...]
    ms = jnp.mean(x * x, axis=-1, keepdims=True)
    o_ref[...] = (x * lax.rsqrt(ms + RMS_EPS) * g_ref[...]).astype(o_ref.dtype)


def _rmsnorm_both_kernel(x_ref, g_ref, o_ref, ot_ref):
    x = x_ref[...]
    ms = jnp.mean(x * x, axis=-1, keepdims=True)
    y = x * lax.rsqrt(ms + RMS_EPS) * g_ref[...]
    o_ref[...] = y.astype(o_ref.dtype)
    ot_ref[...] = y.T.astype(ot_ref.dtype)


def _residual_rmsnorm_kernel(x_ref, dt_ref, g_ref, o_ref, *sum_ref):
    x = x_ref[...] + dt_ref[...].T
    for ref in sum_ref:
        ref[...] = x
    ms = jnp.mean(x * x, axis=-1, keepdims=True)
    o_ref[...] = (x * lax.rsqrt(ms + RMS_EPS) * g_ref[...]).astype(o_ref.dtype)


def residual_rmsnorm(x, delta_t, g, out_dtype, return_sum, rows=256):
    t, d = x.shape
    rows = min(rows, t)
    row_spec = pl.BlockSpec((rows, d), lambda i: (i, 0))
    out_specs, out_shape = [row_spec], [jax.ShapeDtypeStruct((t, d), out_dtype)]
    if return_sum:
        out_specs.append(row_spec)
        out_shape.append(jax.ShapeDtypeStruct((t, d), x.dtype))
    return pl.pallas_call(
        _residual_rmsnorm_kernel,
        grid=(t // rows,),
        in_specs=[row_spec, pl.BlockSpec((d, rows), lambda i: (0, i)), pl.BlockSpec((1, d), lambda i: (0, 0))],
        out_specs=out_specs,
        out_shape=out_shape,
        compiler_params=_params(("parallel",)),
        name="residual_rmsnorm",
    )(x, delta_t, g.reshape(1, d))


def rmsnorm(x, g, out_dtype, rows=256, with_transpose=False):
    t, d = x.shape
    rows = min(rows, t)
    row_spec = pl.BlockSpec((rows, d), lambda i: (i, 0))
    out_specs, out_shape = row_spec, jax.ShapeDtypeStruct((t, d), out_dtype)
    if with_transpose:
        out_specs = [row_spec, pl.BlockSpec((d, rows), lambda i: (0, i))]
        out_shape = [out_shape, jax.ShapeDtypeStruct((d, t), out_dtype)]
    return pl.pallas_call(
        _rmsnorm_both_kernel if with_transpose else _rmsnorm_kernel,
        grid=(t // rows,),
        in_specs=[row_spec, pl.BlockSpec((1, d), lambda i: (0, 0))],
        out_specs=out_specs,
        out_shape=out_shape,
        compiler_params=_params(("parallel",)),
        name="rmsnorm_transposed" if with_transpose else "rmsnorm",
    )(x, g.reshape(1, d))


def _rope_tables(seq_len, dim):
    inv_freq = 1.0 / (ROPE_THETA ** (jnp.arange(0, dim, 2, dtype=F32) / dim))
    ang = jnp.arange(seq_len, dtype=F32)[:, None] * inv_freq[None, :]
    ang = jnp.concatenate([ang, ang], axis=-1)
    return jnp.cos(ang), jnp.sin(ang)


def _rope_operands(seq_len):
    cos_h, sin_h = _rope_tables(seq_len, HEAD_DIM)
    lane = jnp.arange(LANES)
    sin_h_signed = jnp.where(lane < HEAD_DIM // 2, -sin_h, sin_h)
    cos_d, sin_d = _rope_tables(seq_len, DIFF_DIM)
    cos_d2 = jnp.concatenate([cos_d, cos_d], axis=-1)
    sin_d2 = jnp.concatenate([sin_d, sin_d], axis=-1)
    low = (lane % DIFF_DIM) < DIFF_DIM // 2
    sin_d_low = jnp.where(low, -sin_d2, 0.0)
    sin_d_high = jnp.where(low, 0.0, sin_d2)
    return cos_h, sin_h_signed, cos_d2, sin_d_low, sin_d_high


def _qkv_kernel(a_ref, w_ref, cos_h, sin_h, cos_d, sin_dl, sin_dh, o_ref, *, rope128_blocks, rope64_blocks):
    j = pl.program_id(1)
    acc = jnp.dot(a_ref[...], w_ref[...], preferred_element_type=F32)
    n_chunks = acc.shape[1] // LANES
    in128 = (j >= rope128_blocks[0]) & (j < rope128_blocks[1])
    in64 = (j >= rope64_blocks[0]) & (j < rope64_blocks[1])

    @pl.when(in128)
    def _():
        for c in range(n_chunks):
            x = acc[:, c * LANES:(c + 1) * LANES]
            y = x * cos_h[...] + pltpu.roll(x, HEAD_DIM // 2, 1) * sin_h[...]
            o_ref[:, c * LANES:(c + 1) * LANES] = y.astype(o_ref.dtype)

    @pl.when(in64)
    def _():
        for c in range(n_chunks):
            x = acc[:, c * LANES:(c + 1) * LANES]
            y = (x * cos_d[...] + pltpu.roll(x, LANES - DIFF_DIM // 2, 1) * sin_dl[...]
                 + pltpu.roll(x, DIFF_DIM // 2, 1) * sin_dh[...])
            o_ref[:, c * LANES:(c + 1) * LANES] = y.astype(o_ref.dtype)

    @pl.when(jnp.logical_not(in128 | in64))
    def _():
        o_ref[...] = acc.astype(o_ref.dtype)


def qkv_projection(h, w, layer, rope_ops, seq_len, rope128_cols, rope64_cols, tm=1024, tn=512):
    m, k = h.shape
    n = w.shape[2]
    tm = min(tm, seq_len)
    assert seq_len % tm == 0 and m % tm == 0 and n % tn == 0
    for lo, hi in (rope128_cols, rope64_cols):
        assert lo % tn == 0 and hi % tn == 0
    seq_blocks = seq_len // tm
    tab_spec = pl.BlockSpec((tm, LANES), lambda i, j: (i % seq_blocks, 0))
    kern = functools.partial(
        _qkv_kernel,
        rope128_blocks=(rope128_cols[0] // tn, rope128_cols[1] // tn),
        rope64_blocks=(rope64_cols[0] // tn, rope64_cols[1] // tn))
    return pl.pallas_call(
        kern,
        grid=(m // tm, n // tn),
        in_specs=[pl.BlockSpec((tm, k), lambda i, j: (i, 0)),
                  pl.BlockSpec((None, k, tn), lambda i, j: (layer, 0, j)),
                  tab_spec, tab_spec, tab_spec, tab_spec, tab_spec],
        out_specs=pl.BlockSpec((tm, tn), lambda i, j: (i, j)),
        out_shape=jax.ShapeDtypeStruct((m, n), BF16),
        compiler_params=_params(("parallel", "arbitrary")),
        name="qkv_projection",
    )(h, w, *rope_ops)


def _matmul_kernel(a_ref, w_ref, o_ref):
    o_ref[...] = jnp.dot(a_ref[...], w_ref[...], preferred_element_type=F32).astype(o_ref.dtype)


def _matmul_residual_kernel(a_ref, w_ref, r_ref, o_ref):
    o_ref[...] = r_ref[...] + jnp.dot(a_ref[...], w_ref[...], preferred_element_type=F32)


def matmul(a, w, layer, out_dtype, residual=None, tm=1024, tn=512):
    m, k = a.shape
    n = w.shape[2]
    tm, tn = min(tm, m), min(tn, n)
    assert m % tm == 0 and n % tn == 0
    in_specs = [pl.BlockSpec((tm, k), lambda i, j: (i, 0)),
                pl.BlockSpec((None, k, tn), lambda i, j: (layer, 0, j))]
    args = [a, w]
    kern = _matmul_kernel
    if residual is not None:
        in_specs.append(pl.BlockSpec((tm, tn), lambda i, j: (i, j)))
        args.append(residual)
        kern = _matmul_residual_kernel
    return pl.pallas_call(
        kern,
        grid=(m // tm, n // tn),
        in_specs=in_specs,
        out_specs=pl.BlockSpec((tm, tn), lambda i, j: (i, j)),
        out_shape=jax.ShapeDtypeStruct((m, n), out_dtype),
        compiler_params=_params(("parallel", "arbitrary")),
        name="matmul_residual" if residual is not None else "matmul",
    )(*args)


LOG2_E = math.log2(math.e)
SIGN_BIT = 0x80000000
SB_SUFFIX_BLOCK = 256


def _softplus2(u):
    neg_abs = lax.bitcast_convert_type(lax.bitcast_convert_type(u, jnp.uint32) | jnp.uint32(SIGN_BIT), F32)
    return jnp.maximum(u, 0.0) + jnp.log2(1.0 + jnp.exp2(neg_abs))


def _sb_kernel(suffix_ref, q_ref, k_ref, v_ref, o_ref, *, tq, heads, scale2):
    i = pl.program_id(2)
    tk = 2 * tq
    n_sub = tk // SB_SUFFIX_BLOCK
    suffix = suffix_ref[...]
    row = lax.broadcasted_iota(jnp.int32, (tq, tk), 0)
    col = lax.broadcasted_iota(jnp.int32, (tq, tk), 1)
    kd = lax.shift_right_logical(i, 1)
    before = col < row + (i - 2 * kd) * tq
    dims = (((1,), (1,)), ((), ()))
    qs = [q_ref[:, h * HEAD_DIM:(h + 1) * HEAD_DIM] for h in range(heads)]

    def block(kb, state, diagonal):
        start = pl.multiple_of(kb * tk, tk)
        out = []
        for h in range(heads):
            carry, acc = state[h]
            sl = slice(h * HEAD_DIM, (h + 1) * HEAD_DIM)
            k = k_ref[pl.ds(start, tk), sl]
            v = v_ref[pl.ds(start, tk), sl]
            z = lax.dot_general(qs[h], k, dims, preferred_element_type=F32) * scale2
            sp = _softplus2(z)
            spm = jnp.where(before, sp, 0.0) if diagonal else sp
            spb = spm.astype(BF16)
            later = [None] * n_sub
            for j in reversed(range(n_sub)):
                cols = slice(j * SB_SUFFIX_BLOCK, (j + 1) * SB_SUFFIX_BLOCK)
                later[j] = jnp.dot(spb[:, cols], suffix, preferred_element_type=F32) + carry
                carry = carry + jnp.sum(spm[:, cols], axis=1, keepdims=True)
            w = jnp.exp2(z - sp - jnp.concatenate(later, axis=1))
            if diagonal:
                w = jnp.where(before, w, 0.0)
            acc = acc + jnp.dot(w.astype(BF16), v, preferred_element_type=F32)
            out.append((carry, acc))
        return tuple(out)

    state = tuple((jnp.zeros((tq, 1), F32), jnp.zeros((tq, HEAD_DIM), F32)) for _ in range(heads))
    state = block(kd, state, True)
    state = lax.fori_loop(0, kd, lambda n, st: block(kd - 1 - n, st, False), state)
    for h in range(heads):
        o_ref[:, h * HEAD_DIM:(h + 1) * HEAD_DIM] = state[h][1].astype(o_ref.dtype)


def stick_breaking_attention(proj, seq_len, col0, tq=512, heads=4):
    b = proj.shape[0]
    tq = min(tq, seq_len // 2)
    hw = heads * HEAD_DIM
    assert seq_len % (2 * tq) == 0 and SB_HEADS % heads == 0 and col0 % hw == 0
    c0 = col0 // hw
    per = SB_HEADS // heads
    assert (2 * tq) % SB_SUFFIX_BLOCK == 0
    idx = jnp.arange(SB_SUFFIX_BLOCK)
    suffix = (idx[:, None] > idx[None, :]).astype(BF16)
    kern = functools.partial(_sb_kernel, tq=tq, heads=heads, scale2=HEAD_DIM ** -0.5 * LOG2_E)
    return pl.pallas_call(
        kern,
        grid=(b, per, seq_len // tq),
        in_specs=[pl.BlockSpec((SB_SUFFIX_BLOCK, SB_SUFFIX_BLOCK), lambda bi, h, i: (0, 0)),
                  pl.BlockSpec((None, tq, hw), lambda bi, h, i: (bi, i, c0 + h)),
                  pl.BlockSpec((None, seq_len, hw), lambda bi, h, i: (bi, 0, c0 + per + h)),
                  pl.BlockSpec((None, seq_len, hw), lambda bi, h, i: (bi, 0, c0 + 2 * per + h))],
        out_specs=pl.BlockSpec((None, tq, hw), lambda bi, h, i: (bi, i, h)),
        out_shape=jax.ShapeDtypeStruct((b, seq_len, SB_HEADS * HEAD_DIM), BF16),
        compiler_params=_params(("parallel", "parallel", "arbitrary")),
        name="stick_breaking_attention",
    )(suffix, proj, proj, proj)


def _dil_kernel(q_ref, kp_ref, kc_ref, vp_ref, vc_ref, o_ref, lse_ref, *, tq, scale):
    i = pl.program_id(2)
    row = lax.broadcasted_iota(jnp.int32, (tq, tq), 0)
    col = lax.broadcasted_iota(jnp.int32, (tq, tq), 1)
    cur_ok = col <= row
    prev_ok = col >= row + jnp.where(i > 0, 0, tq)
    dims = (((1,), (1,)), ((), ()))
    for h in range(DIL_HEADS_PER_GROUP):
        sl = slice(h * HEAD_DIM, (h + 1) * HEAD_DIM)
        q = q_ref[:, sl]
        s_cur = lax.dot_general(q, kc_ref[:, sl], dims, preferred_element_type=F32) * scale
        s_prev = lax.dot_general(q, kp_ref[:, sl], dims, preferred_element_type=F32) * scale
        s_cur = jnp.where(cur_ok, s_cur, NEG_BIG)
        s_prev = jnp.where(prev_ok, s_prev, NEG_BIG)
        m = jnp.maximum(jnp.max(s_cur, axis=1, keepdims=True), jnp.max(s_prev, axis=1, keepdims=True))
        p_cur = jnp.exp(s_cur - m)
        p_prev = jnp.exp(s_prev - m)
        l = jnp.sum(p_cur, axis=1, keepdims=True) + jnp.sum(p_prev, axis=1, keepdims=True)
        o = (jnp.dot(p_cur.astype(BF16), vc_ref[:, sl], preferred_element_type=F32)
             + jnp.dot(p_prev.astype(BF16), vp_ref[:, sl], preferred_element_type=F32))
        o_ref[:, sl] = o / l
        lse_ref[:, sl] = jnp.broadcast_to(m + jnp.log(l), (tq, HEAD_DIM))


def _regroup_kernel(q_ref, k_ref, v_ref, qo_ref, ko_ref, vo_ref, scr, *, dilation):
    rows, gw = q_ref.shape
    n = rows // dilation
    for src, dst in ((q_ref, qo_ref), (k_ref, ko_ref), (v_ref, vo_ref)):
        for j in range(gw // LANES):
            scr[j] = src[:, j * LANES:(j + 1) * LANES].astype(F32)
        for c in range(dilation):
            for j in range(gw // LANES):
                dst[:, c * gw + j * LANES:c * gw + (j + 1) * LANES] = (
                    scr[j, pl.ds(c, n, stride=dilation), :].astype(dst.dtype))


def dilated_regroup(proj, dilation, q_col, k_col, v_col, rows=512):
    t, cols = proj.shape
    gw = DIL_HEADS_PER_GROUP * HEAD_DIM
    rows = min(rows, t)
    assert t % rows == 0 and rows % (16 * dilation) == 0
    in_specs = [pl.BlockSpec((rows, gw), functools.partial(lambda i, blk: (i, blk), blk=col // gw))
                for col in (q_col, k_col, v_col)]
    out_spec = pl.BlockSpec((rows // dilation, dilation * gw), lambda i: (i, 0))
    out_sds = jax.ShapeDtypeStruct((t // dilation, dilation * gw), proj.dtype)
    return pl.pallas_call(
        functools.partial(_regroup_kernel, dilation=dilation),
        grid=(t // rows,),
        in_specs=in_specs,
        out_specs=[out_spec] * 3,
        out_shape=[out_sds] * 3,
        scratch_shapes=[pltpu.VMEM((gw // LANES, rows, LANES), F32)],
        compiler_params=_params(("parallel",)),
        name=f"dilated_regroup_r{dilation}",
    )(proj, proj, proj)


def dilated_group_attention(q_arr, k_arr, v_arr, batch, seq_len, group, dilation, q_col, k_col, v_col):
    gw = DIL_HEADS_PER_GROUP * HEAD_DIM
    tq = DIL_GROUPS[group][0] // dilation
    sub_len = seq_len // dilation
    assert sub_len % tq == 0
    views, blocks, per_row = [], [], []
    for arr, col in ((q_arr, q_col), (k_arr, k_col), (v_arr, v_col)):
        width = arr.shape[1] // dilation
        assert width % gw == 0 and col % gw == 0
        views.append(arr.reshape(batch, sub_len, dilation * width))
        blocks.append(col // gw)
        per_row.append(width // gw)
    prev = lambda i: jnp.maximum(i - 1, 0)

    def spec(which, row_of):
        return pl.BlockSpec((None, tq, gw),
                            lambda bi, c, i: (bi, row_of(i), c * per_row[which] + blocks[which]))

    out_sds = jax.ShapeDtypeStruct((batch, sub_len, dilation * gw), F32)
    out_spec = pl.BlockSpec((None, tq, gw), lambda bi, c, i: (bi, i, c))
    kern = functools.partial(_dil_kernel, tq=tq, scale=HEAD_DIM ** -0.5)
    o, lse = pl.pallas_call(
        kern,
        grid=(batch, dilation, sub_len // tq),
        in_specs=[spec(0, lambda i: i), spec(1, prev), spec(1, lambda i: i), spec(2, prev), spec(2, lambda i: i)],
        out_specs=[out_spec, out_spec],
        out_shape=[out_sds, out_sds],
        compiler_params=_params(("parallel", "parallel", "arbitrary")),
        name=f"dilated_attention_g{group}",
    )(views[0], views[1], views[1], views[2], views[2])
    return o.reshape(batch * sub_len, dilation * gw), lse.reshape(batch * sub_len, dilation * gw)


def _dil_merge_kernel(*refs, dilations):
    n = len(dilations)
    o_refs, l_refs, out_ref, scratch = refs[:n], refs[n:2 * n], refs[2 * n], refs[2 * n + 1:]
    rows, gw = out_ref.shape
    outs, lses = [], []
    for g, dilation in enumerate(dilations):
        vals = []
        for src, scr in ((o_refs[g], scratch[2 * g]), (l_refs[g], scratch[2 * g + 1])):
            if dilation == 1:
                vals.append(src[...])
            else:
                for c in range(dilation):
                    for j in range(gw // LANES):
                        scr[j, pl.ds(c, rows // dilation, stride=dilation), :] = (
                            src[:, c * gw + j * LANES:c * gw + (j + 1) * LANES])
                vals.append(jnp.concatenate([scr[j] for j in range(gw // LANES)], axis=1))
        outs.append(vals[0])
        lses.append(vals[1])
    m = functools.reduce(jnp.maximum, lses)
    es = [jnp.exp(l - m) for l in lses]
    num = functools.reduce(lambda a, b: a + b, [e * o for e, o in zip(es, outs)])
    den = functools.reduce(lambda a, b: a + b, es)
    out_ref[...] = (num / den).astype(out_ref.dtype)


def dilated_merge(outs, lses, dilations, rows=512):
    gw = DIL_HEADS_PER_GROUP * HEAD_DIM
    t = outs[0].shape[0] * dilations[0]
    rows = min(rows, t)
    specs = [pl.BlockSpec((rows // r, r * gw), lambda i: (i, 0)) for r in dilations]
    return pl.pallas_call(
        functools.partial(_dil_merge_kernel, dilations=tuple(dilations)),
        grid=(t // rows,),
        in_specs=specs + specs,
        out_specs=pl.BlockSpec((rows, gw), lambda i: (i, 0)),
        out_shape=jax.ShapeDtypeStruct((t, gw), BF16),
        scratch_shapes=[pltpu.VMEM((gw // LANES, rows, LANES), F32) for _ in range(2 * len(dilations))],
        compiler_params=_params(("parallel",)),
        name="dilated_merge",
    )(*outs, *lses)


def _diff_kernel(lam_ref, g_ref, q_ref, k_ref, v_ref, o_ref, *, tq, heads, key_blocks, scale2, lam_init):
    i = pl.program_id(2)
    tk = key_blocks * tq
    lp = lam_ref[...]
    lam = (jnp.exp(jnp.sum(lp[0:1] * lp[1:2], axis=1, keepdims=True))
           - jnp.exp(jnp.sum(lp[2:3] * lp[3:4], axis=1, keepdims=True)) + lam_init)
    lane = lax.broadcasted_iota(jnp.int32, (tq, LANES), 1)
    qqs = []
    for h in range(heads):
        q = q_ref[:, h * LANES:(h + 1) * LANES].astype(F32)
        qqs.append(jnp.concatenate([jnp.where(lane < DIFF_DIM, q, 0.0), jnp.where(lane >= DIFF_DIM, q, 0.0)],
                                   axis=0).astype(BF16))
    row = lax.broadcasted_iota(jnp.int32, (2 * tq, tk), 0)
    row = jnp.where(row >= tq, row - tq, row)
    col = lax.broadcasted_iota(jnp.int32, (2 * tq, tk), 1)
    kd = i // key_blocks
    causal = col <= row + (i - key_blocks * kd) * tq
    dims = (((1,), (1,)), ((), ()))

    def block(kb, state, diagonal):
        start = pl.multiple_of(kb * tk, tk)
        out = []
        for h in range(heads):
            m, l, acc = state[h]
            sl = slice(h * LANES, (h + 1) * LANES)
            k = k_ref[pl.ds(start, tk), sl]
            v = v_ref[pl.ds(start, tk), sl]
            s = lax.dot_general(qqs[h], k, dims, preferred_element_type=F32) * scale2
            if diagonal:
                s = jnp.where(causal, s, NEG_BIG)
            m_new = jnp.maximum(m, jnp.max(s, axis=1, keepdims=True))
            alpha = jnp.exp2(m - m_new)
            p = jnp.exp2(s - m_new)
            l = alpha * l + jnp.sum(p, axis=1, keepdims=True)
            acc = alpha * acc + jnp.dot(p.astype(BF16), v, preferred_element_type=F32)
            out.append((m_new, l, acc))
        return tuple(out)

    state = tuple((jnp.full((2 * tq, 1), NEG_BIG, F32), jnp.zeros((2 * tq, 1), F32),
                   jnp.zeros((2 * tq, LANES), F32)) for _ in range(heads))
    state = block(kd, state, True)
    state = lax.fori_loop(0, kd, lambda n, st: block(n, st, False), state)
    for h in range(heads):
        _, l, acc = state[h]
        o_all = acc / l
        o = o_all[:tq] - lam * o_all[tq:]
        ms = jnp.mean(o * o, axis=-1, keepdims=True)
        y = o * lax.rsqrt(ms + RMS_EPS) * g_ref[...]
        o_ref[:, h * LANES:(h + 1) * LANES] = (y * (1.0 - lam_init)).astype(o_ref.dtype)


def differential_attention(proj, lam_params, subln_g, seq_len, q_col, k_col, v_col, lam_init, tq=1024, heads=2,
                           key_blocks=1):
    b = proj.shape[0]
    tq = min(tq, seq_len // key_blocks)
    hw = heads * LANES
    assert seq_len % (key_blocks * tq) == 0 and DIFF_HEADS % heads == 0
    assert q_col % hw == 0 and k_col % hw == 0 and v_col % hw == 0
    qb, kb, vb = q_col // hw, k_col // hw, v_col // hw
    kern = functools.partial(_diff_kernel, tq=tq, heads=heads, key_blocks=key_blocks,
                             scale2=DIFF_DIM ** -0.5 * LOG2_E, lam_init=lam_init)
    return pl.pallas_call(
        kern,
        grid=(b, DIFF_HEADS // heads, seq_len // tq),
        in_specs=[pl.BlockSpec((4, DIFF_DIM), lambda bi, h, i: (0, 0)),
                  pl.BlockSpec((1, 2 * DIFF_DIM), lambda bi, h, i: (0, 0)),
                  pl.BlockSpec((None, tq, hw), lambda bi, h, i: (bi, i, qb + h)),
                  pl.BlockSpec((None, seq_len, hw), lambda bi, h, i: (bi, 0, kb + h)),
                  pl.BlockSpec((None, seq_len, hw), lambda bi, h, i: (bi, 0, vb + h))],
        out_specs=pl.BlockSpec((None, tq, hw), lambda bi, h, i: (bi, i, h)),
        out_shape=jax.ShapeDtypeStruct((b, seq_len, DIFF_HEADS * 2 * DIFF_DIM), BF16),
        compiler_params=_params(("parallel", "parallel", "arbitrary")),
        name="differential_attention",
    )(lam_params, subln_g.reshape(1, -1), proj, proj, proj)


def _gate_merge_kernel(h_ref, wg0, wg1, wg2, o0, o1, o2, wb0, wb1, wb2, out_ref):
    h = h_ref[...]
    acc = None
    for wg, o, wb in ((wg0, o0, wb0), (wg1, o1, wb1), (wg2, o2, wb2)):
        gate = 1.0 / (1.0 + jnp.exp(-jnp.dot(h, wg[...], preferred_element_type=F32)))
        term = gate * jnp.dot(o[...], wb[...], preferred_element_type=F32)
        acc = term if acc is None else acc + term
    out_ref[...] = acc.astype(out_ref.dtype)


def gate_merge(h, w_gate, branch_outs, branch_ws, layer, tm=1024, tn=256):
    t, d = h.shape
    tm, tn = min(tm, t), min(tn, d)
    nj = d // tn
    in_specs = [pl.BlockSpec((tm, d), lambda i, j: (i, 0))]
    in_specs += [pl.BlockSpec((None, d, tn), functools.partial(lambda i, j, b: (layer, 0, b * nj + j), b=b))
                 for b in range(N_BRANCH)]
    in_specs += [pl.BlockSpec((tm, o.shape[1]), lambda i, j: (i, 0)) for o in branch_outs]
    in_specs += [pl.BlockSpec((None, w.shape[1], tn), lambda i, j: (layer, 0, j)) for w in branch_ws]
    return pl.pallas_call(
        _gate_merge_kernel,
        grid=(t // tm, nj),
        in_specs=in_specs,
        out_specs=pl.BlockSpec((tm, tn), lambda i, j: (i, j)),
        out_shape=jax.ShapeDtypeStruct((t, d), BF16),
        compiler_params=_params(("parallel", "arbitrary")),
        name="gate_merge",
    )(h, w_gate, w_gate, w_gate, *branch_outs, *branch_ws)


STAT_TAU, STAT_MAX1, STAT_MAX2, STAT_INVZ = 0, 1, 2, 3
STAT_ROWS = 8


def _top_values(x, scr, count):
    for kk in range(count):
        m = jnp.max(x, axis=0, keepdims=True)
        scr[kk:kk + 1, :] = m
        x = jnp.where(x == m, -jnp.inf, x)


def _peer_route_kernel(q_ref, keys_ref, s1_ref, s2_ref, stat_ref, a_scr, b_scr, c_scr, t_scr):
    dims = (((1,), (1,)), ((), ()))
    s1 = lax.dot_general(keys_ref[0], q_ref[:, :PEER_HALF_QDIM], dims, preferred_element_type=F32)
    s2 = lax.dot_general(keys_ref[1], q_ref[:, PEER_HALF_QDIM:], dims, preferred_element_type=F32)
    s1_ref[...] = s1
    s2_ref[...] = s2
    _top_values(s1, a_scr, PEER_TOPK)
    _top_values(s2, b_scr, PEER_TOPK)
    half = PEER_TOPK // 2
    c_scr[0:PEER_TOPK, :] = a_scr[0:1, :] + b_scr[...]
    for ii in range(1, half):
        c_scr[PEER_TOPK + (ii - 1) * half:PEER_TOPK + ii * half, :] = a_scr[ii:ii + 1, :] + b_scr[0:half, :]
    c_scr[PEER_TOPK + (half - 1) * half:PEER_CANDIDATES, :] = a_scr[half:PEER_TOPK, :] + b_scr[0:1, :]
    cand = c_scr[...]
    _top_values(cand, t_scr, PEER_TOPK)
    tau = t_scr[PEER_TOPK - 1:PEER_TOPK, :]
    best = t_scr[0:1, :]
    z = jnp.sum(jnp.where(cand >= tau, jnp.exp(cand - best), 0.0), axis=0, keepdims=True)
    stat_ref[...] = jnp.zeros_like(stat_ref)
    stat_ref[STAT_TAU:STAT_TAU + 1, :] = tau
    stat_ref[STAT_MAX1:STAT_MAX1 + 1, :] = a_scr[0:1, :]
    stat_ref[STAT_MAX2:STAT_MAX2 + 1, :] = b_scr[0:1, :]
    stat_ref[STAT_INVZ:STAT_INVZ + 1, :] = 1.0 / z


def peer_route(q, sub_keys, layer, tb=256):
    t = q.shape[0]
    tb = min(tb, t)
    score_sds = jax.ShapeDtypeStruct((PEER_HEADS, PEER_NKEYS, t), F32)
    score_spec = pl.BlockSpec((None, PEER_NKEYS, tb), lambda i, h: (h, 0, i))
    return pl.pallas_call(
        _peer_route_kernel,
        grid=(t // tb, PEER_HEADS),
        in_specs=[pl.BlockSpec((tb, 2 * PEER_HALF_QDIM), lambda i, h: (i, h)),
                  pl.BlockSpec((None, None, 2, PEER_NKEYS, PEER_HALF_QDIM), lambda i, h: (layer, h, 0, 0, 0))],
        out_specs=[score_spec, score_spec, pl.BlockSpec((None, STAT_ROWS, tb), lambda i, h: (h, 0, i))],
        out_shape=[score_sds, score_sds, jax.ShapeDtypeStruct((PEER_HEADS, STAT_ROWS, t), F32)],
        scratch_shapes=[pltpu.VMEM((PEER_TOPK, tb), F32), pltpu.VMEM((PEER_TOPK, tb), F32),
                        pltpu.VMEM((PEER_CANDIDATES, tb), F32), pltpu.VMEM((PEER_TOPK, tb), F32)],
        compiler_params=_params(("parallel", "arbitrary")),
        name="peer_route",
    )(q, sub_keys)


def _gelu(a):
    return 0.5 * a * (1.0 + lax.erf(a * (2.0 ** -0.5)))


PEER_STAGES = 3
PEER_EXPERT_BLOCK = 512
GATE_KEY_SPLITS = 8


def _peer_dense_kernel(ht_ref, u_ref, vt_ref, s1_ref, s2_ref, stat_ref, out_ref, e2_scr,
                       act_a, act_b, coef_a, coef_b, *, te, n_blocks):
    e = pl.program_id(1)

    @pl.when(e == 0)
    def _():
        out_ref[...] = jnp.zeros_like(out_ref)
        for h in range(PEER_HEADS):
            e2_scr[h] = jnp.exp(s2_ref[h] - stat_ref[h, STAT_MAX2:STAT_MAX2 + 1, :])

    n_sub = te // PEER_NKEYS
    d_model, tb = ht_ref.shape
    gate_block = e - 1

    def stages(buffers, scores=True, gates=True, output=True):
        act_cur, act_prev, coef_cur, coef_prev = buffers
        n_lane = tb // LANES
        n_slices = GATE_KEY_SPLITS * n_lane
        kc = d_model // n_slices
        if scores:
            act_cur[...] = jnp.dot(u_ref[...], ht_ref[...], preferred_element_type=F32)
        s1_rows, e1_rows = [], []
        for sub in range(n_sub if gates else 0):
            i_idx = gate_block * n_sub + sub
            s1_rows.append([s1_ref[h, pl.ds(i_idx, 1), :] for h in range(PEER_HEADS)])
            e1_rows.append([jnp.exp(s1_rows[sub][h] - stat_ref[h, STAT_MAX1:STAT_MAX1 + 1, :])
                            * stat_ref[h, STAT_INVZ:STAT_INVZ + 1, :] for h in range(PEER_HEADS)])
        half = PEER_NKEYS // GATE_KEY_SPLITS
        for r in range(n_slices):
            c, jh = divmod(r, GATE_KEY_SPLITS)
            lanes = slice(c * LANES, (c + 1) * LANES)
            keys = slice(jh * half, (jh + 1) * half)
            tiles = [None] * n_sub
            for h in range(PEER_HEADS if gates else 0):
                s2_tile = s2_ref[h, keys, lanes]
                e2_tile = e2_scr[h, keys, lanes]
                tau = stat_ref[h, STAT_TAU:STAT_TAU + 1, lanes]
                for sub in range(n_sub):
                    term = jnp.where(s2_tile + s1_rows[sub][h][:, lanes] >= tau,
                                     e2_tile * e1_rows[sub][h][:, lanes], 0.0)
                    tiles[sub] = term if tiles[sub] is None else tiles[sub] + term
            for sub in range(n_sub if gates else 0):
                rows = slice(sub * PEER_NKEYS + jh * half, sub * PEER_NKEYS + (jh + 1) * half)
                coef_prev[rows, lanes] = (tiles[sub] * _gelu(act_prev[rows, lanes])).astype(BF16)
            if output:
                chunk = slice(r * kc, (r + 1) * kc)
                out_ref[chunk, :] += jnp.dot(vt_ref[chunk, :], coef_cur[...], preferred_element_type=F32)

    even, odd = (act_a, act_b, coef_a, coef_b), (act_b, act_a, coef_b, coef_a)
    final = n_blocks + PEER_STAGES - 2
    assert final >= PEER_STAGES
    edge_steps = {0: dict(gates=False, output=False), 1: dict(output=False),
                  final - 1: dict(scores=False), final: dict(scores=False, gates=False)}
    for step, flags in edge_steps.items():
        pl.when(e == step)(functools.partial(stages, even if step % 2 == 0 else odd, **flags))
    interior = (e > 1) & (e < final - 1)
    parity = lax.rem(e, 2)
    pl.when(interior & (parity == 0))(functools.partial(stages, even))
    pl.when(interior & (parity == 1))(functools.partial(stages, odd))


def peer_dense(ht, u_tab, vt_blocks, layer, s1, s2, stats, tb=512):
    d, t = ht.shape
    _, n_blocks, _, te = vt_blocks.shape
    tb = min(tb, t)
    assert t % tb == 0 and u_tab.shape[1] == n_blocks * te and te % PEER_NKEYS == 0
    once = pl.Buffered(1)
    tok_spec = pl.BlockSpec((PEER_HEADS, PEER_NKEYS, tb), lambda i, e: (0, 0, i), pipeline_mode=once)
    last = n_blocks - 1
    kern = functools.partial(_peer_dense_kernel, te=te, n_blocks=n_blocks)
    return pl.pallas_call(
        kern,
        grid=(t // tb, n_blocks + PEER_STAGES - 1),
        in_specs=[pl.BlockSpec((d, tb), lambda i, e: (0, i), pipeline_mode=once),
                  pl.BlockSpec((None, te, d), lambda i, e: (layer, jnp.minimum(e, last), 0)),
                  pl.BlockSpec((None, None, d, te), lambda i, e: (layer, jnp.clip(e - 2, 0, last), 0, 0)),
                  tok_spec, tok_spec,
                  pl.BlockSpec((PEER_HEADS, STAT_ROWS, tb), lambda i, e: (0, 0, i), pipeline_mode=once)],
        out_specs=pl.BlockSpec((d, tb), lambda i, e: (0, i)),
        out_shape=jax.ShapeDtypeStruct((d, t), F32),
        scratch_shapes=[pltpu.VMEM((PEER_HEADS, PEER_NKEYS, tb), F32),
                        pltpu.VMEM((te, tb), F32), pltpu.VMEM((te, tb), F32),
                        pltpu.VMEM((te, tb), BF16), pltpu.VMEM((te, tb), BF16)],
        compiler_params=_params(("parallel", "arbitrary")),
        name="peer_dense",
    )(ht, u_tab, vt_blocks, s1, s2, stats)


def kernel(x, attn_norm_g, ffn_norm_g, final_norm_g, w_qkv, w_gate, w_branch_sb, w_branch_dil,
           w_branch_diff, w_out, diff_lambda, diff_subln_g, peer_w_q, peer_sub_keys, peer_u, peer_v):
    b, s, d = x.shape
    t = b * s
    depth = w_qkv.shape[0]
    sb_w = SB_HEADS * HEAD_DIM
    dil_w = DIL_HEADS_PER_GROUP * len(DIL_GROUPS) * HEAD_DIM
    diff_w = DIFF_HEADS * 2 * DIFF_DIM
    dl_q, dl_k, dl_v = 3 * sb_w, 3 * sb_w + dil_w, 3 * sb_w + 2 * dil_w
    df_q = 3 * sb_w + 3 * dil_w
    df_k, df_v = df_q + diff_w, df_q + 2 * diff_w
    qkv_cols = df_v + diff_w
    rope_ops = _rope_operands(s)
    w_qkv_b, w_gate_b, w_out_b, w_pq_b = (w.astype(BF16) for w in (w_qkv, w_gate, w_out, peer_w_q))
    w_branch_b = tuple(w.astype(BF16) for w in (w_branch_sb, w_branch_dil, w_branch_diff))
    keys_b, u_b = peer_sub_keys.astype(BF16), peer_u.astype(BF16)
    vt_blocks = peer_v.reshape(depth, -1, PEER_EXPERT_BLOCK, d).transpose(0, 1, 3, 2).astype(BF16)

    xt = x.reshape(t, d)
    peer_out_t = None
    for layer in range(depth):
        if peer_out_t is None:
            h = rmsnorm(xt, attn_norm_g[layer], BF16)
        else:
            h, xt = residual_rmsnorm(xt, peer_out_t, attn_norm_g[layer], BF16, return_sum=True)
        proj = qkv_projection(h, w_qkv_b, layer, rope_ops, s,
                              rope128_cols=(dl_q, dl_v), rope64_cols=(df_q, df_v))
        proj3 = proj.reshape(b, s, qkv_cols)
        o_sb = stick_breaking_attention(proj3, s, 0).reshape(t, sb_w)
        dil = []
        gw = DIL_HEADS_PER_GROUP * HEAD_DIM
        for g, (_, dilation) in enumerate(DIL_GROUPS):
            cols = (dl_q + g * gw, dl_k + g * gw, dl_v + g * gw)
            if dilation == 1:
                dil.append(dilated_group_attention(proj, proj, proj, b, s, g, dilation, *cols))
            else:
                qkv_views = dilated_regroup(proj, dilation, *cols)
                dil.append(dilated_group_attention(*qkv_views, b, s, g, dilation, 0, 0, 0))
        o_dl = dilated_merge([o for o, _ in dil], [l for _, l in dil], [r for _, r in DIL_GROUPS])
        lam_init = 0.8 - 0.6 * math.exp(-0.3 * layer)
        o_df = differential_attention(proj3, diff_lambda[layer], diff_subln_g[layer], s,
                                      df_q, df_k, df_v, lam_init).reshape(t, diff_w)
        merged = gate_merge(h, w_gate_b, (o_sb, o_dl, o_df), w_branch_b, layer)
        xt = matmul(merged, w_out_b, layer, F32, residual=xt)

        h2, h2_t = rmsnorm(xt, ffn_norm_g[layer], BF16, with_transpose=True)
        q = matmul(h2, w_pq_b, layer, BF16)
        s1, s2, stats = peer_route(q, keys_b, layer)
        peer_out_t = peer_dense(h2_t, u_b, vt_blocks, layer, s1, s2, stats)
    (out,) = residual_rmsnorm(xt, peer_out_t, final_norm_g, F32, return_sum=False)
    return out.reshape(b, s, d)
```

```python
import functools
import math

import jax
import jax.numpy as jnp
from jax import lax
from jax.experimental import pallas as pl
from jax.experimental.pallas import tpu as pltpu

F32 = jnp.float32
BF16 = jnp.bfloat16

HEAD_DIM = 128
ROPE_THETA = 10000.0
RMS_EPS = 1e-6
NEG_BIG = -1e30

SB_HEADS = 8
DIL_GROUPS = ((128, 1), (512, 4), (2048, 16))
DIL_HEADS_PER_GROUP = 4
DIFF_HEADS = 8
DIFF_DIM = 64
N_BRANCH = 3

PEER_HEADS = 8
PEER_NKEYS = 128
PEER_HALF_QDIM = 128
PEER_TOPK = 16
PEER_CANDIDATES = 16 + 7 * 8 + 8

LANES = 128
VMEM_LIMIT = 56 * 1024 * 1024


def _params(sem, vmem=VMEM_LIMIT):
    return pltpu.CompilerParams(dimension_semantics=sem, vmem_limit_bytes=vmem)


def _rmsnorm_kernel(x_ref, g_ref, o_ref):
    x = x_ref[...]
    ms = jnp.mean(x * x, axis=-1, keepdims=True)
    o_ref[...] = (x * lax.rsqrt(ms + RMS_EPS) * g_ref[...]).astype(o_ref.dtype)


def _rmsnorm_both_kernel(x_ref, g_ref, o_ref, ot_ref):
    x = x_ref[...]
    ms = jnp.mean(x * x, axis=-1, keepdims=True)
    y = x * lax.rsqrt(ms + RMS_EPS) * g_ref[...]
    o_ref[...] = y.astype(o_ref.dtype)
    ot_ref[...] = y.T.astype(ot_ref.dtype)


def _residual_rmsnorm_kernel(x_ref, dt_ref, g_ref, o_ref, *sum_ref):
    x = x_ref[...] + dt_ref[...].T
    for ref in sum_ref:
        ref[...] = x
    ms = jnp.mean(x * x, axis=-1, keepdims=True)
    o_ref[...] = (x * lax.rsqrt(ms + RMS_EPS) * g_ref[...]).astype(o_ref.dtype)


def residual_rmsnorm(x, delta_t, g, out_dtype, return_sum, rows=256):
    t, d = x.shape
    rows = min(rows, t)
    row_spec = pl.BlockSpec((rows, d), lambda i: (i, 0))
    out_specs, out_shape = [row_spec], [jax.ShapeDtypeStruct((t, d), out_dtype)]
    if return_sum:
        out_specs.append(row_spec)
        out_shape.append(jax.ShapeDtypeStruct((t, d), x.dtype))
    return pl.pallas_call(
        _residual_rmsnorm_kernel,
        grid=(t // rows,),
        in_specs=[row_spec, pl.BlockSpec((d, rows), lambda i: (0, i)), pl.BlockSpec((1, d), lambda i: (0, 0))],
        out_specs=out_specs,
        out_shape=out_shape,
        compiler_params=_params(("parallel",)),
        name="residual_rmsnorm",
    )(x, delta_t, g.reshape(1, d))


def rmsnorm(x, g, out_dtype, rows=256, with_transpose=False):
    t, d = x.shape
    rows = min(rows, t)
    row_spec = pl.BlockSpec((rows, d), lambda i: (i, 0))
    out_specs, out_shape = row_spec, jax.ShapeDtypeStruct((t, d), out_dtype)
    if with_transpose:
        out_specs = [row_spec, pl.BlockSpec((d, rows), lambda i: (0, i))]
        out_shape = [out_shape, jax.ShapeDtypeStruct((d, t), out_dtype)]
    return pl.pallas_call(
        _rmsnorm_both_kernel if with_transpose else _rmsnorm_kernel,
        grid=(t // rows,),
        in_specs=[row_spec, pl.BlockSpec((1, d), lambda i: (0, 0))],
        out_specs=out_specs,
        out_shape=out_shape,
        compiler_params=_params(("parallel",)),
        name="rmsnorm_transposed" if with_transpose else "rmsnorm",
    )(x, g.reshape(1, d))


def _rope_tables(seq_len, dim):
    inv_freq = 1.0 / (ROPE_THETA ** (jnp.arange(0, dim, 2, dtype=F32) / dim))
    ang = jnp.arange(seq_len, dtype=F32)[:, None] * inv_freq[None, :]
    ang = jnp.concatenate([ang, ang], axis=-1)
    return jnp.cos(ang), jnp.sin(ang)


def _rope_operands(seq_len):
    cos_h, sin_h = _rope_tables(seq_len, HEAD_DIM)
    lane = jnp.arange(LANES)
    sin_h_signed = jnp.where(lane < HEAD_DIM // 2, -sin_h, sin_h)
    cos_d, sin_d = _rope_tables(seq_len, DIFF_DIM)
    cos_d2 = jnp.concatenate([cos_d, cos_d], axis=-1)
    sin_d2 = jnp.concatenate([sin_d, sin_d], axis=-1)
    low = (lane % DIFF_DIM) < DIFF_DIM // 2
    sin_d_low = jnp.where(low, -sin_d2, 0.0)
    sin_d_high = jnp.where(low, 0.0, sin_d2)
    return cos_h, sin_h_signed, cos_d2, sin_d_low, sin_d_high


def _qkv_kernel(a_ref, w_ref, cos_h, sin_h, cos_d, sin_dl, sin_dh, o_ref, *, rope128_blocks, rope64_blocks):
    j = pl.program_id(1)
    acc = jnp.dot(a_ref[...], w_ref[...], preferred_element_type=F32)
    n_chunks = acc.shape[1] // LANES
    in128 = (j >= rope128_blocks[0]) & (j < rope128_blocks[1])
    in64 = (j >= rope64_blocks[0]) & (j < rope64_blocks[1])

    @pl.when(in128)
    def _():
        for c in range(n_chunks):
            x = acc[:, c * LANES:(c + 1) * LANES]
            y = x * cos_h[...] + pltpu.roll(x, HEAD_DIM // 2, 1) * sin_h[...]
            o_ref[:, c * LANES:(c + 1) * LANES] = y.astype(o_ref.dtype)

    @pl.when(in64)
    def _():
        for c in range(n_chunks):
            x = acc[:, c * LANES:(c + 1) * LANES]
            y = (x * cos_d[...] + pltpu.roll(x, LANES - DIFF_DIM // 2, 1) * sin_dl[...]
                 + pltpu.roll(x, DIFF_DIM // 2, 1) * sin_dh[...])
            o_ref[:, c * LANES:(c + 1) * LANES] = y.astype(o_ref.dtype)

    @pl.when(jnp.logical_not(in128 | in64))
    def _():
        o_ref[...] = acc.astype(o_ref.dtype)


def qkv_projection(h, w, layer, rope_ops, seq_len, rope128_cols, rope64_cols, tm=1024, tn=512):
    m, k = h.shape
    n = w.shape[2]
    tm = min(tm, seq_len)
    assert seq_len % tm == 0 and m % tm == 0 and n % tn == 0
    for lo, hi in (rope128_cols, rope64_cols):
        assert lo % tn == 0 and hi % tn == 0
    seq_blocks = seq_len // tm
    tab_spec = pl.BlockSpec((tm, LANES), lambda i, j: (i % seq_blocks, 0))
    kern = functools.partial(
        _qkv_kernel,
        rope128_blocks=(rope128_cols[0] // tn, rope128_cols[1] // tn),
        rope64_blocks=(rope64_cols[0] // tn, rope64_cols[1] // tn))
    return pl.pallas_call(
        kern,
        grid=(m // tm, n // tn),
        in_specs=[pl.BlockSpec((tm, k), lambda i, j: (i, 0)),
                  pl.BlockSpec((None, k, tn), lambda i, j: (layer, 0, j)),
                  tab_spec, tab_spec, tab_spec, tab_spec, tab_spec],
        out_specs=pl.BlockSpec((tm, tn), lambda i, j: (i, j)),
        out_shape=jax.ShapeDtypeStruct((m, n), BF16),
        compiler_params=_params(("parallel", "arbitrary")),
        name="qkv_projection",
    )(h, w, *rope_ops)


def _matmul_kernel(a_ref, w_ref, o_ref):
    o_ref[...] = jnp.dot(a_ref[...], w_ref[...], preferred_element_type=F32).astype(o_ref.dtype)


def _matmul_residual_kernel(a_ref, w_ref, r_ref, o_ref):
    o_ref[...] = r_ref[...] + jnp.dot(a_ref[...], w_ref[...], preferred_element_type=F32)


def matmul(a, w, layer, out_dtype, residual=None, tm=1024, tn=512):
    m, k = a.shape
    n = w.shape[2]
    tm, tn = min(tm, m), min(tn, n)
    assert m % tm == 0 and n % tn == 0
    in_specs = [pl.BlockSpec((tm, k), lambda i, j: (i, 0)),
                pl.BlockSpec((None, k, tn), lambda i, j: (layer, 0, j))]
    args = [a, w]
    kern = _matmul_kernel
    if residual is not None:
        in_specs.append(pl.BlockSpec((tm, tn), lambda i, j: (i, j)))
        args.append(residual)
        kern = _matmul_residual_kernel
    return pl.pallas_call(
        kern,
        grid=(m // tm, n // tn),
        in_specs=in_specs,
        out_specs=pl.BlockSpec((tm, tn), lambda i, j: (i, j)),
        out_shape=jax.ShapeDtypeStruct((m, n), out_dtype),
        compiler_params=_params(("parallel", "arbitrary")),
        name="matmul_residual" if residual is not None else "matmul",
    )(*args)


LOG2_E = math.log2(math.e)
SIGN_BIT = 0x80000000
SB_SUFFIX_BLOCK = 256


def _softplus2(u):
    neg_abs = lax.bitcast_convert_type(lax.bitcast_convert_type(u, jnp.uint32) | jnp.uint32(SIGN_BIT), F32)
    return jnp.maximum(u, 0.0) + jnp.log2(1.0 + jnp.exp2(neg_abs))


def _sb_kernel(suffix_ref, q_ref, k_ref, v_ref, o_ref, *, tq, heads, scale2):
    i = pl.program_id(2)
    tk = 2 * tq
    n_sub = tk // SB_SUFFIX_BLOCK
    suffix = suffix_ref[...]
    row = lax.broadcasted_iota(jnp.int32, (tq, tk), 0)
    col = lax.broadcasted_iota(jnp.int32, (tq, tk), 1)
    kd = lax.shift_right_logical(i, 1)
    before = col < row + (i - 2 * kd) * tq
    dims = (((1,), (1,)), ((), ()))
    qs = [q_ref[:, h * HEAD_DIM:(h + 1) * HEAD_DIM] for h in range(heads)]

    def block(kb, state, diagonal):
        carries = [st[0] for st in state]
        accs = [st[1] for st in state]
        for j in reversed(range(n_sub)):
            start = pl.multiple_of(kb * tk + j * SB_SUFFIX_BLOCK, SB_SUFFIX_BLOCK)
            cols = slice(j * SB_SUFFIX_BLOCK, (j + 1) * SB_SUFFIX_BLOCK)
            for h in range(heads):
                sl = slice(h * HEAD_DIM, (h + 1) * HEAD_DIM)
                k = k_ref[pl.ds(start, SB_SUFFIX_BLOCK), sl]
                v = v_ref[pl.ds(start, SB_SUFFIX_BLOCK), sl]
                z = lax.dot_general(qs[h], k, dims, preferred_element_type=F32) * scale2
                sp = _softplus2(z)
                spm = jnp.where(before[:, cols], sp, 0.0) if diagonal else sp
                later = jnp.dot(spm.astype(BF16), suffix, preferred_element_type=F32) + carries[h]
                w = jnp.exp2(z - sp - later)
                if diagonal:
                    w = jnp.where(before[:, cols], w, 0.0)
                accs[h] = accs[h] + jnp.dot(w.astype(BF16), v, preferred_element_type=F32)
                carries[h] = carries[h] + jnp.sum(spm, axis=1, keepdims=True)
        return tuple(zip(carries, accs))

    state = tuple((jnp.zeros((tq, 1), F32), jnp.zeros((tq, HEAD_DIM), F32)) for _ in range(heads))
    state = block(kd, state, True)
    state = lax.fori_loop(0, kd, lambda n, st: block(kd - 1 - n, st, False), state)
    for h in range(heads):
        o_ref[:, h * HEAD_DIM:(h + 1) * HEAD_DIM] = state[h][1].astype(o_ref.dtype)


def stick_breaking_attention(proj, seq_len, col0, tq=512, heads=4):
    b = proj.shape[0]
    tq = min(tq, seq_len // 2)
    hw = heads * HEAD_DIM
    assert seq_len % (2 * tq) == 0 and SB_HEADS % heads == 0 and col0 % hw == 0
    c0 = col0 // hw
    per = SB_HEADS // heads
    assert (2 * tq) % SB_SUFFIX_BLOCK == 0
    idx = jnp.arange(SB_SUFFIX_BLOCK)
    suffix = (idx[:, None] > idx[None, :]).astype(BF16)
    kern = functools.partial(_sb_kernel, tq=tq, heads=heads, scale2=HEAD_DIM ** -0.5 * LOG2_E)
    return pl.pallas_call(
        kern,
        grid=(b, per, seq_len // tq),
        in_specs=[pl.BlockSpec((SB_SUFFIX_BLOCK, SB_SUFFIX_BLOCK), lambda bi, h, i: (0, 0)),
                  pl.BlockSpec((None, tq, hw), lambda bi, h, i: (bi, i, c0 + h)),
                  pl.BlockSpec((None, seq_len, hw), lambda bi, h, i: (bi, 0, c0 + per + h)),
                  pl.BlockSpec((None, seq_len, hw), lambda bi, h, i: (bi, 0, c0 + 2 * per + h))],
        out_specs=pl.BlockSpec((None, tq, hw), lambda bi, h, i: (bi, i, h)),
        out_shape=jax.ShapeDtypeStruct((b, seq_len, SB_HEADS * HEAD_DIM), BF16),
        compiler_params=_params(("parallel", "parallel", "arbitrary")),
        name="stick_breaking_attention",
    )(suffix, proj, proj, proj)


def _dil_kernel(q_ref, kp_ref, kc_ref, vp_ref, vc_ref, o_ref, lse_ref, *, tq, scale):
    i = pl.program_id(2)
    row = lax.broadcasted_iota(jnp.int32, (tq, tq), 0)
    col = lax.broadcasted_iota(jnp.int32, (tq, tq), 1)
    cur_ok = col <= row
    prev_ok = col >= row + jnp.where(i > 0, 0, tq)
    dims = (((1,), (1,)), ((), ()))
    for h in range(DIL_HEADS_PER_GROUP):
        sl = slice(h * HEAD_DIM, (h + 1) * HEAD_DIM)
        q = q_ref[:, sl]
        s_cur = lax.dot_general(q, kc_ref[:, sl], dims, preferred_element_type=F32) * scale
        s_prev = lax.dot_general(q, kp_ref[:, sl], dims, preferred_element_type=F32) * scale
        s_cur = jnp.where(cur_ok, s_cur, NEG_BIG)
        s_prev = jnp.where(prev_ok, s_prev, NEG_BIG)
        m = jnp.maximum(jnp.max(s_cur, axis=1, keepdims=True), jnp.max(s_prev, axis=1, keepdims=True))
        p_cur = jnp.exp(s_cur - m)
        p_prev = jnp.exp(s_prev - m)
        l = jnp.sum(p_cur, axis=1, keepdims=True) + jnp.sum(p_prev, axis=1, keepdims=True)
        o = (jnp.dot(p_cur.astype(BF16), vc_ref[:, sl], preferred_element_type=F32)
             + jnp.dot(p_prev.astype(BF16), vp_ref[:, sl], preferred_element_type=F32))
        o_ref[:, sl] = o / l
        lse_ref[:, sl] = jnp.broadcast_to(m + jnp.log(l), (tq, HEAD_DIM))


def _regroup_kernel(q_ref, k_ref, v_ref, qo_ref, ko_ref, vo_ref, scr, *, dilation):
    rows, gw = q_ref.shape
    n = rows // dilation
    for src, dst in ((q_ref, qo_ref), (k_ref, ko_ref), (v_ref, vo_ref)):
        for j in range(gw // LANES):
            scr[j] = src[:, j * LANES:(j + 1) * LANES].astype(F32)
        for c in range(dilation):
            for j in range(gw // LANES):
                dst[:, c * gw + j * LANES:c * gw + (j + 1) * LANES] = (
                    scr[j, pl.ds(c, n, stride=dilation), :].astype(dst.dtype))


def dilated_regroup(proj, dilation, q_col, k_col, v_col, rows=512):
    t, cols = proj.shape
    gw = DIL_HEADS_PER_GROUP * HEAD_DIM
    rows = min(rows, t)
    assert t % rows == 0 and rows % (16 * dilation) == 0
    in_specs = [pl.BlockSpec((rows, gw), functools.partial(lambda i, blk: (i, blk), blk=col // gw))
                for col in (q_col, k_col, v_col)]
    out_spec = pl.BlockSpec((rows // dilation, dilation * gw), lambda i: (i, 0))
    out_sds = jax.ShapeDtypeStruct((t // dilation, dilation * gw), proj.dtype)
    return pl.pallas_call(
        functools.partial(_regroup_kernel, dilation=dilation),
        grid=(t // rows,),
        in_specs=in_specs,
        out_specs=[out_spec] * 3,
        out_shape=[out_sds] * 3,
        scratch_shapes=[pltpu.VMEM((gw // LANES, rows, LANES), F32)],
        compiler_params=_params(("parallel",)),
        name=f"dilated_regroup_r{dilation}",
    )(proj, proj, proj)


def dilated_group_attention(q_arr, k_arr, v_arr, batch, seq_len, group, dilation, q_col, k_col, v_col):
    gw = DIL_HEADS_PER_GROUP * HEAD_DIM
    tq = DIL_GROUPS[group][0] // dilation
    sub_len = seq_len // dilation
    assert sub_len % tq == 0
    views, blocks, per_row = [], [], []
    for arr, col in ((q_arr, q_col), (k_arr, k_col), (v_arr, v_col)):
        width = arr.shape[1] // dilation
        assert width % gw == 0 and col % gw == 0
        views.append(arr.reshape(batch, sub_len, dilation * width))
        blocks.append(col // gw)
        per_row.append(width // gw)
    prev = lambda i: jnp.maximum(i - 1, 0)

    def spec(which, row_of):
        return pl.BlockSpec((None, tq, gw),
                            lambda bi, c, i: (bi, row_of(i), c * per_row[which] + blocks[which]))

    out_sds = jax.ShapeDtypeStruct((batch, sub_len, dilation * gw), F32)
    out_spec = pl.BlockSpec((None, tq, gw), lambda bi, c, i: (bi, i, c))
    kern = functools.partial(_dil_kernel, tq=tq, scale=HEAD_DIM ** -0.5)
    o, lse = pl.pallas_call(
        kern,
        grid=(batch, dilation, sub_len // tq),
        in_specs=[spec(0, lambda i: i), spec(1, prev), spec(1, lambda i: i), spec(2, prev), spec(2, lambda i: i)],
        out_specs=[out_spec, out_spec],
        out_shape=[out_sds, out_sds],
        compiler_params=_params(("parallel", "parallel", "arbitrary")),
        name=f"dilated_attention_g{group}",
    )(views[0], views[1], views[1], views[2], views[2])
    return o.reshape(batch * sub_len, dilation * gw), lse.reshape(batch * sub_len, dilation * gw)


def _dil_merge_kernel(*refs, dilations):
    n = len(dilations)
    o_refs, l_refs, out_ref, scratch = refs[:n], refs[n:2 * n], refs[2 * n], refs[2 * n + 1:]
    rows, gw = out_ref.shape
    outs, lses = [], []
    for g, dilation in enumerate(dilations):
        vals = []
        for src, scr in ((o_refs[g], scratch[2 * g]), (l_refs[g], scratch[2 * g + 1])):
            if dilation == 1:
                vals.append(src[...])
            else:
                for c in range(dilation):
                    for j in range(gw // LANES):
                        scr[j, pl.ds(c, rows // dilation, stride=dilation), :] = (
                            src[:, c * gw + j * LANES:c * gw + (j + 1) * LANES])
                vals.append(jnp.concatenate([scr[j] for j in range(gw // LANES)], axis=1))
        outs.append(vals[0])
        lses.append(vals[1])
    m = functools.reduce(jnp.maximum, lses)
    es = [jnp.exp(l - m) for l in lses]
    num = functools.reduce(lambda a, b: a + b, [e * o for e, o in zip(es, outs)])
    den = functools.reduce(lambda a, b: a + b, es)
    out_ref[...] = (num / den).astype(out_ref.dtype)


def dilated_merge(outs, lses, dilations, rows=512):
    gw = DIL_HEADS_PER_GROUP * HEAD_DIM
    t = outs[0].shape[0] * dilations[0]
    rows = min(rows, t)
    specs = [pl.BlockSpec((rows // r, r * gw), lambda i: (i, 0)) for r in dilations]
    return pl.pallas_call(
        functools.partial(_dil_merge_kernel, dilations=tuple(dilations)),
        grid=(t // rows,),
        in_specs=specs + specs,
        out_specs=pl.BlockSpec((rows, gw), lambda i: (i, 0)),
        out_shape=jax.ShapeDtypeStruct((t, gw), BF16),
        scratch_shapes=[pltpu.VMEM((gw // LANES, rows, LANES), F32) for _ in range(2 * len(dilations))],
        compiler_params=_params(("parallel",)),
        name="dilated_merge",
    )(*outs, *lses)


def _diff_kernel(lam_ref, g_ref, q_ref, k_ref, v_ref, o_ref, *, tq, heads, key_blocks, scale2, lam_init):
    i = pl.program_id(2)
    tk = key_blocks * tq
    lp = lam_ref[...]
    lam = (jnp.exp(jnp.sum(lp[0:1] * lp[1:2], axis=1, keepdims=True))
           - jnp.exp(jnp.sum(lp[2:3] * lp[3:4], axis=1, keepdims=True)) + lam_init)
    lane = lax.broadcasted_iota(jnp.int32, (tq, LANES), 1)
    qqs = []
    for h in range(heads):
        q = q_ref[:, h * LANES:(h + 1) * LANES].astype(F32)
        qqs.append(jnp.concatenate([jnp.where(lane < DIFF_DIM, q, 0.0), jnp.where(lane >= DIFF_DIM, q, 0.0)],
                                   axis=0).astype(BF16))
    row = lax.broadcasted_iota(jnp.int32, (2 * tq, tk), 0)
    row = jnp.where(row >= tq, row - tq, row)
    col = lax.broadcasted_iota(jnp.int32, (2 * tq, tk), 1)
    kd = i // key_blocks
    causal = col <= row + (i - key_blocks * kd) * tq
    dims = (((1,), (1,)), ((), ()))

    def block(kb, state, diagonal):
        start = pl.multiple_of(kb * tk, tk)
        out = []
        for h in range(heads):
            m, l, acc = state[h]
            sl = slice(h * LANES, (h + 1) * LANES)
            k = k_ref[pl.ds(start, tk), sl]
            v = v_ref[pl.ds(start, tk), sl]
            s = lax.dot_general(qqs[h], k, dims, preferred_element_type=F32) * scale2
            if diagonal:
                s = jnp.where(causal, s, NEG_BIG)
            m_new = jnp.maximum(m, jnp.max(s, axis=1, keepdims=True))
            alpha = jnp.exp2(m - m_new)
            p = jnp.exp2(s - m_new)
            l = alpha * l + jnp.sum(p, axis=1, keepdims=True)
            acc = alpha * acc + jnp.dot(p.astype(BF16), v, preferred_element_type=F32)
            out.append((m_new, l, acc))
        return tuple(out)

    state = tuple((jnp.full((2 * tq, 1), NEG_BIG, F32), jnp.zeros((2 * tq, 1), F32),
                   jnp.zeros((2 * tq, LANES), F32)) for _ in range(heads))
    state = block(kd, state, True)
    state = lax.fori_loop(0, kd, lambda n, st: block(n, st, False), state)
    for h in range(heads):
        _, l, acc = state[h]
        o_all = acc / l
        o = o_all[:tq] - lam * o_all[tq:]
        ms = jnp.mean(o * o, axis=-1, keepdims=True)
        y = o * lax.rsqrt(ms + RMS_EPS) * g_ref[...]
        o_ref[:, h * LANES:(h + 1) * LANES] = (y * (1.0 - lam_init)).astype(o_ref.dtype)


def differential_attention(proj, lam_params, subln_g, seq_len, q_col, k_col, v_col, lam_init, tq=1024, heads=2,
                           key_blocks=1):
    b = proj.shape[0]
    tq = min(tq, seq_len // key_blocks)
    hw = heads * LANES
    assert seq_len % (key_blocks * tq) == 0 and DIFF_HEADS % heads == 0
    assert q_col % hw == 0 and k_col % hw == 0 and v_col % hw == 0
    qb, kb, vb = q_col // hw, k_col // hw, v_col // hw
    kern = functools.partial(_diff_kernel, tq=tq, heads=heads, key_blocks=key_blocks,
                             scale2=DIFF_DIM ** -0.5 * LOG2_E, lam_init=lam_init)
    return pl.pallas_call(
        kern,
        grid=(b, DIFF_HEADS // heads, seq_len // tq),
        in_specs=[pl.BlockSpec((4, DIFF_DIM), lambda bi, h, i: (0, 0)),
                  pl.BlockSpec((1, 2 * DIFF_DIM), lambda bi, h, i: (0, 0)),
                  pl.BlockSpec((None, tq, hw), lambda bi, h, i: (bi, i, qb + h)),
                  pl.BlockSpec((None, seq_len, hw), lambda bi, h, i: (bi, 0, kb + h)),
                  pl.BlockSpec((None, seq_len, hw), lambda bi, h, i: (bi, 0, vb + h))],
        out_specs=pl.BlockSpec((None, tq, hw), lambda bi, h, i: (bi, i, h)),
        out_shape=jax.ShapeDtypeStruct((b, seq_len, DIFF_HEADS * 2 * DIFF_DIM), BF16),
        compiler_params=_params(("parallel", "parallel", "arbitrary")),
        name="differential_attention",
    )(lam_params, subln_g.reshape(1, -1), proj, proj, proj)


def _gate_merge_kernel(h_ref, wg0, wg1, wg2, o0, o1, o2, wb0, wb1, wb2, out_ref):
    h = h_ref[...]
    acc = None
    for wg, o, wb in ((wg0, o0, wb0), (wg1, o1, wb1), (wg2, o2, wb2)):
        gate = 1.0 / (1.0 + jnp.exp(-jnp.dot(h, wg[...], preferred_element_type=F32)))
        term = gate * jnp.dot(o[...], wb[...], preferred_element_type=F32)
        acc = term if acc is None else acc + term
    out_ref[...] = acc.astype(out_ref.dtype)


def gate_merge(h, w_gate, branch_outs, branch_ws, layer, tm=1024, tn=256):
    t, d = h.shape
    tm, tn = min(tm, t), min(tn, d)
    nj = d // tn
    in_specs = [pl.BlockSpec((tm, d), lambda i, j: (i, 0))]
    in_specs += [pl.BlockSpec((None, d, tn), functools.partial(lambda i, j, b: (layer, 0, b * nj + j), b=b))
                 for b in range(N_BRANCH)]
    in_specs += [pl.BlockSpec((tm, o.shape[1]), lambda i, j: (i, 0)) for o in branch_outs]
    in_specs += [pl.BlockSpec((None, w.shape[1], tn), lambda i, j: (layer, 0, j)) for w in branch_ws]
    return pl.pallas_call(
        _gate_merge_kernel,
        grid=(t // tm, nj),
        in_specs=in_specs,
        out_specs=pl.BlockSpec((tm, tn), lambda i, j: (i, j)),
        out_shape=jax.ShapeDtypeStruct((t, d), BF16),
        compiler_params=_params(("parallel", "arbitrary")),
        name="gate_merge",
    )(h, w_gate, w_gate, w_gate, *branch_outs, *branch_ws)


STAT_TAU, STAT_MAX1, STAT_MAX2, STAT_INVZ = 0, 1, 2, 3
STAT_ROWS = 8


def _top_values(x, scr, count):
    for kk in range(count):
        m = jnp.max(x, axis=0, keepdims=True)
        scr[kk:kk + 1, :] = m
        x = jnp.where(x == m, -jnp.inf, x)


def _peer_route_kernel(q_ref, keys_ref, s1_ref, s2_ref, stat_ref, a_scr, b_scr, c_scr, t_scr):
    dims = (((1,), (1,)), ((), ()))
    s1 = lax.dot_general(keys_ref[0], q_ref[:, :PEER_HALF_QDIM], dims, preferred_element_type=F32)
    s2 = lax.dot_general(keys_ref[1], q_ref[:, PEER_HALF_QDIM:], dims, preferred_element_type=F32)
    s1_ref[...] = s1
    s2_ref[...] = s2
    _top_values(s1, a_scr, PEER_TOPK)
    _top_values(s2, b_scr, PEER_TOPK)
    half = PEER_TOPK // 2
    c_scr[0:PEER_TOPK, :] = a_scr[0:1, :] + b_scr[...]
    for ii in range(1, half):
        c_scr[PEER_TOPK + (ii - 1) * half:PEER_TOPK + ii * half, :] = a_scr[ii:ii + 1, :] + b_scr[0:half, :]
    c_scr[PEER_TOPK + (half - 1) * half:PEER_CANDIDATES, :] = a_scr[half:PEER_TOPK, :] + b_scr[0:1, :]
    cand = c_scr[...]
    _top_values(cand, t_scr, PEER_TOPK)
    tau = t_scr[PEER_TOPK - 1:PEER_TOPK, :]
    best = t_scr[0:1, :]
    z = jnp.sum(jnp.where(cand >= tau, jnp.exp(cand - best), 0.0), axis=0, keepdims=True)
    stat_ref[...] = jnp.zeros_like(stat_ref)
    stat_ref[STAT_TAU:STAT_TAU + 1, :] = tau
    stat_ref[STAT_MAX1:STAT_MAX1 + 1, :] = a_scr[0:1, :]
    stat_ref[STAT_MAX2:STAT_MAX2 + 1, :] = b_scr[0:1, :]
    stat_ref[STAT_INVZ:STAT_INVZ + 1, :] = 1.0 / z


def peer_route(q, sub_keys, layer, tb=256):
    t = q.shape[0]
    tb = min(tb, t)
    score_sds = jax.ShapeDtypeStruct((PEER_HEADS, PEER_NKEYS, t), F32)
    score_spec = pl.BlockSpec((None, PEER_NKEYS, tb), lambda i, h: (h, 0, i))
    return pl.pallas_call(
        _peer_route_kernel,
        grid=(t // tb, PEER_HEADS),
        in_specs=[pl.BlockSpec((tb, 2 * PEER_HALF_QDIM), lambda i, h: (i, h)),
                  pl.BlockSpec((None, None, 2, PEER_NKEYS, PEER_HALF_QDIM), lambda i, h: (layer, h, 0, 0, 0))],
        out_specs=[score_spec, score_spec, pl.BlockSpec((None, STAT_ROWS, tb), lambda i, h: (h, 0, i))],
        out_shape=[score_sds, score_sds, jax.ShapeDtypeStruct((PEER_HEADS, STAT_ROWS, t), F32)],
        scratch_shapes=[pltpu.VMEM((PEER_TOPK, tb), F32), pltpu.VMEM((PEER_TOPK, tb), F32),
                        pltpu.VMEM((PEER_CANDIDATES, tb), F32), pltpu.VMEM((PEER_TOPK, tb), F32)],
        compiler_params=_params(("parallel", "arbitrary")),
        name="peer_route",
    )(q, sub_keys)


def _gelu(a):
    return 0.5 * a * (1.0 + lax.erf(a * (2.0 ** -0.5)))


PEER_STAGES = 3
PEER_EXPERT_BLOCK = 512
GATE_KEY_SPLITS = 8


def _peer_dense_kernel(ht_ref, u_ref, vt_ref, s1_ref, s2_ref, stat_ref, out_ref, e2_scr,
                       act_a, act_b, coef_a, coef_b, *, te, n_blocks):
    e = pl.program_id(1)

    @pl.when(e == 0)
    def _():
        out_ref[...] = jnp.zeros_like(out_ref)
        for h in range(PEER_HEADS):
            e2_scr[h] = jnp.exp(s2_ref[h] - stat_ref[h, STAT_MAX2:STAT_MAX2 + 1, :])

    n_sub = te // PEER_NKEYS
    d_model, tb = ht_ref.shape
    gate_block = e - 1

    def stages(buffers, scores=True, gates=True, output=True):
        act_cur, act_prev, coef_cur, coef_prev = buffers
        n_lane = tb // LANES
        n_slices = GATE_KEY_SPLITS * n_lane
        kc = d_model // n_slices
        if scores:
            act_cur[...] = jnp.dot(u_ref[...], ht_ref[...], preferred_element_type=F32)
        s1_rows, e1_rows = [], []
        for sub in range(n_sub if gates else 0):
            i_idx = gate_block * n_sub + sub
            s1_rows.append([s1_ref[h, pl.ds(i_idx, 1), :] for h in range(PEER_HEADS)])
            e1_rows.append([jnp.exp(s1_rows[sub][h] - stat_ref[h, STAT_MAX1:STAT_MAX1 + 1, :])
                            * stat_ref[h, STAT_INVZ:STAT_INVZ + 1, :] for h in range(PEER_HEADS)])
        half = PEER_NKEYS // GATE_KEY_SPLITS
        for r in range(n_slices):
            c, jh = divmod(r, GATE_KEY_SPLITS)
            lanes = slice(c * LANES, (c + 1) * LANES)
            keys = slice(jh * half, (jh + 1) * half)
            tiles = [None] * n_sub
            for h in range(PEER_HEADS if gates else 0):
                s2_tile = s2_ref[h, keys, lanes]
                e2_tile = e2_scr[h, keys, lanes]
                tau = stat_ref[h, STAT_TAU:STAT_TAU + 1, lanes]
                for sub in range(n_sub):
                    term = jnp.where(s2_tile + s1_rows[sub][h][:, lanes] >= tau,
                                     e2_tile * e1_rows[sub][h][:, lanes], 0.0)
                    tiles[sub] = term if tiles[sub] is None else tiles[sub] + term
            for sub in range(n_sub if gates else 0):
                rows = slice(sub * PEER_NKEYS + jh * half, sub * PEER_NKEYS + (jh + 1) * half)
                coef_prev[rows, lanes] = (tiles[sub] * _gelu(act_prev[rows, lanes])).astype(BF16)
            if output:
                chunk = slice(r * kc, (r + 1) * kc)
                out_ref[chunk, :] += jnp.dot(vt_ref[chunk, :], coef_cur[...], preferred_element_type=F32)

    even, odd = (act_a, act_b, coef_a, coef_b), (act_b, act_a, coef_b, coef_a)
    final = n_blocks + PEER_STAGES - 2
    assert final >= PEER_STAGES
    edge_steps = {0: dict(gates=False, output=False), 1: dict(output=False),
                  final - 1: dict(scores=False), final: dict(scores=False, gates=False)}
    for step, flags in edge_steps.items():
        pl.when(e == step)(functools.partial(stages, even if step % 2 == 0 else odd, **flags))
    interior = (e > 1) & (e < final - 1)
    parity = lax.rem(e, 2)
    pl.when(interior & (parity == 0))(functools.partial(stages, even))
    pl.when(interior & (parity == 1))(functools.partial(stages, odd))


def peer_dense(ht, u_tab, vt_blocks, layer, s1, s2, stats, tb=512):
    d, t = ht.shape
    _, n_blocks, _, te = vt_blocks.shape
    tb = min(tb, t)
    assert t % tb == 0 and u_tab.shape[1] == n_blocks * te and te % PEER_NKEYS == 0
    once = pl.Buffered(1)
    tok_spec = pl.BlockSpec((PEER_HEADS, PEER_NKEYS, tb), lambda i, e: (0, 0, i), pipeline_mode=once)
    last = n_blocks - 1
    kern = functools.partial(_peer_dense_kernel, te=te, n_blocks=n_blocks)
    return pl.pallas_call(
        kern,
        grid=(t // tb, n_blocks + PEER_STAGES - 1),
        in_specs=[pl.BlockSpec((d, tb), lambda i, e: (0, i), pipeline_mode=once),
                  pl.BlockSpec((None, te, d), lambda i, e: (layer, jnp.minimum(e, last), 0)),
                  pl.BlockSpec((None, None, d, te), lambda i, e: (layer, jnp.clip(e - 2, 0, last), 0, 0)),
                  tok_spec, tok_spec,
                  pl.BlockSpec((PEER_HEADS, STAT_ROWS, tb), lambda i, e: (0, 0, i), pipeline_mode=once)],
        out_specs=pl.BlockSpec((d, tb), lambda i, e: (0, i)),
        out_shape=jax.ShapeDtypeStruct((d, t), F32),
        scratch_shapes=[pltpu.VMEM((PEER_HEADS, PEER_NKEYS, tb), F32),
                        pltpu.VMEM((te, tb), F32), pltpu.VMEM((te, tb), F32),
                        pltpu.VMEM((te, tb), BF16), pltpu.VMEM((te, tb), BF16)],
        compiler_params=_params(("parallel", "arbitrary")),
        name="peer_dense",
    )(ht, u_tab, vt_blocks, s1, s2, stats)


def kernel(x, attn_norm_g, ffn_norm_g, final_norm_g, w_qkv, w_gate, w_branch_sb, w_branch_dil,
           w_branch_diff, w_out, diff_lambda, diff_subln_g, peer_w_q, peer_sub_keys, peer_u, peer_v):
    b, s, d = x.shape
    t = b * s
    depth = w_qkv.shape[0]
    sb_w = SB_HEADS * HEAD_DIM
    dil_w = DIL_HEADS_PER_GROUP * len(DIL_GROUPS) * HEAD_DIM
    diff_w = DIFF_HEADS * 2 * DIFF_DIM
    dl_q, dl_k, dl_v = 3 * sb_w, 3 * sb_w + dil_w, 3 * sb_w + 2 * dil_w
    df_q = 3 * sb_w + 3 * dil_w
    df_k, df_v = df_q + diff_w, df_q + 2 * diff_w
    qkv_cols = df_v + diff_w
    rope_ops = _rope_operands(s)
    w_qkv_b, w_gate_b, w_out_b, w_pq_b = (w.astype(BF16) for w in (w_qkv, w_gate, w_out, peer_w_q))
    w_branch_b = tuple(w.astype(BF16) for w in (w_branch_sb, w_branch_dil, w_branch_diff))
    keys_b, u_b = peer_sub_keys.astype(BF16), peer_u.astype(BF16)
    vt_blocks = peer_v.reshape(depth, -1, PEER_EXPERT_BLOCK, d).transpose(0, 1, 3, 2).astype(BF16)

    xt = x.reshape(t, d)
    peer_out_t = None
    for layer in range(depth):
        if peer_out_t is None:
            h = rmsnorm(xt, attn_norm_g[layer], BF16)
        else:
            h, xt = residual_rmsnorm(xt, peer_out_t, attn_norm_g[layer], BF16, return_sum=True)
        proj = qkv_projection(h, w_qkv_b, layer, rope_ops, s,
                              rope128_cols=(dl_q, dl_v), rope64_cols=(df_q, df_v))
        proj3 = proj.reshape(b, s, qkv_cols)
        o_sb = stick_breaking_attention(proj3, s, 0).reshape(t, sb_w)
        dil = []
        gw = DIL_HEADS_PER_GROUP * HEAD_DIM
        for g, (_, dilation) in enumerate(DIL_GROUPS):
            cols = (dl_q + g * gw, dl_k + g * gw, dl_v + g * gw)
            if dilation == 1:
                dil.append(dilated_group_attention(proj, proj, proj, b, s, g, dilation, *cols))
            else:
                qkv_views = dilated_regroup(proj, dilation, *cols)
                dil.append(dilated_group_attention(*qkv_views, b, s, g, dilation, 0, 0, 0))
        o_dl = dilated_merge([o for o, _ in dil], [l for _, l in dil], [r for _, r in DIL_GROUPS])
        lam_init = 0.8 - 0.6 * math.exp(-0.3 * layer)
        o_df = differential_attention(proj3, diff_lambda[layer], diff_subln_g[layer], s,
                                      df_q, df_k, df_v, lam_init).reshape(t, diff_w)
        merged = gate_merge(h, w_gate_b, (o_sb, o_dl, o_df), w_branch_b, layer)
        xt = matmul(merged, w_out_b, layer, F32, residual=xt)

        h2, h2_t = rmsnorm(xt, ffn_norm_g[layer], BF16, with_transpose=True)
        q = matmul(h2, w_pq_b, layer, BF16)
        s1, s2, stats = peer_route(q, keys_b, layer)
        peer_out_t = peer_dense(h2_t, u_b, vt_blocks, layer, s1, s2, stats)
    (out,) = residual_rmsnorm(xt, peer_out_t, final_norm_g, F32, return_sum=False)
    return out.reshape(b, s, d)
```

```python
import functools
import math

import jax
import jax.numpy as jnp
from jax import lax
from jax.experimental import pallas as pl
from jax.experimental.pallas import tpu as pltpu

F32 = jnp.float32
BF16 = jnp.bfloat16

HEAD_DIM = 128
ROPE_THETA = 10000.0
RMS_EPS = 1e-6
NEG_BIG = -1e30

SB_HEADS = 8
DIL_GROUPS = ((128, 1), (512, 4), (2048, 16))
DIL_HEADS_PER_GROUP = 4
DIFF_HEADS = 8
DIFF_DIM = 64
N_BRANCH = 3

PEER_HEADS = 8
PEER_NKEYS = 128
PEER_HALF_QDIM = 128
PEER_TOPK = 16
PEER_CANDIDATES = 16 + 7 * 8 + 8

LANES = 128
VMEM_LIMIT = 56 * 1024 * 1024


def _params(sem, vmem=VMEM_LIMIT):
    return pltpu.CompilerParams(dimension_semantics=sem, vmem_limit_bytes=vmem)


def _rmsnorm_kernel(x_ref, g_ref, o_ref):
    x = x_ref[...]
    ms = jnp.mean(x * x, axis=-1, keepdims=True)
    o_ref[...] = (x * lax.rsqrt(ms + RMS_EPS) * g_ref[...]).astype(o_ref.dtype)


def _rmsnorm_both_kernel(x_ref, g_ref, o_ref, ot_ref):
    x = x_ref[...]
    ms = jnp.mean(x * x, axis=-1, keepdims=True)
    y = x * lax.rsqrt(ms + RMS_EPS) * g_ref[...]
    o_ref[...] = y.astype(o_ref.dtype)
    ot_ref[...] = y.T.astype(ot_ref.dtype)


def _residual_rmsnorm_kernel(x_ref, dt_ref, g_ref, o_ref, *sum_ref):
    x = x_ref[...] + dt_ref[...].T
    for ref in sum_ref:
        ref[...] = x
    ms = jnp.mean(x * x, axis=-1, keepdims=True)
    o_ref[...] = (x * lax.rsqrt(ms + RMS_EPS) * g_ref[...]).astype(o_ref.dtype)


def residual_rmsnorm(x, delta_t, g, out_dtype, return_sum, rows=256):
    t, d = x.shape
    rows = min(rows, t)
    row_spec = pl.BlockSpec((rows, d), lambda i: (i, 0))
    out_specs, out_shape = [row_spec], [jax.ShapeDtypeStruct((t, d), out_dtype)]
    if return_sum:
        out_specs.append(row_spec)
        out_shape.append(jax.ShapeDtypeStruct((t, d), x.dtype))
    return pl.pallas_call(
        _residual_rmsnorm_kernel,
        grid=(t // rows,),
        in_specs=[row_spec, pl.BlockSpec((d, rows), lambda i: (0, i)), pl.BlockSpec((1, d), lambda i: (0, 0))],
        out_specs=out_specs,
        out_shape=out_shape,
        compiler_params=_params(("parallel",)),
        name="residual_rmsnorm",
    )(x, delta_t, g.reshape(1, d))


def rmsnorm(x, g, out_dtype, rows=256, with_transpose=False):
    t, d = x.shape
    rows = min(rows, t)
    row_spec = pl.BlockSpec((rows, d), lambda i: (i, 0))
    out_specs, out_shape = row_spec, jax.ShapeDtypeStruct((t, d), out_dtype)
    if with_transpose:
        out_specs = [row_spec, pl.BlockSpec((d, rows), lambda i: (0, i))]
        out_shape = [out_shape, jax.ShapeDtypeStruct((d, t), out_dtype)]
    return pl.pallas_call(
        _rmsnorm_both_kernel if with_transpose else _rmsnorm_kernel,
        grid=(t // rows,),
        in_specs=[row_spec, pl.BlockSpec((1, d), lambda i: (0, 0))],
        out_specs=out_specs,
        out_shape=out_shape,
        compiler_params=_params(("parallel",)),
        name="rmsnorm_transposed" if with_transpose else "rmsnorm",
    )(x, g.reshape(1, d))


def _rope_tables(seq_len, dim):
    inv_freq = 1.0 / (ROPE_THETA ** (jnp.arange(0, dim, 2, dtype=F32) / dim))
    ang = jnp.arange(seq_len, dtype=F32)[:, None] * inv_freq[None, :]
    ang = jnp.concatenate([ang, ang], axis=-1)
    return jnp.cos(ang), jnp.sin(ang)


def _rope_operands(seq_len):
    cos_h, sin_h = _rope_tables(seq_len, HEAD_DIM)
    lane = jnp.arange(LANES)
    sin_h_signed = jnp.where(lane < HEAD_DIM // 2, -sin_h, sin_h)
    cos_d, sin_d = _rope_tables(seq_len, DIFF_DIM)
    cos_d2 = jnp.concatenate([cos_d, cos_d], axis=-1)
    sin_d2 = jnp.concatenate([sin_d, sin_d], axis=-1)
    low = (lane % DIFF_DIM) < DIFF_DIM // 2
    sin_d_low = jnp.where(low, -sin_d2, 0.0)
    sin_d_high = jnp.where(low, 0.0, sin_d2)
    return cos_h, sin_h_signed, cos_d2, sin_d_low, sin_d_high


def _qkv_kernel(a_ref, w_ref, cos_h, sin_h, cos_d, sin_dl, sin_dh, o_ref, *, rope128_blocks, rope64_blocks):
    j = pl.program_id(1)
    acc = jnp.dot(a_ref[...], w_ref[...], preferred_element_type=F32)
    n_chunks = acc.shape[1] // LANES
    in128 = (j >= rope128_blocks[0]) & (j < rope128_blocks[1])
    in64 = (j >= rope64_blocks[0]) & (j < rope64_blocks[1])

    @pl.when(in128)
    def _():
        for c in range(n_chunks):
            x = acc[:, c * LANES:(c + 1) * LANES]
            y = x * cos_h[...] + pltpu.roll(x, HEAD_DIM // 2, 1) * sin_h[...]
            o_ref[:, c * LANES:(c + 1) * LANES] = y.astype(o_ref.dtype)

    @pl.when(in64)
    def _():
        for c in range(n_chunks):
            x = acc[:, c * LANES:(c + 1) * LANES]
            y = (x * cos_d[...] + pltpu.roll(x, LANES - DIFF_DIM // 2, 1) * sin_dl[...]
                 + pltpu.roll(x, DIFF_DIM // 2, 1) * sin_dh[...])
            o_ref[:, c * LANES:(c + 1) * LANES] = y.astype(o_ref.dtype)

    @pl.when(jnp.logical_not(in128 | in64))
    def _():
        o_ref[...] = acc.astype(o_ref.dtype)


def qkv_projection(h, w, layer, rope_ops, seq_len, rope128_cols, rope64_cols, tm=1024, tn=512):
    m, k = h.shape
    n = w.shape[2]
    tm = min(tm, seq_len)
    assert seq_len % tm == 0 and m % tm == 0 and n % tn == 0
    for lo, hi in (rope128_cols, rope64_cols):
        assert lo % tn == 0 and hi % tn == 0
    seq_blocks = seq_len // tm
    tab_spec = pl.BlockSpec((tm, LANES), lambda i, j: (i % seq_blocks, 0))
    kern = functools.partial(
        _qkv_kernel,
        rope128_blocks=(rope128_cols[0] // tn, rope128_cols[1] // tn),
        rope64_blocks=(rope64_cols[0] // tn, rope64_cols[1] // tn))
    return pl.pallas_call(
        kern,
        grid=(m // tm, n // tn),
        in_specs=[pl.BlockSpec((tm, k), lambda i, j: (i, 0)),
                  pl.BlockSpec((None, k, tn), lambda i, j: (layer, 0, j)),
                  tab_spec, tab_spec, tab_spec, tab_spec, tab_spec],
        out_specs=pl.BlockSpec((tm, tn), lambda i, j: (i, j)),
        out_shape=jax.ShapeDtypeStruct((m, n), BF16),
        compiler_params=_params(("parallel", "arbitrary")),
        name="qkv_projection",
    )(h, w, *rope_ops)


def _matmul_kernel(a_ref, w_ref, o_ref):
    o_ref[...] = jnp.dot(a_ref[...], w_ref[...], preferred_element_type=F32).astype(o_ref.dtype)


def _matmul_residual_kernel(a_ref, w_ref, r_ref, o_ref):
    o_ref[...] = r_ref[...] + jnp.dot(a_ref[...], w_ref[...], preferred_element_type=F32)


def matmul(a, w, layer, out_dtype, residual=None, tm=1024, tn=512):
    m, k = a.shape
    n = w.shape[2]
    tm, tn = min(tm, m), min(tn, n)
    assert m % tm == 0 and n % tn == 0
    in_specs = [pl.BlockSpec((tm, k), lambda i, j: (i, 0)),
                pl.BlockSpec((None, k, tn), lambda i, j: (layer, 0, j))]
    args = [a, w]
    kern = _matmul_kernel
    if residual is not None:
        in_specs.append(pl.BlockSpec((tm, tn), lambda i, j: (i, j)))
        args.append(residual)
        kern = _matmul_residual_kernel
    return pl.pallas_call(
        kern,
        grid=(m // tm, n // tn),
        in_specs=in_specs,
        out_specs=pl.BlockSpec((tm, tn), lambda i, j: (i, j)),
        out_shape=jax.ShapeDtypeStruct((m, n), out_dtype),
        compiler_params=_params(("parallel", "arbitrary")),
        name="matmul_residual" if residual is not None else "matmul",
    )(*args)


LOG2_E = math.log2(math.e)
SIGN_BIT = 0x80000000
SB_SUFFIX_BLOCK = 256


def _softplus2(u):
    neg_abs = lax.bitcast_convert_type(lax.bitcast_convert_type(u, jnp.uint32) | jnp.uint32(SIGN_BIT), F32)
    return jnp.maximum(u, 0.0) + jnp.log2(1.0 + jnp.exp2(neg_abs))


def _sb_kernel(suffix_ref, q_ref, k_ref, v_ref, o_ref, *, tq, heads, scale2):
    i = pl.program_id(2)
    tk = 2 * tq
    suffix = suffix_ref[...]
    row = lax.broadcasted_iota(jnp.int32, (tq, tq), 0)
    col = lax.broadcasted_iota(jnp.int32, (tq, tq), 1)
    before = col < row
    dims = (((1,), (1,)), ((), ()))
    qs = [q_ref[:, h * HEAD_DIM:(h + 1) * HEAD_DIM] for h in range(heads)]

    def block(first_key, n_keys, state, diagonal):
        carries = [st[0] for st in state]
        accs = [st[1] for st in state]
        for j in reversed(range(n_keys // SB_SUFFIX_BLOCK)):
            start = pl.multiple_of(first_key + j * SB_SUFFIX_BLOCK, SB_SUFFIX_BLOCK)
            cols = slice(j * SB_SUFFIX_BLOCK, (j + 1) * SB_SUFFIX_BLOCK)
            for h in range(heads):
                sl = slice(h * HEAD_DIM, (h + 1) * HEAD_DIM)
                k = k_ref[pl.ds(start, SB_SUFFIX_BLOCK), sl]
                v = v_ref[pl.ds(start, SB_SUFFIX_BLOCK), sl]
                z = lax.dot_general(qs[h], k, dims, preferred_element_type=F32) * scale2
                sp = _softplus2(z)
                spm = jnp.where(before[:, cols], sp, 0.0) if diagonal else sp
                later = jnp.dot(spm.astype(BF16), suffix, preferred_element_type=F32) + carries[h]
                w = jnp.exp2(z - sp - later)
                if diagonal:
                    w = jnp.where(before[:, cols], w, 0.0)
                accs[h] = accs[h] + jnp.dot(w.astype(BF16), v, preferred_element_type=F32)
                carries[h] = carries[h] + jnp.sum(spm, axis=1, keepdims=True)
        return tuple(zip(carries, accs))

    state = tuple((jnp.zeros((tq, 1), F32), jnp.zeros((tq, HEAD_DIM), F32)) for _ in range(heads))
    state = block(i * tq, tq, state, True)
    state = lax.fori_loop(0, i & 1, lambda n, st: block((i - 1) * tq, tq, st, False), state)
    pairs = lax.shift_right_logical(i, 1)
    state = lax.fori_loop(0, pairs, lambda n, st: block((pairs - 1 - n) * tk, tk, st, False), state)
    for h in range(heads):
        o_ref[:, h * HEAD_DIM:(h + 1) * HEAD_DIM] = state[h][1].astype(o_ref.dtype)


def stick_breaking_attention(proj, seq_len, col0, tq=512, heads=4):
    b = proj.shape[0]
    tq = min(tq, seq_len // 2)
    hw = heads * HEAD_DIM
    assert seq_len % (2 * tq) == 0 and SB_HEADS % heads == 0 and col0 % hw == 0
    c0 = col0 // hw
    per = SB_HEADS // heads
    assert tq % SB_SUFFIX_BLOCK == 0
    idx = jnp.arange(SB_SUFFIX_BLOCK)
    suffix = (idx[:, None] > idx[None, :]).astype(BF16)
    kern = functools.partial(_sb_kernel, tq=tq, heads=heads, scale2=HEAD_DIM ** -0.5 * LOG2_E)
    return pl.pallas_call(
        kern,
        grid=(b, per, seq_len // tq),
        in_specs=[pl.BlockSpec((SB_SUFFIX_BLOCK, SB_SUFFIX_BLOCK), lambda bi, h, i: (0, 0)),
                  pl.BlockSpec((None, tq, hw), lambda bi, h, i: (bi, i, c0 + h)),
                  pl.BlockSpec((None, seq_len, hw), lambda bi, h, i: (bi, 0, c0 + per + h)),
                  pl.BlockSpec((None, seq_len, hw), lambda bi, h, i: (bi, 0, c0 + 2 * per + h))],
        out_specs=pl.BlockSpec((None, tq, hw), lambda bi, h, i: (bi, i, h)),
        out_shape=jax.ShapeDtypeStruct((b, seq_len, SB_HEADS * HEAD_DIM), BF16),
        compiler_params=_params(("parallel", "parallel", "arbitrary")),
        name="stick_breaking_attention",
    )(suffix, proj, proj, proj)


def _dil_kernel(q_ref, kp_ref, kc_ref, vp_ref, vc_ref, o_ref, lse_ref, *, tq, scale):
    i = pl.program_id(2)
    row = lax.broadcasted_iota(jnp.int32, (tq, tq), 0)
    col = lax.broadcasted_iota(jnp.int32, (tq, tq), 1)
    cur_ok = col <= row
    prev_ok = col >= row + jnp.where(i > 0, 0, tq)
    dims = (((1,), (1,)), ((), ()))
    for h in range(DIL_HEADS_PER_GROUP):
        sl = slice(h * HEAD_DIM, (h + 1) * HEAD_DIM)
        q = q_ref[:, sl]
        s_cur = lax.dot_general(q, kc_ref[:, sl], dims, preferred_element_type=F32) * scale
        s_prev = lax.dot_general(q, kp_ref[:, sl], dims, preferred_element_type=F32) * scale
        s_cur = jnp.where(cur_ok, s_cur, NEG_BIG)
        s_prev = jnp.where(prev_ok, s_prev, NEG_BIG)
        m = jnp.maximum(jnp.max(s_cur, axis=1, keepdims=True), jnp.max(s_prev, axis=1, keepdims=True))
        p_cur = jnp.exp(s_cur - m)
        p_prev = jnp.exp(s_prev - m)
        l = jnp.sum(p_cur, axis=1, keepdims=True) + jnp.sum(p_prev, axis=1, keepdims=True)
        o = (jnp.dot(p_cur.astype(BF16), vc_ref[:, sl], preferred_element_type=F32)
             + jnp.dot(p_prev.astype(BF16), vp_ref[:, sl], preferred_element_type=F32))
        o_ref[:, sl] = o / l
        lse_ref[:, sl] = jnp.broadcast_to(m + jnp.log(l), (tq, HEAD_DIM))


def _regroup_kernel(q_ref, k_ref, v_ref, qo_ref, ko_ref, vo_ref, scr, *, dilation):
    rows, gw = q_ref.shape
    n = rows // dilation
    for src, dst in ((q_ref, qo_ref), (k_ref, ko_ref), (v_ref, vo_ref)):
        for j in range(gw // LANES):
            scr[j] = src[:, j * LANES:(j + 1) * LANES].astype(F32)
        for c in range(dilation):
            for j in range(gw // LANES):
                dst[:, c * gw + j * LANES:c * gw + (j + 1) * LANES] = (
                    scr[j, pl.ds(c, n, stride=dilation), :].astype(dst.dtype))


def dilated_regroup(proj, dilation, q_col, k_col, v_col, rows=512):
    t, cols = proj.shape
    gw = DIL_HEADS_PER_GROUP * HEAD_DIM
    rows = min(rows, t)
    assert t % rows == 0 and rows % (16 * dilation) == 0
    in_specs = [pl.BlockSpec((rows, gw), functools.partial(lambda i, blk: (i, blk), blk=col // gw))
                for col in (q_col, k_col, v_col)]
    out_spec = pl.BlockSpec((rows // dilation, dilation * gw), lambda i: (i, 0))
    out_sds = jax.ShapeDtypeStruct((t // dilation, dilation * gw), proj.dtype)
    return pl.pallas_call(
        functools.partial(_regroup_kernel, dilation=dilation),
        grid=(t // rows,),
        in_specs=in_specs,
        out_specs=[out_spec] * 3,
        out_shape=[out_sds] * 3,
        scratch_shapes=[pltpu.VMEM((gw // LANES, rows, LANES), F32)],
        compiler_params=_params(("parallel",)),
        name=f"dilated_regroup_r{dilation}",
    )(proj, proj, proj)


def dilated_group_attention(q_arr, k_arr, v_arr, batch, seq_len, group, dilation, q_col, k_col, v_col):
    gw = DIL_HEADS_PER_GROUP * HEAD_DIM
    tq = DIL_GROUPS[group][0] // dilation
    sub_len = seq_len // dilation
    assert sub_len % tq == 0
    views, blocks, per_row = [], [], []
    for arr, col in ((q_arr, q_col), (k_arr, k_col), (v_arr, v_col)):
        width = arr.shape[1] // dilation
        assert width % gw == 0 and col % gw == 0
        views.append(arr.reshape(batch, sub_len, dilation * width))
        blocks.append(col // gw)
        per_row.append(width // gw)
    prev = lambda i: jnp.maximum(i - 1, 0)

    def spec(which, row_of):
        return pl.BlockSpec((None, tq, gw),
                            lambda bi, c, i: (bi, row_of(i), c * per_row[which] + blocks[which]))

    out_sds = jax.ShapeDtypeStruct((batch, sub_len, dilation * gw), F32)
    out_spec = pl.BlockSpec((None, tq, gw), lambda bi, c, i: (bi, i, c))
    kern = functools.partial(_dil_kernel, tq=tq, scale=HEAD_DIM ** -0.5)
    o, lse = pl.pallas_call(
        kern,
        grid=(batch, dilation, sub_len // tq),
        in_specs=[spec(0, lambda i: i), spec(1, prev), spec(1, lambda i: i), spec(2, prev), spec(2, lambda i: i)],
        out_specs=[out_spec, out_spec],
        out_shape=[out_sds, out_sds],
        compiler_params=_params(("parallel", "parallel", "arbitrary")),
        name=f"dilated_attention_g{group}",
    )(views[0], views[1], views[1], views[2], views[2])
    return o.reshape(batch * sub_len, dilation * gw), lse.reshape(batch * sub_len, dilation * gw)


def _dil_merge_kernel(*refs, dilations):
    n = len(dilations)
    o_refs, l_refs, out_ref, scratch = refs[:n], refs[n:2 * n], refs[2 * n], refs[2 * n + 1:]
    rows, gw = out_ref.shape
    outs, lses = [], []
    for g, dilation in enumerate(dilations):
        vals = []
        for src, scr in ((o_refs[g], scratch[2 * g]), (l_refs[g], scratch[2 * g + 1])):
            if dilation == 1:
                vals.append(src[...])
            else:
                for c in range(dilation):
                    for j in range(gw // LANES):
                        scr[j, pl.ds(c, rows // dilation, stride=dilation), :] = (
                            src[:, c * gw + j * LANES:c * gw + (j + 1) * LANES])
                vals.append(jnp.concatenate([scr[j] for j in range(gw // LANES)], axis=1))
        outs.append(vals[0])
        lses.append(vals[1])
    m = functools.reduce(jnp.maximum, lses)
    es = [jnp.exp(l - m) for l in lses]
    num = functools.reduce(lambda a, b: a + b, [e * o for e, o in zip(es, outs)])
    den = functools.reduce(lambda a, b: a + b, es)
    out_ref[...] = (num / den).astype(out_ref.dtype)


def dilated_merge(outs, lses, dilations, rows=512):
    gw = DIL_HEADS_PER_GROUP * HEAD_DIM
    t = outs[0].shape[0] * dilations[0]
    rows = min(rows, t)
    specs = [pl.BlockSpec((rows // r, r * gw), lambda i: (i, 0)) for r in dilations]
    return pl.pallas_call(
        functools.partial(_dil_merge_kernel, dilations=tuple(dilations)),
        grid=(t // rows,),
        in_specs=specs + specs,
        out_specs=pl.BlockSpec((rows, gw), lambda i: (i, 0)),
        out_shape=jax.ShapeDtypeStruct((t, gw), BF16),
        scratch_shapes=[pltpu.VMEM((gw // LANES, rows, LANES), F32) for _ in range(2 * len(dilations))],
        compiler_params=_params(("parallel",)),
        name="dilated_merge",
    )(*outs, *lses)


def _diff_kernel(lam_ref, g_ref, q_ref, k_ref, v_ref, o_ref, *, tq, heads, key_blocks, scale2, lam_init):
    i = pl.program_id(2)
    tk = key_blocks * tq
    lp = lam_ref[...]
    lam = (jnp.exp(jnp.sum(lp[0:1] * lp[1:2], axis=1, keepdims=True))
           - jnp.exp(jnp.sum(lp[2:3] * lp[3:4], axis=1, keepdims=True)) + lam_init)
    lane = lax.broadcasted_iota(jnp.int32, (tq, LANES), 1)
    qqs = []
    for h in range(heads):
        q = q_ref[:, h * LANES:(h + 1) * LANES].astype(F32)
        qqs.append(jnp.concatenate([jnp.where(lane < DIFF_DIM, q, 0.0), jnp.where(lane >= DIFF_DIM, q, 0.0)],
                                   axis=0).astype(BF16))
    row = lax.broadcasted_iota(jnp.int32, (2 * tq, tk), 0)
    row = jnp.where(row >= tq, row - tq, row)
    col = lax.broadcasted_iota(jnp.int32, (2 * tq, tk), 1)
    kd = i // key_blocks
    causal = col <= row + (i - key_blocks * kd) * tq
    dims = (((1,), (1,)), ((), ()))

    def block(kb, state, diagonal):
        start = pl.multiple_of(kb * tk, tk)
        out = []
        for h in range(heads):
            m, l, acc = state[h]
            sl = slice(h * LANES, (h + 1) * LANES)
            k = k_ref[pl.ds(start, tk), sl]
            v = v_ref[pl.ds(start, tk), sl]
            s = lax.dot_general(qqs[h], k, dims, preferred_element_type=F32) * scale2
            if diagonal:
                s = jnp.where(causal, s, NEG_BIG)
            m_new = jnp.maximum(m, jnp.max(s, axis=1, keepdims=True))
            alpha = jnp.exp2(m - m_new)
            p = jnp.exp2(s - m_new)
            l = alpha * l + jnp.sum(p, axis=1, keepdims=True)
            acc = alpha * acc + jnp.dot(p.astype(BF16), v, preferred_element_type=F32)
            out.append((m_new, l, acc))
        return tuple(out)

    state = tuple((jnp.full((2 * tq, 1), NEG_BIG, F32), jnp.zeros((2 * tq, 1), F32),
                   jnp.zeros((2 * tq, LANES), F32)) for _ in range(heads))
    state = block(kd, state, True)
    state = lax.fori_loop(0, kd, lambda n, st: block(n, st, False), state)
    for h in range(heads):
        _, l, acc = state[h]
        o_all = acc / l
        o = o_all[:tq] - lam * o_all[tq:]
        ms = jnp.mean(o * o, axis=-1, keepdims=True)
        y = o * lax.rsqrt(ms + RMS_EPS) * g_ref[...]
        o_ref[:, h * LANES:(h + 1) * LANES] = (y * (1.0 - lam_init)).astype(o_ref.dtype)


def differential_attention(proj, lam_params, subln_g, seq_len, q_col, k_col, v_col, lam_init, tq=1024, heads=2,
                           key_blocks=1):
    b = proj.shape[0]
    tq = min(tq, seq_len // key_blocks)
    hw = heads * LANES
    assert seq_len % (key_blocks * tq) == 0 and DIFF_HEADS % heads == 0
    assert q_col % hw == 0 and k_col % hw == 0 and v_col % hw == 0
    qb, kb, vb = q_col // hw, k_col // hw, v_col // hw
    kern = functools.partial(_diff_kernel, tq=tq, heads=heads, key_blocks=key_blocks,
                             scale2=DIFF_DIM ** -0.5 * LOG2_E, lam_init=lam_init)
    return pl.pallas_call(
        kern,
        grid=(b, DIFF_HEADS // heads, seq_len // tq),
        in_specs=[pl.BlockSpec((4, DIFF_DIM), lambda bi, h, i: (0, 0)),
                  pl.BlockSpec((1, 2 * DIFF_DIM), lambda bi, h, i: (0, 0)),
                  pl.BlockSpec((None, tq, hw), lambda bi, h, i: (bi, i, qb + h)),
                  pl.BlockSpec((None, seq_len, hw), lambda bi, h, i: (bi, 0, kb + h)),
                  pl.BlockSpec((None, seq_len, hw), lambda bi, h, i: (bi, 0, vb + h))],
        out_specs=pl.BlockSpec((None, tq, hw), lambda bi, h, i: (bi, i, h)),
        out_shape=jax.ShapeDtypeStruct((b, seq_len, DIFF_HEADS * 2 * DIFF_DIM), BF16),
        compiler_params=_params(("parallel", "parallel", "arbitrary")),
        name="differential_attention",
    )(lam_params, subln_g.reshape(1, -1), proj, proj, proj)


def _gate_merge_kernel(h_ref, wg0, wg1, wg2, o0, o1, o2, wb0, wb1, wb2, out_ref):
    h = h_ref[...]
    acc = None
    for wg, o, wb in ((wg0, o0, wb0), (wg1, o1, wb1), (wg2, o2, wb2)):
        gate = 1.0 / (1.0 + jnp.exp(-jnp.dot(h, wg[...], preferred_element_type=F32)))
        term = gate * jnp.dot(o[...], wb[...], preferred_element_type=F32)
        acc = term if acc is None else acc + term
    out_ref[...] = acc.astype(out_ref.dtype)


def gate_merge(h, w_gate, branch_outs, branch_ws, layer, tm=1024, tn=256):
    t, d = h.shape
    tm, tn = min(tm, t), min(tn, d)
    nj = d // tn
    in_specs = [pl.BlockSpec((tm, d), lambda i, j: (i, 0))]
    in_specs += [pl.BlockSpec((None, d, tn), functools.partial(lambda i, j, b: (layer, 0, b * nj + j), b=b))
                 for b in range(N_BRANCH)]
    in_specs += [pl.BlockSpec((tm, o.shape[1]), lambda i, j: (i, 0)) for o in branch_outs]
    in_specs += [pl.BlockSpec((None, w.shape[1], tn), lambda i, j: (layer, 0, j)) for w in branch_ws]
    return pl.pallas_call(
        _gate_merge_kernel,
        grid=(t // tm, nj),
        in_specs=in_specs,
        out_specs=pl.BlockSpec((tm, tn), lambda i, j: (i, j)),
        out_shape=jax.ShapeDtypeStruct((t, d), BF16),
        compiler_params=_params(("parallel", "arbitrary")),
        name="gate_merge",
    )(h, w_gate, w_gate, w_gate, *branch_outs, *branch_ws)


STAT_TAU, STAT_MAX1, STAT_MAX2, STAT_INVZ = 0, 1, 2, 3
STAT_ROWS = 8


def _top_values(x, scr, count):
    for kk in range(count):
        m = jnp.max(x, axis=0, keepdims=True)
        scr[kk:kk + 1, :] = m
        x = jnp.where(x == m, -jnp.inf, x)


def _peer_route_kernel(q_ref, keys_ref, s1_ref, s2_ref, stat_ref, a_scr, b_scr, c_scr, t_scr):
    dims = (((1,), (1,)), ((), ()))
    s1 = lax.dot_general(keys_ref[0], q_ref[:, :PEER_HALF_QDIM], dims, preferred_element_type=F32)
    s2 = lax.dot_general(keys_ref[1], q_ref[:, PEER_HALF_QDIM:], dims, preferred_element_type=F32)
    s1_ref[...] = s1
    s2_ref[...] = s2
    _top_values(s1, a_scr, PEER_TOPK)
    _top_values(s2, b_scr, PEER_TOPK)
    half = PEER_TOPK // 2
    c_scr[0:PEER_TOPK, :] = a_scr[0:1, :] + b_scr[...]
    for ii in range(1, half):
        c_scr[PEER_TOPK + (ii - 1) * half:PEER_TOPK + ii * half, :] = a_scr[ii:ii + 1, :] + b_scr[0:half, :]
    c_scr[PEER_TOPK + (half - 1) * half:PEER_CANDIDATES, :] = a_scr[half:PEER_TOPK, :] + b_scr[0:1, :]
    cand = c_scr[...]
    _top_values(cand, t_scr, PEER_TOPK)
    tau = t_scr[PEER_TOPK - 1:PEER_TOPK, :]
    best = t_scr[0:1, :]
    z = jnp.sum(jnp.where(cand >= tau, jnp.exp(cand - best), 0.0), axis=0, keepdims=True)
    stat_ref[...] = jnp.zeros_like(stat_ref)
    stat_ref[STAT_TAU:STAT_TAU + 1, :] = tau
    stat_ref[STAT_MAX1:STAT_MAX1 + 1, :] = a_scr[0:1, :]
    stat_ref[STAT_MAX2:STAT_MAX2 + 1, :] = b_scr[0:1, :]
    stat_ref[STAT_INVZ:STAT_INVZ + 1, :] = 1.0 / z


def peer_route(q, sub_keys, layer, tb=256):
    t = q.shape[0]
    tb = min(tb, t)
    score_sds = jax.ShapeDtypeStruct((PEER_HEADS, PEER_NKEYS, t), F32)
    score_spec = pl.BlockSpec((None, PEER_NKEYS, tb), lambda i, h: (h, 0, i))
    return pl.pallas_call(
        _peer_route_kernel,
        grid=(t // tb, PEER_HEADS),
        in_specs=[pl.BlockSpec((tb, 2 * PEER_HALF_QDIM), lambda i, h: (i, h)),
                  pl.BlockSpec((None, None, 2, PEER_NKEYS, PEER_HALF_QDIM), lambda i, h: (layer, h, 0, 0, 0))],
        out_specs=[score_spec, score_spec, pl.BlockSpec((None, STAT_ROWS, tb), lambda i, h: (h, 0, i))],
        out_shape=[score_sds, score_sds, jax.ShapeDtypeStruct((PEER_HEADS, STAT_ROWS, t), F32)],
        scratch_shapes=[pltpu.VMEM((PEER_TOPK, tb), F32), pltpu.VMEM((PEER_TOPK, tb), F32),
                        pltpu.VMEM((PEER_CANDIDATES, tb), F32), pltpu.VMEM((PEER_TOPK, tb), F32)],
        compiler_params=_params(("parallel", "arbitrary")),
        name="peer_route",
    )(q, sub_keys)


def _gelu(a):
    return 0.5 * a * (1.0 + lax.erf(a * (2.0 ** -0.5)))


PEER_STAGES = 3
PEER_EXPERT_BLOCK = 512
GATE_KEY_SPLITS = 8


def _peer_dense_kernel(ht_ref, u_ref, vt_ref, s1_ref, s2_ref, stat_ref, out_ref, e2_scr,
                       act_a, act_b, coef_a, coef_b, *, te, n_blocks):
    e = pl.program_id(1)

    @pl.when(e == 0)
    def _():
        out_ref[...] = jnp.zeros_like(out_ref)
        for h in range(PEER_HEADS):
            e2_scr[h] = jnp.exp(s2_ref[h] - stat_ref[h, STAT_MAX2:STAT_MAX2 + 1, :])

    n_sub = te // PEER_NKEYS
    d_model, tb = ht_ref.shape
    gate_block = e - 1

    def stages(buffers, scores=True, gates=True, output=True):
        act_cur, act_prev, coef_cur, coef_prev = buffers
        n_lane = tb // LANES
        n_slices = GATE_KEY_SPLITS * n_lane
        kc = d_model // n_slices
        if scores:
            act_cur[...] = jnp.dot(u_ref[...], ht_ref[...], preferred_element_type=F32)
        s1_rows, e1_rows = [], []
        for sub in range(n_sub if gates else 0):
            i_idx = gate_block * n_sub + sub
            s1_rows.append([s1_ref[h, pl.ds(i_idx, 1), :] for h in range(PEER_HEADS)])
            e1_rows.append([jnp.exp(s1_rows[sub][h] - stat_ref[h, STAT_MAX1:STAT_MAX1 + 1, :])
                            * stat_ref[h, STAT_INVZ:STAT_INVZ + 1, :] for h in range(PEER_HEADS)])
        half = PEER_NKEYS // GATE_KEY_SPLITS
        for r in range(n_slices):
            c, jh = divmod(r, GATE_KEY_SPLITS)
            lanes = slice(c * LANES, (c + 1) * LANES)
            keys = slice(jh * half, (jh + 1) * half)
            tiles = [None] * n_sub
            for h in range(PEER_HEADS if gates else 0):
                s2_tile = s2_ref[h, keys, lanes]
                e2_tile = e2_scr[h, keys, lanes]
                tau = stat_ref[h, STAT_TAU:STAT_TAU + 1, lanes]
                for sub in range(n_sub):
                    term = jnp.where(s2_tile + s1_rows[sub][h][:, lanes] >= tau,
                                     e2_tile * e1_rows[sub][h][:, lanes], 0.0)
                    tiles[sub] = term if tiles[sub] is None else tiles[sub] + term
            for sub in range(n_sub if gates else 0):
                rows = slice(sub * PEER_NKEYS + jh * half, sub * PEER_NKEYS + (jh + 1) * half)
                coef_prev[rows, lanes] = (tiles[sub] * _gelu(act_prev[rows, lanes])).astype(BF16)
            if output:
                chunk = slice(r * kc, (r + 1) * kc)
                out_ref[chunk, :] += jnp.dot(vt_ref[chunk, :], coef_cur[...], preferred_element_type=F32)

    even, odd = (act_a, act_b, coef_a, coef_b), (act_b, act_a, coef_b, coef_a)
    final = n_blocks + PEER_STAGES - 2
    assert final >= PEER_STAGES
    edge_steps = {0: dict(gates=False, output=False), 1: dict(output=False),
                  final - 1: dict(scores=False), final: dict(scores=False, gates=False)}
    for step, flags in edge_steps.items():
        pl.when(e == step)(functools.partial(stages, even if step % 2 == 0 else odd, **flags))
    interior = (e > 1) & (e < final - 1)
    parity = lax.rem(e, 2)
    pl.when(interior & (parity == 0))(functools.partial(stages, even))
    pl.when(interior & (parity == 1))(functools.partial(stages, odd))


def peer_dense(ht, u_tab, vt_blocks, layer, s1, s2, stats, tb=512):
    d, t = ht.shape
    _, n_blocks, _, te = vt_blocks.shape
    tb = min(tb, t)
    assert t % tb == 0 and u_tab.shape[1] == n_blocks * te and te % PEER_NKEYS == 0
    once = pl.Buffered(1)
    tok_spec = pl.BlockSpec((PEER_HEADS, PEER_NKEYS, tb), lambda i, e: (0, 0, i), pipeline_mode=once)
    last = n_blocks - 1
    kern = functools.partial(_peer_dense_kernel, te=te, n_blocks=n_blocks)
    return pl.pallas_call(
        kern,
        grid=(t // tb, n_blocks + PEER_STAGES - 1),
        in_specs=[pl.BlockSpec((d, tb), lambda i, e: (0, i), pipeline_mode=once),
                  pl.BlockSpec((None, te, d), lambda i, e: (layer, jnp.minimum(e, last), 0)),
                  pl.BlockSpec((None, None, d, te), lambda i, e: (layer, jnp.clip(e - 2, 0, last), 0, 0)),
                  tok_spec, tok_spec,
                  pl.BlockSpec((PEER_HEADS, STAT_ROWS, tb), lambda i, e: (0, 0, i), pipeline_mode=once)],
        out_specs=pl.BlockSpec((d, tb), lambda i, e: (0, i)),
        out_shape=jax.ShapeDtypeStruct((d, t), F32),
        scratch_shapes=[pltpu.VMEM((PEER_HEADS, PEER_NKEYS, tb), F32),
                        pltpu.VMEM((te, tb), F32), pltpu.VMEM((te, tb), F32),
                        pltpu.VMEM((te, tb), BF16), pltpu.VMEM((te, tb), BF16)],
        compiler_params=_params(("parallel", "arbitrary")),
        name="peer_dense",
    )(ht, u_tab, vt_blocks, s1, s2, stats)


def kernel(x, attn_norm_g, ffn_norm_g, final_norm_g, w_qkv, w_gate, w_branch_sb, w_branch_dil,
           w_branch_diff, w_out, diff_lambda, diff_subln_g, peer_w_q, peer_sub_keys, peer_u, peer_v):
    b, s, d = x.shape
    t = b * s
    depth = w_qkv.shape[0]
    sb_w = SB_HEADS * HEAD_DIM
    dil_w = DIL_HEADS_PER_GROUP * len(DIL_GROUPS) * HEAD_DIM
    diff_w = DIFF_HEADS * 2 * DIFF_DIM
    dl_q, dl_k, dl_v = 3 * sb_w, 3 * sb_w + dil_w, 3 * sb_w + 2 * dil_w
    df_q = 3 * sb_w + 3 * dil_w
    df_k, df_v = df_q + diff_w, df_q + 2 * diff_w
    qkv_cols = df_v + diff_w
    rope_ops = _rope_operands(s)
    w_qkv_b, w_gate_b, w_out_b, w_pq_b = (w.astype(BF16) for w in (w_qkv, w_gate, w_out, peer_w_q))
    w_branch_b = tuple(w.astype(BF16) for w in (w_branch_sb, w_branch_dil, w_branch_diff))
    keys_b, u_b = peer_sub_keys.astype(BF16), peer_u.astype(BF16)
    vt_blocks = peer_v.reshape(depth, -1, PEER_EXPERT_BLOCK, d).transpose(0, 1, 3, 2).astype(BF16)

    xt = x.reshape(t, d)
    peer_out_t = None
    for layer in range(depth):
        if peer_out_t is None:
            h = rmsnorm(xt, attn_norm_g[layer], BF16)
        else:
            h, xt = residual_rmsnorm(xt, peer_out_t, attn_norm_g[layer], BF16, return_sum=True)
        proj = qkv_projection(h, w_qkv_b, layer, rope_ops, s,
                              rope128_cols=(dl_q, dl_v), rope64_cols=(df_q, df_v))
        proj3 = proj.reshape(b, s, qkv_cols)
        o_sb = stick_breaking_attention(proj3, s, 0).reshape(t, sb_w)
        dil = []
        gw = DIL_HEADS_PER_GROUP * HEAD_DIM
        for g, (_, dilation) in enumerate(DIL_GROUPS):
            cols = (dl_q + g * gw, dl_k + g * gw, dl_v + g * gw)
            if dilation == 1:
                dil.append(dilated_group_attention(proj, proj, proj, b, s, g, dilation, *cols))
            else:
                qkv_views = dilated_regroup(proj, dilation, *cols)
                dil.append(dilated_group_attention(*qkv_views, b, s, g, dilation, 0, 0, 0))
        o_dl = dilated_merge([o for o, _ in dil], [l for _, l in dil], [r for _, r in DIL_GROUPS])
        lam_init = 0.8 - 0.6 * math.exp(-0.3 * layer)
        o_df = differential_attention(proj3, diff_lambda[layer], diff_subln_g[layer], s,
                                      df_q, df_k, df_v, lam_init).reshape(t, diff_w)
        merged = gate_merge(h, w_gate_b, (o_sb, o_dl, o_df), w_branch_b, layer)
        xt = matmul(merged, w_out_b, layer, F32, residual=xt)

        h2, h2_t = rmsnorm(xt, ffn_norm_g[layer], BF16, with_transpose=True)
        q = matmul(h2, w_pq_b, layer, BF16)
        s1, s2, stats = peer_route(q, keys_b, layer)
        peer_out_t = peer_dense(h2_t, u_b, vt_blocks, layer, s1, s2, stats)
    (out,) = residual_rmsnorm(xt, peer_out_t, final_norm_g, F32, return_sum=False)
    return out.reshape(b, s, d)
```

```python
import functools
import math

import jax
import jax.numpy as jnp
from jax import lax
from jax.experimental import pallas as pl
from jax.experimental.pallas import tpu as pltpu

F32 = jnp.float32
BF16 = jnp.bfloat16

HEAD_DIM = 128
ROPE_THETA = 10000.0
RMS_EPS = 1e-6
NEG_BIG = -1e30

SB_HEADS = 8
DIL_GROUPS = ((128, 1), (512, 4), (2048, 16))
DIL_HEADS_PER_GROUP = 4
DIFF_HEADS = 8
DIFF_DIM = 64
N_BRANCH = 3

PEER_HEADS = 8
PEER_NKEYS = 128
PEER_HALF_QDIM = 128
PEER_TOPK = 16
PEER_CANDIDATES = 16 + 7 * 8 + 8

LANES = 128
VMEM_LIMIT = 56 * 1024 * 1024


def _params(sem, vmem=VMEM_LIMIT):
    return pltpu.CompilerParams(dimension_semantics=sem, vmem_limit_bytes=vmem)


def _rmsnorm_kernel(x_ref, g_ref, o_ref):
    x = x_ref[...]
    ms = jnp.mean(x * x, axis=-1, keepdims=True)
    o_ref[...] = (x * lax.rsqrt(ms + RMS_EPS) * g_ref[...]).astype(o_ref.dtype)


def _rmsnorm_both_kernel(x_ref, g_ref, o_ref, ot_ref):
    x = x_ref[...]
    ms = jnp.mean(x * x, axis=-1, keepdims=True)
    y = x * lax.rsqrt(ms + RMS_EPS) * g_ref[...]
    o_ref[...] = y.astype(o_ref.dtype)
    ot_ref[...] = y.T.astype(ot_ref.dtype)


def _residual_rmsnorm_kernel(x_ref, dt_ref, g_ref, o_ref, *sum_ref):
    x = x_ref[...] + dt_ref[...].T
    for ref in sum_ref:
        ref[...] = x
    ms = jnp.mean(x * x, axis=-1, keepdims=True)
    o_ref[...] = (x * lax.rsqrt(ms + RMS_EPS) * g_ref[...]).astype(o_ref.dtype)


def residual_rmsnorm(x, delta_t, g, out_dtype, return_sum, rows=256):
    t, d = x.shape
    rows = min(rows, t)
    row_spec = pl.BlockSpec((rows, d), lambda i: (i, 0))
    out_specs, out_shape = [row_spec], [jax.ShapeDtypeStruct((t, d), out_dtype)]
    if return_sum:
        out_specs.append(row_spec)
        out_shape.append(jax.ShapeDtypeStruct((t, d), x.dtype))
    return pl.pallas_call(
        _residual_rmsnorm_kernel,
        grid=(t // rows,),
        in_specs=[row_spec, pl.BlockSpec((d, rows), lambda i: (0, i)), pl.BlockSpec((1, d), lambda i: (0, 0))],
        out_specs=out_specs,
        out_shape=out_shape,
        compiler_params=_params(("parallel",)),
        name="residual_rmsnorm",
    )(x, delta_t, g.reshape(1, d))


def rmsnorm(x, g, out_dtype, rows=256, with_transpose=False):
    t, d = x.shape
    rows = min(rows, t)
    row_spec = pl.BlockSpec((rows, d), lambda i: (i, 0))
    out_specs, out_shape = row_spec, jax.ShapeDtypeStruct((t, d), out_dtype)
    if with_transpose:
        out_specs = [row_spec, pl.BlockSpec((d, rows), lambda i: (0, i))]
        out_shape = [out_shape, jax.ShapeDtypeStruct((d, t), out_dtype)]
    return pl.pallas_call(
        _rmsnorm_both_kernel if with_transpose else _rmsnorm_kernel,
        grid=(t // rows,),
        in_specs=[row_spec, pl.BlockSpec((1, d), lambda i: (0, 0))],
        out_specs=out_specs,
        out_shape=out_shape,
        compiler_params=_params(("parallel",)),
        name="rmsnorm_transposed" if with_transpose else "rmsnorm",
    )(x, g.reshape(1, d))


def _rope_tables(seq_len, dim):
    inv_freq = 1.0 / (ROPE_THETA ** (jnp.arange(0, dim, 2, dtype=F32) / dim))
    ang = jnp.arange(seq_len, dtype=F32)[:, None] * inv_freq[None, :]
    ang = jnp.concatenate([ang, ang], axis=-1)
    return jnp.cos(ang), jnp.sin(ang)


def _rope_operands(seq_len):
    cos_h, sin_h = _rope_tables(seq_len, HEAD_DIM)
    lane = jnp.arange(LANES)
    sin_h_signed = jnp.where(lane < HEAD_DIM // 2, -sin_h, sin_h)
    cos_d, sin_d = _rope_tables(seq_len, DIFF_DIM)
    cos_d2 = jnp.concatenate([cos_d, cos_d], axis=-1)
    sin_d2 = jnp.concatenate([sin_d, sin_d], axis=-1)
    low = (lane % DIFF_DIM) < DIFF_DIM // 2
    sin_d_low = jnp.where(low, -sin_d2, 0.0)
    sin_d_high = jnp.where(low, 0.0, sin_d2)
    return cos_h, sin_h_signed, cos_d2, sin_d_low, sin_d_high


def _qkv_kernel(a_ref, w_ref, cos_h, sin_h, cos_d, sin_dl, sin_dh, o_ref, *, rope128_blocks, rope64_blocks):
    j = pl.program_id(1)
    acc = jnp.dot(a_ref[...], w_ref[...], preferred_element_type=F32)
    n_chunks = acc.shape[1] // LANES
    in128 = (j >= rope128_blocks[0]) & (j < rope128_blocks[1])
    in64 = (j >= rope64_blocks[0]) & (j < rope64_blocks[1])

    @pl.when(in128)
    def _():
        for c in range(n_chunks):
            x = acc[:, c * LANES:(c + 1) * LANES]
            y = x * cos_h[...] + pltpu.roll(x, HEAD_DIM // 2, 1) * sin_h[...]
            o_ref[:, c * LANES:(c + 1) * LANES] = y.astype(o_ref.dtype)

    @pl.when(in64)
    def _():
        for c in range(n_chunks):
            x = acc[:, c * LANES:(c + 1) * LANES]
            y = (x * cos_d[...] + pltpu.roll(x, LANES - DIFF_DIM // 2, 1) * sin_dl[...]
                 + pltpu.roll(x, DIFF_DIM // 2, 1) * sin_dh[...])
            o_ref[:, c * LANES:(c + 1) * LANES] = y.astype(o_ref.dtype)

    @pl.when(jnp.logical_not(in128 | in64))
    def _():
        o_ref[...] = acc.astype(o_ref.dtype)


def qkv_projection(h, w, layer, rope_ops, seq_len, rope128_cols, rope64_cols, tm=1024, tn=512):
    m, k = h.shape
    n = w.shape[2]
    tm = min(tm, seq_len)
    assert seq_len % tm == 0 and m % tm == 0 and n % tn == 0
    for lo, hi in (rope128_cols, rope64_cols):
        assert lo % tn == 0 and hi % tn == 0
    seq_blocks = seq_len // tm
    tab_spec = pl.BlockSpec((tm, LANES), lambda i, j: (i % seq_blocks, 0))
    kern = functools.partial(
        _qkv_kernel,
        rope128_blocks=(rope128_cols[0] // tn, rope128_cols[1] // tn),
        rope64_blocks=(rope64_cols[0] // tn, rope64_cols[1] // tn))
    return pl.pallas_call(
        kern,
        grid=(m // tm, n // tn),
        in_specs=[pl.BlockSpec((tm, k), lambda i, j: (i, 0)),
                  pl.BlockSpec((None, k, tn), lambda i, j: (layer, 0, j)),
                  tab_spec, tab_spec, tab_spec, tab_spec, tab_spec],
        out_specs=pl.BlockSpec((tm, tn), lambda i, j: (i, j)),
        out_shape=jax.ShapeDtypeStruct((m, n), BF16),
        compiler_params=_params(("parallel", "arbitrary")),
        name="qkv_projection",
    )(h, w, *rope_ops)


def _matmul_kernel(a_ref, w_ref, o_ref):
    o_ref[...] = jnp.dot(a_ref[...], w_ref[...], preferred_element_type=F32).astype(o_ref.dtype)


def _matmul_residual_kernel(a_ref, w_ref, r_ref, o_ref):
    o_ref[...] = r_ref[...] + jnp.dot(a_ref[...], w_ref[...], preferred_element_type=F32)


def matmul(a, w, layer, out_dtype, residual=None, tm=1024, tn=512):
    m, k = a.shape
    n = w.shape[2]
    tm, tn = min(tm, m), min(tn, n)
    assert m % tm == 0 and n % tn == 0
    in_specs = [pl.BlockSpec((tm, k), lambda i, j: (i, 0)),
                pl.BlockSpec((None, k, tn), lambda i, j: (layer, 0, j))]
    args = [a, w]
    kern = _matmul_kernel
    if residual is not None:
        in_specs.append(pl.BlockSpec((tm, tn), lambda i, j: (i, j)))
        args.append(residual)
        kern = _matmul_residual_kernel
    return pl.pallas_call(
        kern,
        grid=(m // tm, n // tn),
        in_specs=in_specs,
        out_specs=pl.BlockSpec((tm, tn), lambda i, j: (i, j)),
        out_shape=jax.ShapeDtypeStruct((m, n), out_dtype),
        compiler_params=_params(("parallel", "arbitrary")),
        name="matmul_residual" if residual is not None else "matmul",
    )(*args)


LOG2_E = math.log2(math.e)
SIGN_BIT = 0x80000000
SB_SUFFIX_BLOCK = 256


def _softplus2(u):
    neg_abs = lax.bitcast_convert_type(lax.bitcast_convert_type(u, jnp.uint32) | jnp.uint32(SIGN_BIT), F32)
    return jnp.maximum(u, 0.0) + jnp.log2(1.0 + jnp.exp2(neg_abs))


def _sb_kernel(suffix_ref, q_ref, k_ref, v_ref, o_ref, *, tq, heads, scale2):
    i = pl.program_id(2)
    tk = 2 * tq
    suffix = suffix_ref[...]
    row = lax.broadcasted_iota(jnp.int32, (tq, tq), 0)
    col = lax.broadcasted_iota(jnp.int32, (tq, tq), 1)
    before = col < row
    dims = (((1,), (1,)), ((), ()))
    qs = [q_ref[:, h * HEAD_DIM:(h + 1) * HEAD_DIM] for h in range(heads)]

    def block(first_key, n_keys, state, diagonal):
        carries = [st[0] for st in state]
        accs = [st[1] for st in state]
        for j in reversed(range(n_keys // SB_SUFFIX_BLOCK)):
            start = pl.multiple_of(first_key + j * SB_SUFFIX_BLOCK, SB_SUFFIX_BLOCK)
            cols = slice(j * SB_SUFFIX_BLOCK, (j + 1) * SB_SUFFIX_BLOCK)
            for h in range(heads):
                sl = slice(h * HEAD_DIM, (h + 1) * HEAD_DIM)
                k = k_ref[pl.ds(start, SB_SUFFIX_BLOCK), sl]
                v = v_ref[pl.ds(start, SB_SUFFIX_BLOCK), sl]
                z = lax.dot_general(qs[h], k, dims, preferred_element_type=F32) * scale2
                sp = _softplus2(z)
                spm = jnp.where(before[:, cols], sp, 0.0) if diagonal else sp
                later = jnp.dot(spm.astype(BF16), suffix, preferred_element_type=F32) + carries[h]
                w = jnp.exp2(z - sp - later)
                if diagonal:
                    w = jnp.where(before[:, cols], w, 0.0)
                accs[h] = accs[h] + jnp.dot(w.astype(BF16), v, preferred_element_type=F32)
                carries[h] = carries[h] + jnp.sum(spm, axis=1, keepdims=True)
        return tuple(zip(carries, accs))

    state = tuple((jnp.zeros((tq, 1), F32), jnp.zeros((tq, HEAD_DIM), F32)) for _ in range(heads))
    state = block(i * tq, tq, state, True)
    state = lax.fori_loop(0, i & 1, lambda n, st: block((i - 1) * tq, tq, st, False), state)
    pairs = lax.shift_right_logical(i, 1)
    state = lax.fori_loop(0, pairs, lambda n, st: block((pairs - 1 - n) * tk, tk, st, False), state)
    for h in range(heads):
        o_ref[:, h * HEAD_DIM:(h + 1) * HEAD_DIM] = state[h][1].astype(o_ref.dtype)


def stick_breaking_attention(proj, seq_len, col0, tq=1024, heads=2):
    b = proj.shape[0]
    tq = min(tq, seq_len // 2)
    hw = heads * HEAD_DIM
    assert seq_len % (2 * tq) == 0 and SB_HEADS % heads == 0 and col0 % hw == 0
    c0 = col0 // hw
    per = SB_HEADS // heads
    assert tq % SB_SUFFIX_BLOCK == 0
    idx = jnp.arange(SB_SUFFIX_BLOCK)
    suffix = (idx[:, None] > idx[None, :]).astype(BF16)
    kern = functools.partial(_sb_kernel, tq=tq, heads=heads, scale2=HEAD_DIM ** -0.5 * LOG2_E)
    return pl.pallas_call(
        kern,
        grid=(b, per, seq_len // tq),
        in_specs=[pl.BlockSpec((SB_SUFFIX_BLOCK, SB_SUFFIX_BLOCK), lambda bi, h, i: (0, 0)),
                  pl.BlockSpec((None, tq, hw), lambda bi, h, i: (bi, i, c0 + h)),
                  pl.BlockSpec((None, seq_len, hw), lambda bi, h, i: (bi, 0, c0 + per + h)),
                  pl.BlockSpec((None, seq_len, hw), lambda bi, h, i: (bi, 0, c0 + 2 * per + h))],
        out_specs=pl.BlockSpec((None, tq, hw), lambda bi, h, i: (bi, i, h)),
        out_shape=jax.ShapeDtypeStruct((b, seq_len, SB_HEADS * HEAD_DIM), BF16),
        compiler_params=_params(("parallel", "parallel", "arbitrary")),
        name="stick_breaking_attention",
    )(suffix, proj, proj, proj)


def _dil_kernel(q_ref, kp_ref, kc_ref, vp_ref, vc_ref, o_ref, lse_ref, *, tq, scale):
    i = pl.program_id(2)
    row = lax.broadcasted_iota(jnp.int32, (tq, tq), 0)
    col = lax.broadcasted_iota(jnp.int32, (tq, tq), 1)
    cur_ok = col <= row
    prev_ok = col >= row + jnp.where(i > 0, 0, tq)
    dims = (((1,), (1,)), ((), ()))
    for h in range(DIL_HEADS_PER_GROUP):
        sl = slice(h * HEAD_DIM, (h + 1) * HEAD_DIM)
        q = q_ref[:, sl]
        s_cur = lax.dot_general(q, kc_ref[:, sl], dims, preferred_element_type=F32) * scale
        s_prev = lax.dot_general(q, kp_ref[:, sl], dims, preferred_element_type=F32) * scale
        s_cur = jnp.where(cur_ok, s_cur, NEG_BIG)
        s_prev = jnp.where(prev_ok, s_prev, NEG_BIG)
        m = jnp.maximum(jnp.max(s_cur, axis=1, keepdims=True), jnp.max(s_prev, axis=1, keepdims=True))
        p_cur = jnp.exp(s_cur - m)
        p_prev = jnp.exp(s_prev - m)
        l = jnp.sum(p_cur, axis=1, keepdims=True) + jnp.sum(p_prev, axis=1, keepdims=True)
        o = (jnp.dot(p_cur.astype(BF16), vc_ref[:, sl], preferred_element_type=F32)
             + jnp.dot(p_prev.astype(BF16), vp_ref[:, sl], preferred_element_type=F32))
        o_ref[:, sl] = o / l
        lse_ref[:, sl] = jnp.broadcast_to(m + jnp.log(l), (tq, HEAD_DIM))


def _regroup_kernel(q_ref, k_ref, v_ref, qo_ref, ko_ref, vo_ref, scr, *, dilation):
    rows, gw = q_ref.shape
    n = rows // dilation
    for src, dst in ((q_ref, qo_ref), (k_ref, ko_ref), (v_ref, vo_ref)):
        for j in range(gw // LANES):
            scr[j] = src[:, j * LANES:(j + 1) * LANES].astype(F32)
        for c in range(dilation):
            for j in range(gw // LANES):
                dst[:, c * gw + j * LANES:c * gw + (j + 1) * LANES] = (
                    scr[j, pl.ds(c, n, stride=dilation), :].astype(dst.dtype))


def dilated_regroup(proj, dilation, q_col, k_col, v_col, rows=512):
    t, cols = proj.shape
    gw = DIL_HEADS_PER_GROUP * HEAD_DIM
    rows = min(rows, t)
    assert t % rows == 0 and rows % (16 * dilation) == 0
    in_specs = [pl.BlockSpec((rows, gw), functools.partial(lambda i, blk: (i, blk), blk=col // gw))
                for col in (q_col, k_col, v_col)]
    out_spec = pl.BlockSpec((rows // dilation, dilation * gw), lambda i: (i, 0))
    out_sds = jax.ShapeDtypeStruct((t // dilation, dilation * gw), proj.dtype)
    return pl.pallas_call(
        functools.partial(_regroup_kernel, dilation=dilation),
        grid=(t // rows,),
        in_specs=in_specs,
        out_specs=[out_spec] * 3,
        out_shape=[out_sds] * 3,
        scratch_shapes=[pltpu.VMEM((gw // LANES, rows, LANES), F32)],
        compiler_params=_params(("parallel",)),
        name=f"dilated_regroup_r{dilation}",
    )(proj, proj, proj)


def dilated_group_attention(q_arr, k_arr, v_arr, batch, seq_len, group, dilation, q_col, k_col, v_col):
    gw = DIL_HEADS_PER_GROUP * HEAD_DIM
    tq = DIL_GROUPS[group][0] // dilation
    sub_len = seq_len // dilation
    assert sub_len % tq == 0
    views, blocks, per_row = [], [], []
    for arr, col in ((q_arr, q_col), (k_arr, k_col), (v_arr, v_col)):
        width = arr.shape[1] // dilation
        assert width % gw == 0 and col % gw == 0
        views.append(arr.reshape(batch, sub_len, dilation * width))
        blocks.append(col // gw)
        per_row.append(width // gw)
    prev = lambda i: jnp.maximum(i - 1, 0)

    def spec(which, row_of):
        return pl.BlockSpec((None, tq, gw),
                            lambda bi, c, i: (bi, row_of(i), c * per_row[which] + blocks[which]))

    out_sds = jax.ShapeDtypeStruct((batch, sub_len, dilation * gw), F32)
    out_spec = pl.BlockSpec((None, tq, gw), lambda bi, c, i: (bi, i, c))
    kern = functools.partial(_dil_kernel, tq=tq, scale=HEAD_DIM ** -0.5)
    o, lse = pl.pallas_call(
        kern,
        grid=(batch, dilation, sub_len // tq),
        in_specs=[spec(0, lambda i: i), spec(1, prev), spec(1, lambda i: i), spec(2, prev), spec(2, lambda i: i)],
        out_specs=[out_spec, out_spec],
        out_shape=[out_sds, out_sds],
        compiler_params=_params(("parallel", "parallel", "arbitrary")),
        name=f"dilated_attention_g{group}",
    )(views[0], views[1], views[1], views[2], views[2])
    return o.reshape(batch * sub_len, dilation * gw), lse.reshape(batch * sub_len, dilation * gw)


def _dil_merge_kernel(*refs, dilations):
    n = len(dilations)
    o_refs, l_refs, out_ref, scratch = refs[:n], refs[n:2 * n], refs[2 * n], refs[2 * n + 1:]
    rows, gw = out_ref.shape
    outs, lses = [], []
    for g, dilation in enumerate(dilations):
        vals = []
        for src, scr in ((o_refs[g], scratch[2 * g]), (l_refs[g], scratch[2 * g + 1])):
            if dilation == 1:
                vals.append(src[...])
            else:
                for c in range(dilation):
                    for j in range(gw // LANES):
                        scr[j, pl.ds(c, rows // dilation, stride=dilation), :] = (
                            src[:, c * gw + j * LANES:c * gw + (j + 1) * LANES])
                vals.append(jnp.concatenate([scr[j] for j in range(gw // LANES)], axis=1))
        outs.append(vals[0])
        lses.append(vals[1])
    m = functools.reduce(jnp.maximum, lses)
    es = [jnp.exp(l - m) for l in lses]
    num = functools.reduce(lambda a, b: a + b, [e * o for e, o in zip(es, outs)])
    den = functools.reduce(lambda a, b: a + b, es)
    out_ref[...] = (num / den).astype(out_ref.dtype)


def dilated_merge(outs, lses, dilations, rows=512):
    gw = DIL_HEADS_PER_GROUP * HEAD_DIM
    t = outs[0].shape[0] * dilations[0]
    rows = min(rows, t)
    specs = [pl.BlockSpec((rows // r, r * gw), lambda i: (i, 0)) for r in dilations]
    return pl.pallas_call(
        functools.partial(_dil_merge_kernel, dilations=tuple(dilations)),
        grid=(t // rows,),
        in_specs=specs + specs,
        out_specs=pl.BlockSpec((rows, gw), lambda i: (i, 0)),
        out_shape=jax.ShapeDtypeStruct((t, gw), BF16),
        scratch_shapes=[pltpu.VMEM((gw // LANES, rows, LANES), F32) for _ in range(2 * len(dilations))],
        compiler_params=_params(("parallel",)),
        name="dilated_merge",
    )(*outs, *lses)


def _diff_kernel(lam_ref, g_ref, q_ref, k_ref, v_ref, o_ref, *, tq, heads, key_blocks, scale2, lam_init):
    i = pl.program_id(2)
    tk = key_blocks * tq
    lp = lam_ref[...]
    lam = (jnp.exp(jnp.sum(lp[0:1] * lp[1:2], axis=1, keepdims=True))
           - jnp.exp(jnp.sum(lp[2:3] * lp[3:4], axis=1, keepdims=True)) + lam_init)
    lane = lax.broadcasted_iota(jnp.int32, (tq, LANES), 1)
    qqs = []
    for h in range(heads):
        q = q_ref[:, h * LANES:(h + 1) * LANES].astype(F32)
        qqs.append(jnp.concatenate([jnp.where(lane < DIFF_DIM, q, 0.0), jnp.where(lane >= DIFF_DIM, q, 0.0)],
                                   axis=0).astype(BF16))
    row = lax.broadcasted_iota(jnp.int32, (2 * tq, tk), 0)
    row = jnp.where(row >= tq, row - tq, row)
    col = lax.broadcasted_iota(jnp.int32, (2 * tq, tk), 1)
    kd = i // key_blocks
    causal = col <= row + (i - key_blocks * kd) * tq
    dims = (((1,), (1,)), ((), ()))

    def block(kb, state, diagonal):
        start = pl.multiple_of(kb * tk, tk)
        out = []
        for h in range(heads):
            m, l, acc = state[h]
            sl = slice(h * LANES, (h + 1) * LANES)
            k = k_ref[pl.ds(start, tk), sl]
            v = v_ref[pl.ds(start, tk), sl]
            s = lax.dot_general(qqs[h], k, dims, preferred_element_type=F32) * scale2
            if diagonal:
                s = jnp.where(causal, s, NEG_BIG)
            m_new = jnp.maximum(m, jnp.max(s, axis=1, keepdims=True))
            alpha = jnp.exp2(m - m_new)
            p = jnp.exp2(s - m_new)
            l = alpha * l + jnp.sum(p, axis=1, keepdims=True)
            acc = alpha * acc + jnp.dot(p.astype(BF16), v, preferred_element_type=F32)
            out.append((m_new, l, acc))
        return tuple(out)

    state = tuple((jnp.full((2 * tq, 1), NEG_BIG, F32), jnp.zeros((2 * tq, 1), F32),
                   jnp.zeros((2 * tq, LANES), F32)) for _ in range(heads))
    state = block(kd, state, True)
    state = lax.fori_loop(0, kd, lambda n, st: block(n, st, False), state)
    for h in range(heads):
        _, l, acc = state[h]
        o_all = acc / l
        o = o_all[:tq] - lam * o_all[tq:]
        ms = jnp.mean(o * o, axis=-1, keepdims=True)
        y = o * lax.rsqrt(ms + RMS_EPS) * g_ref[...]
        o_ref[:, h * LANES:(h + 1) * LANES] = (y * (1.0 - lam_init)).astype(o_ref.dtype)


def differential_attention(proj, lam_params, subln_g, seq_len, q_col, k_col, v_col, lam_init, tq=1024, heads=2,
                           key_blocks=1):
    b = proj.shape[0]
    tq = min(tq, seq_len // key_blocks)
    hw = heads * LANES
    assert seq_len % (key_blocks * tq) == 0 and DIFF_HEADS % heads == 0
    assert q_col % hw == 0 and k_col % hw == 0 and v_col % hw == 0
    qb, kb, vb = q_col // hw, k_col // hw, v_col // hw
    kern = functools.partial(_diff_kernel, tq=tq, heads=heads, key_blocks=key_blocks,
                             scale2=DIFF_DIM ** -0.5 * LOG2_E, lam_init=lam_init)
    return pl.pallas_call(
        kern,
        grid=(b, DIFF_HEADS // heads, seq_len // tq),
        in_specs=[pl.BlockSpec((4, DIFF_DIM), lambda bi, h, i: (0, 0)),
                  pl.BlockSpec((1, 2 * DIFF_DIM), lambda bi, h, i: (0, 0)),
                  pl.BlockSpec((None, tq, hw), lambda bi, h, i: (bi, i, qb + h)),
                  pl.BlockSpec((None, seq_len, hw), lambda bi, h, i: (bi, 0, kb + h)),
                  pl.BlockSpec((None, seq_len, hw), lambda bi, h, i: (bi, 0, vb + h))],
        out_specs=pl.BlockSpec((None, tq, hw), lambda bi, h, i: (bi, i, h)),
        out_shape=jax.ShapeDtypeStruct((b, seq_len, DIFF_HEADS * 2 * DIFF_DIM), BF16),
        compiler_params=_params(("parallel", "parallel", "arbitrary")),
        name="differential_attention",
    )(lam_params, subln_g.reshape(1, -1), proj, proj, proj)


def _gate_merge_kernel(h_ref, wg0, wg1, wg2, o0, o1, o2, wb0, wb1, wb2, out_ref):
    h = h_ref[...]
    acc = None
    for wg, o, wb in ((wg0, o0, wb0), (wg1, o1, wb1), (wg2, o2, wb2)):
        gate = 1.0 / (1.0 + jnp.exp(-jnp.dot(h, wg[...], preferred_element_type=F32)))
        term = gate * jnp.dot(o[...], wb[...], preferred_element_type=F32)
        acc = term if acc is None else acc + term
    out_ref[...] = acc.astype(out_ref.dtype)


def gate_merge(h, w_gate, branch_outs, branch_ws, layer, tm=1024, tn=256):
    t, d = h.shape
    tm, tn = min(tm, t), min(tn, d)
    nj = d // tn
    in_specs = [pl.BlockSpec((tm, d), lambda i, j: (i, 0))]
    in_specs += [pl.BlockSpec((None, d, tn), functools.partial(lambda i, j, b: (layer, 0, b * nj + j), b=b))
                 for b in range(N_BRANCH)]
    in_specs += [pl.BlockSpec((tm, o.shape[1]), lambda i, j: (i, 0)) for o in branch_outs]
    in_specs += [pl.BlockSpec((None, w.shape[1], tn), lambda i, j: (layer, 0, j)) for w in branch_ws]
    return pl.pallas_call(
        _gate_merge_kernel,
        grid=(t // tm, nj),
        in_specs=in_specs,
        out_specs=pl.BlockSpec((tm, tn), lambda i, j: (i, j)),
        out_shape=jax.ShapeDtypeStruct((t, d), BF16),
        compiler_params=_params(("parallel", "arbitrary")),
        name="gate_merge",
    )(h, w_gate, w_gate, w_gate, *branch_outs, *branch_ws)


STAT_TAU, STAT_MAX1, STAT_MAX2, STAT_INVZ = 0, 1, 2, 3
STAT_ROWS = 8


def _top_values(x, scr, count):
    for kk in range(count):
        m = jnp.max(x, axis=0, keepdims=True)
        scr[kk:kk + 1, :] = m
        x = jnp.where(x == m, -jnp.inf, x)


def _peer_route_kernel(q_ref, keys_ref, s1_ref, s2_ref, stat_ref, a_scr, b_scr, c_scr, t_scr):
    dims = (((1,), (1,)), ((), ()))
    s1 = lax.dot_general(keys_ref[0], q_ref[:, :PEER_HALF_QDIM], dims, preferred_element_type=F32)
    s2 = lax.dot_general(keys_ref[1], q_ref[:, PEER_HALF_QDIM:], dims, preferred_element_type=F32)
    s1_ref[...] = s1
    s2_ref[...] = s2
    _top_values(s1, a_scr, PEER_TOPK)
    _top_values(s2, b_scr, PEER_TOPK)
    half = PEER_TOPK // 2
    c_scr[0:PEER_TOPK, :] = a_scr[0:1, :] + b_scr[...]
    for ii in range(1, half):
        c_scr[PEER_TOPK + (ii - 1) * half:PEER_TOPK + ii * half, :] = a_scr[ii:ii + 1, :] + b_scr[0:half, :]
    c_scr[PEER_TOPK + (half - 1) * half:PEER_CANDIDATES, :] = a_scr[half:PEER_TOPK, :] + b_scr[0:1, :]
    cand = c_scr[...]
    _top_values(cand, t_scr, PEER_TOPK)
    tau = t_scr[PEER_TOPK - 1:PEER_TOPK, :]
    best = t_scr[0:1, :]
    z = jnp.sum(jnp.where(cand >= tau, jnp.exp(cand - best), 0.0), axis=0, keepdims=True)
    stat_ref[...] = jnp.zeros_like(stat_ref)
    stat_ref[STAT_TAU:STAT_TAU + 1, :] = tau
    stat_ref[STAT_MAX1:STAT_MAX1 + 1, :] = a_scr[0:1, :]
    stat_ref[STAT_MAX2:STAT_MAX2 + 1, :] = b_scr[0:1, :]
    stat_ref[STAT_INVZ:STAT_INVZ + 1, :] = 1.0 / z


def peer_route(q, sub_keys, layer, tb=256):
    t = q.shape[0]
    tb = min(tb, t)
    score_sds = jax.ShapeDtypeStruct((PEER_HEADS, PEER_NKEYS, t), F32)
    score_spec = pl.BlockSpec((None, PEER_NKEYS, tb), lambda i, h: (h, 0, i))
    return pl.pallas_call(
        _peer_route_kernel,
        grid=(t // tb, PEER_HEADS),
        in_specs=[pl.BlockSpec((tb, 2 * PEER_HALF_QDIM), lambda i, h: (i, h)),
                  pl.BlockSpec((None, None, 2, PEER_NKEYS, PEER_HALF_QDIM), lambda i, h: (layer, h, 0, 0, 0))],
        out_specs=[score_spec, score_spec, pl.BlockSpec((None, STAT_ROWS, tb), lambda i, h: (h, 0, i))],
        out_shape=[score_sds, score_sds, jax.ShapeDtypeStruct((PEER_HEADS, STAT_ROWS, t), F32)],
        scratch_shapes=[pltpu.VMEM((PEER_TOPK, tb), F32), pltpu.VMEM((PEER_TOPK, tb), F32),
                        pltpu.VMEM((PEER_CANDIDATES, tb), F32), pltpu.VMEM((PEER_TOPK, tb), F32)],
        compiler_params=_params(("parallel", "arbitrary")),
        name="peer_route",
    )(q, sub_keys)


def _gelu(a):
    return 0.5 * a * (1.0 + lax.erf(a * (2.0 ** -0.5)))


PEER_STAGES = 3
PEER_EXPERT_BLOCK = 512
GATE_KEY_SPLITS = 8


def _peer_dense_kernel(ht_ref, u_ref, vt_ref, s1_ref, s2_ref, stat_ref, out_ref, e2_scr,
                       act_a, act_b, coef_a, coef_b, *, te, n_blocks):
    e = pl.program_id(1)

    @pl.when(e == 0)
    def _():
        out_ref[...] = jnp.zeros_like(out_ref)
        for h in range(PEER_HEADS):
            e2_scr[h] = jnp.exp(s2_ref[h] - stat_ref[h, STAT_MAX2:STAT_MAX2 + 1, :])

    n_sub = te // PEER_NKEYS
    d_model, tb = ht_ref.shape
    gate_block = e - 1

    def stages(buffers, scores=True, gates=True, output=True):
        act_cur, act_prev, coef_cur, coef_prev = buffers
        n_lane = tb // LANES
        n_slices = GATE_KEY_SPLITS * n_lane
        kc = d_model // n_slices
        if scores:
            act_cur[...] = jnp.dot(u_ref[...], ht_ref[...], preferred_element_type=F32)
        s1_rows, e1_rows = [], []
        for sub in range(n_sub if gates else 0):
            i_idx = gate_block * n_sub + sub
            s1_rows.append([s1_ref[h, pl.ds(i_idx, 1), :] for h in range(PEER_HEADS)])
            e1_rows.append([jnp.exp(s1_rows[sub][h] - stat_ref[h, STAT_MAX1:STAT_MAX1 + 1, :])
                            * stat_ref[h, STAT_INVZ:STAT_INVZ + 1, :] for h in range(PEER_HEADS)])
        half = PEER_NKEYS // GATE_KEY_SPLITS
        for r in range(n_slices):
            c, jh = divmod(r, GATE_KEY_SPLITS)
            lanes = slice(c * LANES, (c + 1) * LANES)
            keys = slice(jh * half, (jh + 1) * half)
            tiles = [None] * n_sub
            for h in range(PEER_HEADS if gates else 0):
                s2_tile = s2_ref[h, keys, lanes]
                e2_tile = e2_scr[h, keys, lanes]
                tau = stat_ref[h, STAT_TAU:STAT_TAU + 1, lanes]
                for sub in range(n_sub):
                    term = jnp.where(s2_tile + s1_rows[sub][h][:, lanes] >= tau,
                                     e2_tile * e1_rows[sub][h][:, lanes], 0.0)
                    tiles[sub] = term if tiles[sub] is None else tiles[sub] + term
            for sub in range(n_sub if gates else 0):
                rows = slice(sub * PEER_NKEYS + jh * half, sub * PEER_NKEYS + (jh + 1) * half)
                coef_prev[rows, lanes] = (tiles[sub] * _gelu(act_prev[rows, lanes])).astype(BF16)
            if output:
                chunk = slice(r * kc, (r + 1) * kc)
                out_ref[chunk, :] += jnp.dot(vt_ref[chunk, :], coef_cur[...], preferred_element_type=F32)

    even, odd = (act_a, act_b, coef_a, coef_b), (act_b, act_a, coef_b, coef_a)
    final = n_blocks + PEER_STAGES - 2
    assert final >= PEER_STAGES
    edge_steps = {0: dict(gates=False, output=False), 1: dict(output=False),
                  final - 1: dict(scores=False), final: dict(scores=False, gates=False)}
    for step, flags in edge_steps.items():
        pl.when(e == step)(functools.partial(stages, even if step % 2 == 0 else odd, **flags))
    interior = (e > 1) & (e < final - 1)
    parity = lax.rem(e, 2)
    pl.when(interior & (parity == 0))(functools.partial(stages, even))
    pl.when(interior & (parity == 1))(functools.partial(stages, odd))


def peer_dense(ht, u_tab, vt_blocks, layer, s1, s2, stats, tb=512):
    d, t = ht.shape
    _, n_blocks, _, te = vt_blocks.shape
    tb = min(tb, t)
    assert t % tb == 0 and u_tab.shape[1] == n_blocks * te and te % PEER_NKEYS == 0
    once = pl.Buffered(1)
    tok_spec = pl.BlockSpec((PEER_HEADS, PEER_NKEYS, tb), lambda i, e: (0, 0, i), pipeline_mode=once)
    last = n_blocks - 1
    kern = functools.partial(_peer_dense_kernel, te=te, n_blocks=n_blocks)
    return pl.pallas_call(
        kern,
        grid=(t // tb, n_blocks + PEER_STAGES - 1),
        in_specs=[pl.BlockSpec((d, tb), lambda i, e: (0, i), pipeline_mode=once),
                  pl.BlockSpec((None, te, d), lambda i, e: (layer, jnp.minimum(e, last), 0)),
                  pl.BlockSpec((None, None, d, te), lambda i, e: (layer, jnp.clip(e - 2, 0, last), 0, 0)),
                  tok_spec, tok_spec,
                  pl.BlockSpec((PEER_HEADS, STAT_ROWS, tb), lambda i, e: (0, 0, i), pipeline_mode=once)],
        out_specs=pl.BlockSpec((d, tb), lambda i, e: (0, i)),
        out_shape=jax.ShapeDtypeStruct((d, t), F32),
        scratch_shapes=[pltpu.VMEM((PEER_HEADS, PEER_NKEYS, tb), F32),
                        pltpu.VMEM((te, tb), F32), pltpu.VMEM((te, tb), F32),
                        pltpu.VMEM((te, tb), BF16), pltpu.VMEM((te, tb), BF16)],
        compiler_params=_params(("parallel", "arbitrary")),
        name="peer_dense",
    )(ht, u_tab, vt_blocks, s1, s2, stats)


def kernel(x, attn_norm_g, ffn_norm_g, final_norm_g, w_qkv, w_gate, w_branch_sb, w_branch_dil,
           w_branch_diff, w_out, diff_lambda, diff_subln_g, peer_w_q, peer_sub_keys, peer_u, peer_v):
    b, s, d = x.shape
    t = b * s
    depth = w_qkv.shape[0]
    sb_w = SB_HEADS * HEAD_DIM
    dil_w = DIL_HEADS_PER_GROUP * len(DIL_GROUPS) * HEAD_DIM
    diff_w = DIFF_HEADS * 2 * DIFF_DIM
    dl_q, dl_k, dl_v = 3 * sb_w, 3 * sb_w + dil_w, 3 * sb_w + 2 * dil_w
    df_q = 3 * sb_w + 3 * dil_w
    df_k, df_v = df_q + diff_w, df_q + 2 * diff_w
    qkv_cols = df_v + diff_w
    rope_ops = _rope_operands(s)
    w_qkv_b, w_gate_b, w_out_b, w_pq_b = (w.astype(BF16) for w in (w_qkv, w_gate, w_out, peer_w_q))
    w_branch_b = tuple(w.astype(BF16) for w in (w_branch_sb, w_branch_dil, w_branch_diff))
    keys_b, u_b = peer_sub_keys.astype(BF16), peer_u.astype(BF16)
    vt_blocks = peer_v.reshape(depth, -1, PEER_EXPERT_BLOCK, d).transpose(0, 1, 3, 2).astype(BF16)

    xt = x.reshape(t, d)
    peer_out_t = None
    for layer in range(depth):
        if peer_out_t is None:
            h = rmsnorm(xt, attn_norm_g[layer], BF16)
        else:
            h, xt = residual_rmsnorm(xt, peer_out_t, attn_norm_g[layer], BF16, return_sum=True)
        proj = qkv_projection(h, w_qkv_b, layer, rope_ops, s,
                              rope128_cols=(dl_q, dl_v), rope64_cols=(df_q, df_v))
        proj3 = proj.reshape(b, s, qkv_cols)
        o_sb = stick_breaking_attention(proj3, s, 0).reshape(t, sb_w)
        dil = []
        gw = DIL_HEADS_PER_GROUP * HEAD_DIM
        for g, (_, dilation) in enumerate(DIL_GROUPS):
            cols = (dl_q + g * gw, dl_k + g * gw, dl_v + g * gw)
            if dilation == 1:
                dil.append(dilated_group_attention(proj, proj, proj, b, s, g, dilation, *cols))
            else:
                qkv_views = dilated_regroup(proj, dilation, *cols)
                dil.append(dilated_group_attention(*qkv_views, b, s, g, dilation, 0, 0, 0))
        o_dl = dilated_merge([o for o, _ in dil], [l for _, l in dil], [r for _, r in DIL_GROUPS])
        lam_init = 0.8 - 0.6 * math.exp(-0.3 * layer)
        o_df = differential_attention(proj3, diff_lambda[layer], diff_subln_g[layer], s,
                                      df_q, df_k, df_v, lam_init).reshape(t, diff_w)
        merged = gate_merge(h, w_gate_b, (o_sb, o_dl, o_df), w_branch_b, layer)
        xt = matmul(merged, w_out_b, layer, F32, residual=xt)

        h2, h2_t = rmsnorm(xt, ffn_norm_g[layer], BF16, with_transpose=True)
        q = matmul(h2, w_pq_b, layer, BF16)
        s1, s2, stats = peer_route(q, keys_b, layer)
        peer_out_t = peer_dense(h2_t, u_b, vt_blocks, layer, s1, s2, stats)
    (out,) = residual_rmsnorm(xt, peer_out_t, final_norm_g, F32, return_sum=False)
    return out.reshape(b, s, d)
```
